```python
import jax
import jax.numpy as jnp
from jax import lax
import numpy as np

D_MODEL = 2048
BATCH = 2
SEQ = 8192
DEPTH = 4

MLA_HEADS = 8
MLA_Q_LORA = 512
MLA_KV_LORA = 512
MLA_NOPE_DIM = 128
MLA_ROPE_DIM = 64
MLA_V_DIM = 128
MLA_QK_DIM = MLA_NOPE_DIM + MLA_ROPE_DIM
ROPE_THETA = 10000.0
Q_BLOCK = 128

GDN_HEADS = 8
GDN_K_DIM = 128
GDN_V_DIM = 128
GDN_CONV = 4
GDN_CHUNK = 64

IN_SPLITS = (
    MLA_Q_LORA,
    MLA_KV_LORA,
    MLA_ROPE_DIM,
    GDN_HEADS * GDN_K_DIM,
    GDN_HEADS * GDN_K_DIM,
    GDN_HEADS * GDN_V_DIM,
    GDN_HEADS * GDN_V_DIM,
    GDN_HEADS,
    GDN_HEADS,
)
IN_COLS = sum(IN_SPLITS)
MIX_WIDTH = MLA_HEADS * MLA_V_DIM + GDN_HEADS * GDN_V_DIM

RWKV_HEAD = 64
RWKV_HEADS = D_MODEL // RWKV_HEAD
RWKV_DECAY_LORA = max(32, round(1.8 * D_MODEL ** 0.5 / 32) * 32)
RWKV_AAA_LORA = max(32, round(1.8 * D_MODEL ** 0.5 / 32) * 32)
RWKV_MV_LORA = max(32, round(1.3 * D_MODEL ** 0.5 / 32) * 32)
RWKV_GATE_LORA = max(32, round(0.6 * D_MODEL ** 0.8 / 32) * 32)
RWKV_LN_EPS = 64e-5

FFN_HIDDEN = -(-8 * D_MODEL // (3 * 256)) * 256
RMS_EPS = 1e-6

kernel_name = "hybrid_mla_gdn_rwkv7_sandwich"


def rms_norm(x, w, eps=RMS_EPS):
    xf = x.astype(jnp.float32)
    y = xf * lax.rsqrt(jnp.mean(xf * xf, axis=-1, keepdims=True) + eps)
    return (y * w.astype(jnp.float32)).astype(x.dtype)


def l2_normalize(x, eps=1e-6):
    xf = x.astype(jnp.float32)
    return xf * lax.rsqrt(jnp.sum(xf * xf, axis=-1, keepdims=True) + eps)


def rope_tables(positions, dtype):
    inv_freq = 1.0 / (ROPE_THETA ** (jnp.arange(0, MLA_ROPE_DIM, 2, dtype=jnp.float32) / MLA_ROPE_DIM))
    ang = positions.astype(jnp.float32)[..., None] * inv_freq
    return jnp.cos(ang).astype(dtype), jnp.sin(ang).astype(dtype)


def apply_rope(x, cos, sin):
    x1, x2 = jnp.split(x, 2, axis=-1)
    return jnp.concatenate([x1 * cos - x2 * sin, x2 * cos + x1 * sin], axis=-1)


def causal_depthwise_conv(x, w):
    k_width, ch = w.shape
    return lax.conv_general_dilated(
        x, w[:, None, :].astype(x.dtype), window_strides=(1,), padding=[(k_width - 1, 0)],
        dimension_numbers=("NWC", "WIO", "NWC"), feature_group_count=ch)


def mla_causal_attention(q_nope, q_rope, k_nope, k_rope, v):
    b, s, h, dn = q_nope.shape
    nb = s // Q_BLOCK
    scale = MLA_QK_DIM ** -0.5
    qn = jnp.moveaxis(q_nope.reshape(b, nb, Q_BLOCK, h, dn), 1, 0)
    qr = jnp.moveaxis(q_rope.reshape(b, nb, Q_BLOCK, h, MLA_ROPE_DIM), 1, 0)
    k_pos = jnp.arange(s)

    def one_block(args):
        i, qn_b, qr_b = args
        sc = (jnp.einsum("bqhd,bkhd->bhqk", qn_b, k_nope)
              + jnp.einsum("bqhr,bkr->bhqk", qr_b, k_rope)).astype(jnp.float32) * scale
        q_pos = i * Q_BLOCK + jnp.arange(Q_BLOCK)
        sc = jnp.where(k_pos[None, :] <= q_pos[:, None], sc, -jnp.inf)
        p = jax.nn.softmax(sc, axis=-1).astype(v.dtype)
        return jnp.einsum("bhqk,bkhd->bqhd", p, v)

    out = lax.map(one_block, (jnp.arange(nb), qn, qr))
    return jnp.moveaxis(out, 0, 1).reshape(b, s, h, v.shape[-1])


def gated_delta_rule_chunked(q, k, v, g, beta):
    b, s, h, dk = q.shape
    dv = v.shape[-1]
    c = GDN_CHUNK
    nc = s // c

    def to_chunks(t):
        return jnp.moveaxis(t.reshape((b, nc, c) + t.shape[2:]), 3, 1)

    q, k, v = to_chunks(q * dk ** -0.5), to_chunks(k), to_chunks(v)
    g, beta = to_chunks(g), to_chunks(beta)
    gam = jnp.cumsum(g, axis=-1)
    incl = jnp.tril(jnp.ones((c, c), dtype=bool))
    strict = jnp.tril(jnp.ones((c, c), dtype=bool), -1)
    decay = jnp.exp(jnp.where(incl, gam[..., :, None] - gam[..., None, :], -jnp.inf))
    kb = k * beta[..., None]
    vb = v * beta[..., None]
    a_mat = jnp.where(strict, jnp.einsum("bhnik,bhnjk->bhnij", kb, k) * decay, 0.0)
    rhs = jnp.concatenate([vb, kb * jnp.exp(gam)[..., None]], axis=-1)
    sol = lax.linalg.triangular_solve(a_mat + jnp.eye(c, dtype=a_mat.dtype), rhs,
                                      left_side=True, lower=True, unit_diagonal=True)
    u, w = sol[..., :dv], sol[..., dv:]
    attn = jnp.einsum("bhnik,bhnjk->bhnij", q, k) * decay
    q_dec = q * jnp.exp(gam)[..., None]
    k_dec = k * jnp.exp(gam[..., -1:] - gam)[..., None]
    last = jnp.exp(gam[..., -1])

    def step(state, inp):
        qd, at, u_c, w_c, kd, ld = inp
        e = u_c - jnp.einsum("bhck,bhkv->bhcv", w_c, state)
        o = jnp.einsum("bhck,bhkv->bhcv", qd, state) + jnp.einsum("bhij,bhjv->bhiv", at, e)
        state = state * ld[..., None, None] + jnp.einsum("bhck,bhcv->bhkv", kd, e)
        return state, o

    xs = tuple(jnp.moveaxis(t, 2, 0) for t in (q_dec, attn, u, w, k_dec, last))
    _, o = lax.scan(step, jnp.zeros((b, h, dk, dv), jnp.float32), xs)
    return o.transpose(1, 0, 3, 2, 4).reshape(b, s, h, dv)


def hybrid_mla_gdn_mixer(xn, positions, w_in, q_norm, w_uq, kv_norm, w_ukv,
                         conv_w, a_log, dt_bias, out_norm, w_out):
    b, s, _ = xn.shape
    dt = xn.dtype
    hcat = xn @ w_in
    offs = np.cumsum(IN_SPLITS)[:-1].tolist()
    c_q, c_kv, k_rope, gq, gk, gv, gz, gb, ga = jnp.split(hcat, offs, axis=-1)

    q = (rms_norm(c_q, q_norm) @ w_uq).reshape(b, s, MLA_HEADS, MLA_QK_DIM)
    kv = (rms_norm(c_kv, kv_norm) @ w_ukv).reshape(b, s, MLA_HEADS, MLA_NOPE_DIM + MLA_V_DIM)
    q_nope, q_rope = q[..., :MLA_NOPE_DIM], q[..., MLA_NOPE_DIM:]
    k_nope, v_mla = kv[..., :MLA_NOPE_DIM], kv[..., MLA_NOPE_DIM:]
    cos, sin = rope_tables(positions, dt)
    q_rope = apply_rope(q_rope, cos[:, :, None, :], sin[:, :, None, :])
    k_rope = apply_rope(k_rope, cos, sin)
    mla_out = mla_causal_attention(q_nope, q_rope, k_nope, k_rope, v_mla)
    mla_out = mla_out.reshape(b, s, MLA_HEADS * MLA_V_DIM)

    qkv = jax.nn.silu(causal_depthwise_conv(jnp.concatenate([gq, gk, gv], axis=-1), conv_w))
    hq, hk, hv = jnp.split(qkv, [GDN_HEADS * GDN_K_DIM, 2 * GDN_HEADS * GDN_K_DIM], axis=-1)
    hq = l2_normalize(hq.reshape(b, s, GDN_HEADS, GDN_K_DIM))
    hk = l2_normalize(hk.reshape(b, s, GDN_HEADS, GDN_K_DIM))
    hv = hv.reshape(b, s, GDN_HEADS, GDN_V_DIM).astype(jnp.float32)
    beta = jax.nn.sigmoid(gb.astype(jnp.float32))
    g = -jnp.exp(a_log.astype(jnp.float32)) * jax.nn.softplus(
        ga.astype(jnp.float32) + dt_bias.astype(jnp.float32))
    o = gated_delta_rule_chunked(hq, hk, hv, g, beta)
    z = gz.reshape(b, s, GDN_HEADS, GDN_V_DIM)
    gdn_out = (rms_norm(o, out_norm).astype(dt) * jax.nn.silu(z)).reshape(b, s, GDN_HEADS * GDN_V_DIM)

    return jnp.concatenate([mla_out, gdn_out], axis=-1) @ w_out


def rwkv7_recurrence(r, decay, k, v, kk, a):
    b, s, h, n = r.shape

    def step(state, inp):
        r_t, d_t, k_t, v_t, kk_t, a_t = inp
        sk = jnp.einsum("bhvk,bhk->bhv", state, kk_t)
        state = (state * d_t[:, :, None, :]
                 - sk[..., None] * (kk_t * a_t)[:, :, None, :]
                 + v_t[..., None] * k_t[:, :, None, :])
        return state, jnp.einsum("bhvk,bhk->bhv", state, r_t)

    xs = tuple(jnp.moveaxis(t, 1, 0) for t in (r, decay, k, v, kk, a))
    _, y = lax.scan(step, jnp.zeros((b, h, n, n), jnp.float32), xs)
    return jnp.moveaxis(y, 0, 1)


def rwkv7_time_mix(xn, v_first, mix, w_r, w_k, w_v, w_o, w0, w1, w2, a0, a1, a2,
                   g1, g2, k_k, k_a, r_k, ln_w, ln_b, vres):
    b, s, d = xn.shape
    dt = xn.dtype
    x_prev = jnp.pad(xn, ((0, 0), (1, 0), (0, 0)))[:, :-1]
    xx = x_prev - xn
    xr, xw, xk, xv, xa, xg = (xn + xx * mix[i] for i in range(6))

    r = xr @ w_r
    w_log = -jax.nn.softplus(-(w0 + jnp.tanh(xw @ w1) @ w2).astype(jnp.float32)) - 0.5
    k = xk @ w_k
    v = xv @ w_v
    if vres is None:
        v_first = v
    else:
        v0, v1, v2 = vres
        v = v + (v_first - v) * jax.nn.sigmoid(v0 + (xv @ v1) @ v2)
    a = jax.nn.sigmoid((a0 + (xa @ a1) @ a2).astype(jnp.float32))
    gate = jax.nn.sigmoid(xg @ g1) @ g2

    heads = lambda t: t.reshape(b, s, RWKV_HEADS, RWKV_HEAD).astype(jnp.float32)
    kk = l2_normalize(heads(k * k_k), eps=1e-12)
    a_h = heads(a)
    k_a_h = k_a.reshape(RWKV_HEADS, RWKV_HEAD).astype(jnp.float32)
    k_h = heads(k) * (1.0 + (a_h - 1.0) * k_a_h)
    r_h, v_h = heads(r), heads(v)
    decay = heads(jnp.exp(-jnp.exp(w_log)))
    y = rwkv7_recurrence(r_h, decay, k_h, v_h, kk, a_h)

    mu = jnp.mean(y, axis=-1, keepdims=True)
    var = jnp.mean(jnp.square(y - mu), axis=-1, keepdims=True)
    y = ((y - mu) * lax.rsqrt(var + RWKV_LN_EPS)).reshape(b, s, d)
    y = y * ln_w.astype(jnp.float32) + ln_b.astype(jnp.float32)
    bonus = jnp.sum(r_h * k_h * r_k.astype(jnp.float32), axis=-1, keepdims=True) * v_h
    y = (y + bonus.reshape(b, s, d)).astype(dt)
    return (y * gate) @ w_o, v_first


def swiglu(x, w_gate, w_up, w_down):
    return (jax.nn.silu(x @ w_gate) * (x @ w_up)) @ w_down


def setup_inputs(seed: int = 0) -> dict:
    key = jax.random.key(seed)
    ks = iter(jax.random.split(key, 64))
    ne, no = (DEPTH + 1) // 2, DEPTH // 2
    nv = max(no - 1, 0)
    d = D_MODEL

    def nrm(shape, scale):
        return scale * jax.random.normal(next(ks), shape, jnp.float32)

    def unif(shape, lo, hi):
        return jax.random.uniform(next(ks), shape, jnp.float32, lo, hi)

    def gain(shape):
        return 1.0 + nrm(shape, 0.02)

    x = jax.random.normal(next(ks), (BATCH, SEQ, d), jnp.float32)
    positions = jnp.broadcast_to(jnp.arange(SEQ, dtype=jnp.int32), (BATCH, SEQ))
    dt0 = jnp.exp(unif((ne, GDN_HEADS), float(np.log(1e-3)), float(np.log(1e-1))))
    return {
        "x": x,
        "positions": positions,
        "norm_mix_pre": gain((DEPTH, d)),
        "norm_mix_post": gain((DEPTH, d)),
        "norm_ffn_pre": gain((DEPTH, d)),
        "norm_ffn_post": gain((DEPTH, d)),
        "hyb_w_in": nrm((ne, d, IN_COLS), d ** -0.5),
        "mla_q_norm": gain((ne, MLA_Q_LORA)),
        "mla_w_uq": nrm((ne, MLA_Q_LORA, MLA_HEADS * MLA_QK_DIM), MLA_Q_LORA ** -0.5),
        "mla_kv_norm": gain((ne, MLA_KV_LORA)),
        "mla_w_ukv": nrm((ne, MLA_KV_LORA, MLA_HEADS * (MLA_NOPE_DIM + MLA_V_DIM)), MLA_KV_LORA ** -0.5),
        "gdn_conv_w": nrm((ne, GDN_CONV, GDN_HEADS * (2 * GDN_K_DIM + GDN_V_DIM)), GDN_CONV ** -0.5),
        "gdn_a_log": jnp.log(unif((ne, GDN_HEADS), 1.0, 16.0)),
        "gdn_dt_bias": dt0 + jnp.log(-jnp.expm1(-dt0)),
        "gdn_out_norm": gain((ne, GDN_V_DIM)),
        "hyb_w_out": nrm((ne, MIX_WIDTH, d), MIX_WIDTH ** -0.5),
        "rwkv_mix": unif((no, 6, d), 0.0, 1.0),
        "rwkv_w_r": nrm((no, d, d), d ** -0.5),
        "rwkv_w_k": nrm((no, d, d), d ** -0.5),
        "rwkv_w_v": nrm((no, d, d), d ** -0.5),
        "rwkv_w_o": nrm((no, d, d), d ** -0.5),
        "rwkv_w0": unif((no, d), -6.0, -1.0),
        "rwkv_w1": nrm((no, d, RWKV_DECAY_LORA), d ** -0.5),
        "rwkv_w2": nrm((no, RWKV_DECAY_LORA, d), 0.1 * RWKV_DECAY_LORA ** -0.5),
        "rwkv_a0": nrm((no, d), 0.1),
        "rwkv_a1": nrm((no, d, RWKV_AAA_LORA), d ** -0.5),
        "rwkv_a2": nrm((no, RWKV_AAA_LORA, d), 0.1 * RWKV_AAA_LORA ** -0.5),
        "rwkv_g1": nrm((no, d, RWKV_GATE_LORA), d ** -0.5),
        "rwkv_g2": nrm((no, RWKV_GATE_LORA, d), RWKV_GATE_LORA ** -0.5),
        "rwkv_k_k": 0.85 + nrm((no, d), 0.02),
        "rwkv_k_a": gain((no, d)),
        "rwkv_r_k": nrm((no, RWKV_HEADS, RWKV_HEAD), 0.1),
        "rwkv_ln_w": gain((no, d)),
        "rwkv_ln_b": nrm((no, d), 0.02),
        "rwkv_v0": 1.0 + nrm((nv, d), 0.1),
        "rwkv_v1": nrm((nv, d, RWKV_MV_LORA), d ** -0.5),
        "rwkv_v2": nrm((nv, RWKV_MV_LORA, d), 0.1 * RWKV_MV_LORA ** -0.5),
        "ffn_w_gate": nrm((DEPTH, d, FFN_HIDDEN), d ** -0.5),
        "ffn_w_up": nrm((DEPTH, d, FFN_HIDDEN), d ** -0.5),
        "ffn_w_down": nrm((DEPTH, FFN_HIDDEN, d), FFN_HIDDEN ** -0.5),
    }


def reference(x, positions, norm_mix_pre, norm_mix_post, norm_ffn_pre, norm_ffn_post,
              hyb_w_in, mla_q_norm, mla_w_uq, mla_kv_norm, mla_w_ukv, gdn_conv_w,
              gdn_a_log, gdn_dt_bias, gdn_out_norm, hyb_w_out,
              rwkv_mix, rwkv_w_r, rwkv_w_k, rwkv_w_v, rwkv_w_o, rwkv_w0, rwkv_w1, rwkv_w2,
              rwkv_a0, rwkv_a1, rwkv_a2, rwkv_g1, rwkv_g2, rwkv_k_k, rwkv_k_a, rwkv_r_k,
              rwkv_ln_w, rwkv_ln_b, rwkv_v0, rwkv_v1, rwkv_v2,
              ffn_w_gate, ffn_w_up, ffn_w_down):
    h = x
    v_first = None
    for layer in range(DEPTH):
        xn = rms_norm(h, norm_mix_pre[layer])
        if layer % 2 == 0:
            e = layer // 2
            mix_out = hybrid_mla_gdn_mixer(
                xn, positions, hyb_w_in[e], mla_q_norm[e], mla_w_uq[e], mla_kv_norm[e],
                mla_w_ukv[e], gdn_conv_w[e], gdn_a_log[e], gdn_dt_bias[e], gdn_out_norm[e],
                hyb_w_out[e])
        else:
            o = layer // 2
            vres = None if o == 0 else (rwkv_v0[o - 1], rwkv_v1[o - 1], rwkv_v2[o - 1])
            mix_out, v_first = rwkv7_time_mix(
                xn, v_first, rwkv_mix[o], rwkv_w_r[o], rwkv_w_k[o], rwkv_w_v[o], rwkv_w_o[o],
                rwkv_w0[o], rwkv_w1[o], rwkv_w2[o], rwkv_a0[o], rwkv_a1[o], rwkv_a2[o],
                rwkv_g1[o], rwkv_g2[o], rwkv_k_k[o], rwkv_k_a[o], rwkv_r_k[o],
                rwkv_ln_w[o], rwkv_ln_b[o], vres)
        h = h + rms_norm(mix_out, norm_mix_post[layer])
        f = swiglu(rms_norm(h, norm_ffn_pre[layer]), ffn_w_gate[layer], ffn_w_up[layer], ffn_w_down[layer])
        h = h + rms_norm(f, norm_ffn_post[layer])
    return h
```

```python
import functools

import jax
import jax.numpy as jnp
import numpy as np
from jax import lax
from jax.experimental import pallas as pl
from jax.experimental.pallas import tpu as pltpu

F32 = jnp.float32
BF16 = jnp.bfloat16
HI = lax.Precision.HIGHEST

D_MODEL = 2048
DEPTH = 4
LANES = 128
VMEM_LIMIT = 56 * 1024 * 1024

MLA_HEADS = 8
MLA_Q_LORA = 512
MLA_KV_LORA = 512
MLA_NOPE = 128
MLA_ROPE = 64
MLA_V = 128
MLA_QK = MLA_NOPE + MLA_ROPE
MLA_HEAD_PAD = 256
ROPE_THETA = 10000.0

GDN_HEADS = 8
GDN_DK = 128
GDN_DV = 128
GDN_CONV = 4
GDN_CHUNK = 128

RWKV_HEAD = 64
RWKV_HEADS = D_MODEL // RWKV_HEAD
RWKV_CHUNK = 64
RWKV_LN_EPS = 64e-5
RMS_EPS = 1e-6

HC_CQ, HC_CKV, HC_GQ, HC_GK, HC_GV, HC_GZ, HC_KR, HC_SM = 0, 512, 1024, 2048, 3072, 4096, 5120, 5248
HC_COLS = 5376


def _cparams(sem):
    return pltpu.CompilerParams(dimension_semantics=sem, vmem_limit_bytes=VMEM_LIMIT)


def _tile(n, pref):
    t = min(n, pref)
    while n % t:
        t -= 8
    return t


def _dot(a, b, prec=None):
    return jnp.dot(a, b, preferred_element_type=F32, precision=prec)


def _dot_nt(a, b, prec=None):
    return lax.dot_general(a, b, (((1,), (1,)), ((), ())), preferred_element_type=F32, precision=prec)


def _rms(x, g, eps=RMS_EPS):
    return x * lax.rsqrt(jnp.mean(x * x, axis=-1, keepdims=True) + eps) * g


def _sigmoid(x):
    return 1.0 / (1.0 + jnp.exp(-x))


def _silu(x):
    return x * _sigmoid(x)


def _softplus(x):
    return jnp.maximum(x, 0.0) + jnp.log(1.0 + jnp.exp(-jnp.abs(x)))


def _norm_mm_kernel(x_ref, g_ref, w_ref, o_ref, xn_ref):
    @pl.when(pl.program_id(1) == 0)
    def _():
        xn_ref[...] = _rms(x_ref[...], g_ref[...]).astype(BF16)

    o_ref[...] = _dot(xn_ref[...], w_ref[...]).astype(o_ref.dtype)


def norm_matmul(x, g, w, *, tn, out_dtype=F32, tm=512):
    m, k = x.shape
    n = w.shape[1]
    tm = _tile(m, tm)
    return pl.pallas_call(
        _norm_mm_kernel,
        grid=(m // tm, n // tn),
        in_specs=[pl.BlockSpec((tm, k), lambda i, j: (i, 0)),
                  pl.BlockSpec((1, k), lambda i, j: (0, 0)),
                  pl.BlockSpec((k, tn), lambda i, j: (0, j))],
        out_specs=pl.BlockSpec((tm, tn), lambda i, j: (i, j)),
        out_shape=jax.ShapeDtypeStruct((m, n), out_dtype),
        scratch_shapes=[pltpu.VMEM((tm, k), BF16)],
        compiler_params=_cparams(("parallel", "arbitrary")),
        name="norm_matmul",
    )(x, g, w)


def _ffn_up_kernel(x_ref, g_ref, wg_ref, wu_ref, o_ref, xn_ref):
    @pl.when(pl.program_id(1) == 0)
    def _():
        xn_ref[...] = _rms(x_ref[...], g_ref[...]).astype(BF16)

    xn = xn_ref[...]
    gate = _dot(xn, wg_ref[...])
    up = _dot(xn, wu_ref[...])
    o_ref[...] = (_silu(gate) * up).astype(o_ref.dtype)


def ffn_up(h, g, wg, wu, *, tm=512, tn=512):
    m, k = h.shape
    n = wg.shape[1]
    tm = _tile(m, tm)
    tn = _tile(n, tn)
    return pl.pallas_call(
        _ffn_up_kernel,
        grid=(m // tm, n // tn),
        in_specs=[pl.BlockSpec((tm, k), lambda i, j: (i, 0)),
                  pl.BlockSpec((1, k), lambda i, j: (0, 0)),
                  pl.BlockSpec((k, tn), lambda i, j: (0, j)),
                  pl.BlockSpec((k, tn), lambda i, j: (0, j))],
        out_specs=pl.BlockSpec((tm, tn), lambda i, j: (i, j)),
        out_shape=jax.ShapeDtypeStruct((m, n), BF16),
        scratch_shapes=[pltpu.VMEM((tm, k), BF16)],
        compiler_params=_cparams(("parallel", "arbitrary")),
        name="ffn_up",
    )(h, g, wg, wu)


def _mm_norm_res_kernel(a_ref, w_ref, g_ref, h_ref, o_ref, acc_ref):
    kk = pl.program_id(1)

    @pl.when(kk == 0)
    def _():
        acc_ref[...] = jnp.zeros_like(acc_ref)

    acc_ref[...] += _dot(a_ref[...], w_ref[...])

    @pl.when(kk == pl.num_programs(1) - 1)
    def _():
        o_ref[...] = h_ref[...] + _rms(acc_ref[...], g_ref[...])


def matmul_norm_residual(a, w, g, h, *, tm=512, tk=512):
    m, k = a.shape
    n = w.shape[1]
    tm = _tile(m, tm)
    tk = _tile(k, tk)
    return pl.pallas_call(
        _mm_norm_res_kernel,
        grid=(m // tm, k // tk),
        in_specs=[pl.BlockSpec((tm, tk), lambda i, j: (i, j)),
                  pl.BlockSpec((tk, n), lambda i, j: (j, 0)),
                  pl.BlockSpec((1, n), lambda i, j: (0, 0)),
                  pl.BlockSpec((tm, n), lambda i, j: (i, 0))],
        out_specs=pl.BlockSpec((tm, n), lambda i, j: (i, 0)),
        out_shape=jax.ShapeDtypeStruct((m, n), F32),
        scratch_shapes=[pltpu.VMEM((tm, n), F32)],
        compiler_params=_cparams(("parallel", "arbitrary")),
        name="matmul_norm_residual",
    )(a, w, g, h)


def _mm_kernel(a_ref, w_ref, o_ref):
    o_ref[...] = _dot(a_ref[...], w_ref[...]).astype(o_ref.dtype)


def matmul(a, w, *, tm=512, tn=1024, out_dtype=F32):
    m, k = a.shape
    n = w.shape[1]
    tm = _tile(m, tm)
    tn = _tile(n, tn)
    return pl.pallas_call(
        _mm_kernel,
        grid=(m // tm, n // tn),
        in_specs=[pl.BlockSpec((tm, k), lambda i, j: (i, 0)),
                  pl.BlockSpec((k, tn), lambda i, j: (0, j))],
        out_specs=pl.BlockSpec((tm, tn), lambda i, j: (i, j)),
        out_shape=jax.ShapeDtypeStruct((m, n), out_dtype),
        compiler_params=_cparams(("parallel", "arbitrary")),
        name="matmul",
    )(a, w)


def _rope_group(grp, ct, st):
    return grp * ct + pltpu.roll(grp, 64, 1) * st


def _mla_proj_kernel(cq_ref, ckv_ref, kr_ref, qn_ref, kvn_ref, wq_ref, wkv_ref, ct_ref, st_ref,
                     q_ref, k_ref, v_ref):
    scale = MLA_QK ** -0.5
    ct = ct_ref[...]
    st = st_ref[...]
    xq = _rms(cq_ref[...], qn_ref[...]).astype(BF16)
    xkv = _rms(ckv_ref[...], kvn_ref[...]).astype(BF16)
    kr = _rope_group(kr_ref[...], ct, st).astype(BF16)
    for h in range(MLA_HEADS):
        c0 = h * MLA_HEAD_PAD
        yq = _dot(xq, wq_ref[:, c0:c0 + MLA_HEAD_PAD])
        q_ref[:, c0:c0 + LANES] = (yq[:, :LANES] * scale).astype(BF16)
        q_ref[:, c0 + LANES:c0 + 2 * LANES] = (_rope_group(yq[:, LANES:], ct, st) * scale).astype(BF16)
        ykv = _dot(xkv, wkv_ref[:, c0:c0 + MLA_HEAD_PAD])
        k_ref[:, c0:c0 + LANES] = ykv[:, :LANES].astype(BF16)
        k_ref[:, c0 + LANES:c0 + 2 * LANES] = kr
        v_ref[:, h * MLA_V:(h + 1) * MLA_V] = ykv[:, LANES:].astype(BF16)


def mla_proj(hcat, q_norm, kv_norm, wq, wkv, ct, st, *, tm=512):
    m = hcat.shape[0]
    tm = _tile(m, tm)
    hp = MLA_HEADS * MLA_HEAD_PAD
    row = lambda i: (i, 0)
    const = lambda i: (0, 0)
    return pl.pallas_call(
        _mla_proj_kernel,
        grid=(m // tm,),
        in_specs=[pl.BlockSpec((tm, MLA_Q_LORA), lambda i: (i, HC_CQ // MLA_Q_LORA)),
                  pl.BlockSpec((tm, MLA_KV_LORA), lambda i: (i, HC_CKV // MLA_KV_LORA)),
                  pl.BlockSpec((tm, LANES), lambda i: (i, HC_KR // LANES)),
                  pl.BlockSpec((1, MLA_Q_LORA), const),
                  pl.BlockSpec((1, MLA_KV_LORA), const),
                  pl.BlockSpec((MLA_Q_LORA, hp), const),
                  pl.BlockSpec((MLA_KV_LORA, hp), const),
                  pl.BlockSpec((tm, LANES), row),
                  pl.BlockSpec((tm, LANES), row)],
        out_specs=[pl.BlockSpec((tm, hp), row),
                   pl.BlockSpec((tm, hp), row),
                   pl.BlockSpec((tm, MLA_HEADS * MLA_V), row)],
        out_shape=[jax.ShapeDtypeStruct((m, hp), BF16),
                   jax.ShapeDtypeStruct((m, hp), BF16),
                   jax.ShapeDtypeStruct((m, MLA_HEADS * MLA_V), BF16)],
        compiler_params=_cparams(("parallel",)),
        name="mla_proj",
    )(hcat, hcat, hcat, q_norm, kv_norm, wq, wkv, ct, st)


def _attn_kernel(q_ref, k_ref, v_ref, o_ref, *, tq, tk):
    qi = pl.program_id(2)
    q = q_ref[...]
    row = qi * tq + lax.broadcasted_iota(jnp.int32, (tq, tk), 0)
    col0 = lax.broadcasted_iota(jnp.int32, (tq, tk), 1)

    def body(ki, carry):
        m_prev, l_prev, acc = carry
        start = pl.multiple_of(ki * tk, tk)
        k = k_ref[pl.ds(start, tk), :]
        v = v_ref[pl.ds(start, tk), :]
        s = _dot_nt(q, k)
        s = jnp.where(col0 + ki * tk <= row, s, -1e30)
        m_new = jnp.maximum(m_prev, jnp.max(s, axis=-1, keepdims=True))
        alpha = jnp.exp(m_prev - m_new)
        p = jnp.exp(s - m_new)
        l_new = alpha * l_prev + jnp.sum(p, axis=-1, keepdims=True)
        acc = alpha * acc + _dot(p.astype(BF16), v)
        return m_new, l_new, acc

    nk = (qi * tq + tq + tk - 1) // tk
    init = (jnp.full((tq, 1), -1e30, F32), jnp.zeros((tq, 1), F32), jnp.zeros((tq, MLA_V), F32))
    _, l_fin, acc = lax.fori_loop(0, nk, body, init)
    o_ref[...] = (acc / l_fin).astype(o_ref.dtype)


def mla_attention(q, k, v, batch, seq, *, tq=512, tk=512):
    tq = _tile(seq, tq)
    tk = _tile(seq, tk)
    nq = seq // tq
    return pl.pallas_call(
        functools.partial(_attn_kernel, tq=tq, tk=tk),
        grid=(batch, MLA_HEADS, nq),
        in_specs=[pl.BlockSpec((tq, MLA_HEAD_PAD), lambda b, h, i: (b * nq + i, h)),
                  pl.BlockSpec((seq, MLA_HEAD_PAD), lambda b, h, i: (b, h)),
                  pl.BlockSpec((seq, MLA_V), lambda b, h, i: (b, h))],
        out_specs=pl.BlockSpec((tq, MLA_V), lambda b, h, i: (b * nq + i, h)),
        out_shape=jax.ShapeDtypeStruct((batch * seq, MLA_HEADS * MLA_V), BF16),
        compiler_params=_cparams(("parallel", "parallel", "arbitrary")),
        name="mla_attention",
    )(q, k, v)


def _unit_lower_inverse(a_strict, levels, prec):
    n = a_strict.shape[0]
    eye = (lax.broadcasted_iota(jnp.int32, (n, n), 0) == lax.broadcasted_iota(jnp.int32, (n, n), 1)).astype(F32)
    p = -a_strict
    r = eye + p
    for _ in range(levels - 1):
        pb = p.astype(BF16) if prec is None else p
        p = _dot(pb, pb, prec)
        rb = r.astype(BF16) if prec is None else r
        pb = p.astype(BF16) if prec is None else p
        r = r + _dot(rb, pb, prec)
    return r


def _col_of_row(row_vec, n):
    eye = lax.broadcasted_iota(jnp.int32, (n, n), 0) == lax.broadcasted_iota(jnp.int32, (n, n), 1)
    return jnp.sum(jnp.where(eye, jnp.broadcast_to(row_vec, (n, n)), 0.0), axis=1, keepdims=True)


def _gdn_kernel(gq_ref, gk_ref, gv_ref, gz_ref, sm_ref, cw_ref, alog_ref, dtb_ref, on_ref, o_ref,
                xq_scr, xk_scr, xv_scr, state_scr, *, chunk, prec, prec_intra):
    c = chunk
    h = pl.program_id(1)
    ci = pl.program_id(2)

    @pl.when(ci == 0)
    def _():
        for scr in (xq_scr, xk_scr, xv_scr):
            scr[0:8, :] = jnp.zeros((8, LANES), F32)
        state_scr[...] = jnp.zeros_like(state_scr)

    def conv_silu(x_ref, scr, widx):
        scr[8:8 + c, :] = x_ref[...]
        acc = jnp.zeros((c, LANES), F32)
        for j in range(GDN_CONV):
            s = GDN_CONV - 1 - j
            acc = acc + cw_ref[widx, j:j + 1, :] * scr[8 - s:8 - s + c, :]
        tail = scr[c:c + 8, :]
        scr[0:8, :] = tail
        return _silu(acc)

    def l2n(x):
        return x * lax.rsqrt(jnp.sum(x * x, axis=-1, keepdims=True) + 1e-6)

    q = l2n(conv_silu(gq_ref, xq_scr, 0)) * (GDN_DK ** -0.5)
    k = l2n(conv_silu(gk_ref, xk_scr, 1))
    v = conv_silu(gv_ref, xv_scr, 2)

    sm = sm_ref[...]
    lane = lax.broadcasted_iota(jnp.int32, (c, LANES), 1)
    gb = jnp.sum(jnp.where(lane == h, sm, 0.0), axis=-1, keepdims=True)
    ga = jnp.sum(jnp.where(lane == GDN_HEADS + h, sm, 0.0), axis=-1, keepdims=True)
    beta = _sigmoid(gb)
    g = -jnp.exp(alog_ref[...]) * _softplus(jnp.broadcast_to(ga, (c, LANES)) + dtb_ref[...])

    ri = lax.broadcasted_iota(jnp.int32, (c, c), 0)
    cj = lax.broadcasted_iota(jnp.int32, (c, c), 1)
    tri = (cj <= ri).astype(F32)
    gam = _dot(tri, g, HI)
    diff = gam - gam.T
    decay = jnp.exp(jnp.where(cj <= ri, diff, -1e30))
    eg = jnp.exp(gam)
    gl = gam[c - 1:c, :]
    kb = k * beta
    vb = v * beta

    def cast(x):
        return x.astype(BF16) if prec is None else x

    def casti(x):
        return x.astype(BF16) if prec_intra is None else x

    a_mat = jnp.where(cj < ri, _dot_nt(casti(kb), casti(k), prec_intra) * decay, 0.0)
    tinv = casti(_unit_lower_inverse(a_mat, 7, prec_intra))
    u = _dot(tinv, casti(vb), prec_intra)
    w = _dot(tinv, casti(kb * eg), prec_intra)
    attn = _dot_nt(cast(q), cast(k), prec) * decay
    q_dec = q * eg
    k_dec = k * jnp.exp(gl - gam)
    state = state_scr[...]
    sc = cast(state)
    e = u - _dot(cast(w), sc, prec)
    ec = cast(e)
    o = _dot(cast(q_dec), sc, prec) + _dot(cast(attn), ec, prec)
    state_scr[...] = state * jnp.exp(gl) + _dot(cast(k_dec.T), ec, prec)
    o_ref[...] = (_rms(o, on_ref[...]) * _silu(gz_ref[...])).astype(o_ref.dtype)


def gdn_mixer(hcat, conv_w, a_log, dt_bias, out_norm, batch, seq, *, prec=None, prec_intra=HI):
    c = GDN_CHUNK
    assert c == LANES and seq % c == 0
    nc = seq // c
    blk = lambda off: pl.BlockSpec((c, LANES), lambda b, h, i, off=off: (b * nc + i, off // LANES + h))
    cw = conv_w.reshape(GDN_CONV, 3, GDN_HEADS, LANES).transpose(2, 1, 0, 3)
    alog = jnp.broadcast_to(a_log.reshape(GDN_HEADS, 1, 1), (GDN_HEADS, 1, LANES)).astype(F32)
    dtb = jnp.broadcast_to(dt_bias.reshape(GDN_HEADS, 1, 1), (GDN_HEADS, 1, LANES)).astype(F32)
    return pl.pallas_call(
        functools.partial(_gdn_kernel, chunk=c, prec=prec, prec_intra=prec_intra),
        grid=(batch, GDN_HEADS, nc),
        in_specs=[blk(HC_GQ), blk(HC_GK), blk(HC_GV), blk(HC_GZ),
                  pl.BlockSpec((c, LANES), lambda b, h, i: (b * nc + i, HC_SM // LANES)),
                  pl.BlockSpec((None, 3, GDN_CONV, LANES), lambda b, h, i: (h, 0, 0, 0)),
                  pl.BlockSpec((None, 1, LANES), lambda b, h, i: (h, 0, 0)),
                  pl.BlockSpec((None, 1, LANES), lambda b, h, i: (h, 0, 0)),
                  pl.BlockSpec((1, GDN_DV), lambda b, h, i: (0, 0))],
        out_specs=pl.BlockSpec((c, LANES), lambda b, h, i: (b * nc + i, h)),
        out_shape=jax.ShapeDtypeStruct((batch * seq, GDN_HEADS * GDN_DV), BF16),
        scratch_shapes=[pltpu.VMEM((c + 8, LANES), F32)] * 3 + [pltpu.VMEM((GDN_DK, GDN_DV), F32)],
        compiler_params=_cparams(("parallel", "parallel", "arbitrary")),
        name="gdn_mixer",
    )(hcat, hcat, hcat, hcat, hcat, cw, alog, dtb, out_norm)


def _rwkv_mix_kernel(h_ref, hp_ref, g_ref, mix_ref, o_ref, *, tm, seq):
    i = pl.program_id(0)
    g = g_ref[...]
    xn = _rms(h_ref[...], g)
    prev_last = _rms(hp_ref[...], g)[7:8, :]
    prev_last = jnp.where((i * tm) % seq == 0, 0.0, prev_last)
    rows = lax.broadcasted_iota(jnp.int32, xn.shape, 0)
    xprev = jnp.where(rows == 0, prev_last, pltpu.roll(xn, 1, 0))
    xx = xprev - xn
    for j in range(6):
        o_ref[j] = (xn + xx * mix_ref[j:j + 1, :]).astype(o_ref.dtype)


def rwkv_mix(h, g, mix, seq, *, tm=256):
    m, d = h.shape
    tm = _tile(seq, tm)
    return pl.pallas_call(
        functools.partial(_rwkv_mix_kernel, tm=tm, seq=seq),
        grid=(m // tm,),
        in_specs=[pl.BlockSpec((tm, d), lambda i: (i, 0)),
                  pl.BlockSpec((8, d), lambda i: (jnp.maximum(i * (tm // 8) - 1, 0), 0)),
                  pl.BlockSpec((1, d), lambda i: (0, 0)),
                  pl.BlockSpec((6, d), lambda i: (0, 0))],
        out_specs=pl.BlockSpec((6, tm, d), lambda i: (0, i, 0)),
        out_shape=jax.ShapeDtypeStruct((6, m, d), BF16),
        compiler_params=_cparams(("parallel",)),
        name="rwkv_mix",
    )(h, h, g, mix)


def _lora_kernel(x_ref, a_ref, b_ref, bias_ref, *rest, mid, post):
    o_ref = rest[-1]
    z = _dot(x_ref[...], a_ref[...])
    if mid == "tanh":
        z = jnp.tanh(z)
    elif mid == "sigmoid":
        z = _sigmoid(z)
    y = _dot(z.astype(BF16), b_ref[...]) + bias_ref[...]
    if post == "logdecay":
        y = -jnp.exp(-_softplus(-y) - 0.5)
    elif post == "sigmoid":
        y = _sigmoid(y)
    elif post == "vres":
        v = rest[0][...]
        y = v + (rest[1][...] - v) * _sigmoid(y)
    o_ref[...] = y.astype(o_ref.dtype)


def lora(x, a, b, bias, *extras, mid, post, tm=512):
    m, d = x.shape
    r = a.shape[1]
    n = b.shape[1]
    tm = _tile(m, tm)
    row = lambda i: (i, 0)
    const = lambda i: (0, 0)
    return pl.pallas_call(
        functools.partial(_lora_kernel, mid=mid, post=post),
        grid=(m // tm,),
        in_specs=[pl.BlockSpec((tm, d), row), pl.BlockSpec((d, r), const), pl.BlockSpec((r, n), const),
                  pl.BlockSpec((1, n), const)] + [pl.BlockSpec((tm, n), row) for _ in extras],
        out_specs=pl.BlockSpec((tm, n), row),
        out_shape=jax.ShapeDtypeStruct((m, n), F32),
        compiler_params=_cparams(("parallel",)),
        name="lora_" + post,
    )(x, a, b, bias, *extras)


def _rwkv_kernel(r_ref, k_ref, v_ref, a_ref, lw_ref, gt_ref, kk_ref, ka_ref, rk_ref, lnw_ref, lnb_ref,
                 o_ref, state_scr, *, chunk, pairs, prec):
    c = chunk
    ci = pl.program_id(2)

    @pl.when(ci == 0)
    def _():
        state_scr[...] = jnp.zeros_like(state_scr)

    lane = lax.broadcasted_iota(jnp.int32, (c, LANES), 1)
    m0 = lane < RWKV_HEAD
    ri = lax.broadcasted_iota(jnp.int32, (c, c), 0)
    cj = lax.broadcasted_iota(jnp.int32, (c, c), 1)
    tri = (cj <= ri).astype(F32)
    r2 = lax.broadcasted_iota(jnp.int32, (2 * c, 2 * c), 0)
    c2 = lax.broadcasted_iota(jnp.int32, (2 * c, 2 * c), 1)
    strict2 = c2 < r2
    rw = lax.broadcasted_iota(jnp.int32, (c, 2 * c), 0)
    cw = lax.broadcasted_iota(jnp.int32, (c, 2 * c), 1)
    incl_w = jnp.where(cw >= c, cw - c, cw) <= rw

    def seg_sum(x):
        s0 = jnp.sum(jnp.where(m0, x, 0.0), axis=-1, keepdims=True)
        s1 = jnp.sum(jnp.where(m0, 0.0, x), axis=-1, keepdims=True)
        return jnp.where(m0, s0, s1)

    def stack(x):
        return jnp.concatenate([jnp.where(m0, x, 0.0), jnp.where(m0, 0.0, x)], axis=0)

    def cast(x):
        return x.astype(BF16) if prec is None else x

    for p in range(pairs):
        sl = slice(p * LANES, (p + 1) * LANES)
        r = r_ref[:, sl]
        k = k_ref[:, sl]
        v = v_ref[:, sl]
        a = a_ref[:, sl]
        lw = lw_ref[:, sl]
        kk = k * kk_ref[:, sl]
        kk = kk * lax.rsqrt(seg_sum(kk * kk) + 1e-12)
        kh = k * (1.0 + (a - 1.0) * ka_ref[:, sl])
        bv = kk * a

        cum = _dot(tri, lw, HI)
        cl = cum[c - 1:c, :]
        e_cum = jnp.exp(cum)
        e_neg = jnp.exp(-cum)
        e_end = jnp.exp(cl - cum)
        rg = r * e_cum
        xs = cast(stack(kk * jnp.exp(cum - lw)))
        bs = cast(stack(bv * e_neg))
        ks = cast(stack(kh * e_neg))
        vs = cast(stack(v))
        khs = stack(kh * e_end)
        bhs = stack(bv * e_end)
        rgc = cast(rg)

        akk = jnp.where(strict2, _dot_nt(xs, bs, prec), 0.0)
        akv = jnp.where(strict2, _dot_nt(xs, ks, prec), 0.0)
        tinv = cast(_unit_lower_inverse(akk, 6, prec))
        state = state_scr[p]
        sc = cast(state)
        us = _dot(tinv, cast(_dot(xs, sc, prec) + _dot(cast(akv), vs, prec)), prec)
        usc = cast(us)
        brk = jnp.where(incl_w, _dot_nt(rgc, ks, prec), 0.0)
        brb = jnp.where(incl_w, _dot_nt(rgc, bs, prec), 0.0)
        y = _dot(rgc, sc, prec) + _dot(cast(brk), vs, prec) - _dot(cast(brb), usc, prec)
        state_scr[p] = (state * _col_of_row(jnp.exp(cl), LANES)
                        + _dot(cast(khs.T), vs, prec) - _dot(cast(bhs.T), usc, prec))

        mu = seg_sum(y) * (1.0 / RWKV_HEAD)
        yc = y - mu
        var = seg_sum(yc * yc) * (1.0 / RWKV_HEAD)
        yn = yc * lax.rsqrt(var + RWKV_LN_EPS) * lnw_ref[:, sl] + lnb_ref[:, sl]
        bonus = seg_sum(r * kh * rk_ref[:, sl]) * v
        o_ref[:, sl] = ((yn + bonus) * gt_ref[:, sl]).astype(o_ref.dtype)


def rwkv_recurrence(r, k, v, a, lw, gate, k_k, k_a, r_k, ln_w, ln_b, batch, seq, *, pairs=4, prec=None):
    c = RWKV_CHUNK
    m, d = r.shape
    nc = seq // c
    width = pairs * LANES
    blk = pl.BlockSpec((c, width), lambda b, hp, i: (b * nc + i, hp))
    par = pl.BlockSpec((1, width), lambda b, hp, i: (0, hp))
    return pl.pallas_call(
        functools.partial(_rwkv_kernel, chunk=c, pairs=pairs, prec=prec),
        grid=(batch, d // width, nc),
        in_specs=[blk] * 6 + [par] * 5,
        out_specs=blk,
        out_shape=jax.ShapeDtypeStruct((m, d), BF16),
        scratch_shapes=[pltpu.VMEM((pairs, LANES, LANES), F32)],
        compiler_params=_cparams(("parallel", "parallel", "arbitrary")),
        name="rwkv_recurrence",
    )(r, k, v, a, lw, gate, k_k, k_a, r_k, ln_w, ln_b)


def _pad_cols(w, n):
    return jnp.pad(w, ((0, 0), (0, n - w.shape[1])))


def _rot_cols(w):
    half = MLA_ROPE // 2
    return jnp.concatenate([-w[..., half:], w[..., :half]], axis=-1)


def _hybrid_layer(h, ct, st, batch, seq, g_pre, g_post, w_in, q_norm, w_uq, kv_norm, w_ukv, conv_w, a_log,
                  dt_bias, out_norm, w_out):
    d = D_MODEL
    o_cq, o_ckv, o_kr = 0, MLA_Q_LORA, MLA_Q_LORA + MLA_KV_LORA
    o_g = o_kr + MLA_ROPE
    o_sm = o_g + 4 * GDN_HEADS * GDN_DK
    w_kr = w_in[:, o_kr:o_g]
    w_cat = jnp.concatenate([w_in[:, :o_kr], w_in[:, o_g:o_sm], w_kr, _rot_cols(w_kr),
                             _pad_cols(w_in[:, o_sm:], LANES)], axis=1).astype(BF16)
    hcat = norm_matmul(h, g_pre.reshape(1, d), w_cat, tn=768)

    wq = w_uq.reshape(MLA_Q_LORA, MLA_HEADS, MLA_QK)
    wq_rope = wq[..., MLA_NOPE:]
    wq = jnp.concatenate([wq[..., :MLA_NOPE], wq_rope, _rot_cols(wq_rope)], axis=-1)
    wq = wq.reshape(MLA_Q_LORA, MLA_HEADS * MLA_HEAD_PAD).astype(BF16)
    q, k, v = mla_proj(hcat, q_norm.reshape(1, -1), kv_norm.reshape(1, -1), wq, w_ukv.astype(BF16), ct, st)
    mla_out = mla_attention(q, k, v, batch, seq)
    gdn_out = gdn_mixer(hcat, conv_w, a_log, dt_bias, out_norm.reshape(1, -1), batch, seq)
    mix = jnp.concatenate([mla_out, gdn_out], axis=1)
    return matmul_norm_residual(mix, w_out.astype(BF16), g_post.reshape(1, d), h, tk=2048)


def _rwkv_layer(h, v_first, batch, seq, g_pre, g_post, mix, w_r, w_k, w_v, w_o, w0, w1, w2, a0, a1, a2,
                g1, g2, k_k, k_a, r_k, ln_w, ln_b, vres):
    d = D_MODEL
    row = lambda t: t.reshape(1, d).astype(F32)
    padc = lambda w: _pad_cols(w, -(-w.shape[1] // LANES) * LANES).astype(BF16)
    padr = lambda w: jnp.pad(w, ((0, -(-w.shape[0] // LANES) * LANES - w.shape[0]), (0, 0))).astype(BF16)
    xs = rwkv_mix(h, g_pre.reshape(1, d), mix, seq)
    xr, xw, xk, xv, xa, xg = (xs[i] for i in range(6))
    r = matmul(xr, w_r.astype(BF16))
    k = matmul(xk, w_k.astype(BF16))
    v = matmul(xv, w_v.astype(BF16))
    lw = lora(xw, padc(w1), padr(w2), row(w0), mid="tanh", post="logdecay")
    a = lora(xa, padc(a1), padr(a2), row(a0), mid="none", post="sigmoid")
    gate = lora(xg, padc(g1), padr(g2), jnp.zeros((1, d), F32), mid="sigmoid", post="none")
    if vres is None:
        v_first = v
    else:
        v0, v1, v2 = vres
        v = lora(xv, padc(v1), padr(v2), row(v0), v, v_first, mid="none", post="vres")
    y = rwkv_recurrence(r, k, v, a, lw, gate, row(k_k), row(k_a), row(r_k), row(ln_w), row(ln_b), batch, seq)
    return matmul_norm_residual(y, w_o.astype(BF16), g_post.reshape(1, d), h, tk=2048), v_first


def _ffn_layer(h, g_pre, g_post, w_gate, w_up, w_down):
    d = D_MODEL
    act = ffn_up(h, g_pre.reshape(1, d), w_gate.astype(BF16), w_up.astype(BF16))
    return matmul_norm_residual(act, w_down.astype(BF16), g_post.reshape(1, d), h)


def _rope_tables(positions):
    inv_freq = 1.0 / (ROPE_THETA ** (jnp.arange(0, MLA_ROPE, 2, dtype=F32) / MLA_ROPE))
    ang = positions.astype(F32).reshape(-1, 1) * inv_freq
    zeros = jnp.zeros((ang.shape[0], LANES - MLA_ROPE), F32)
    cos, sin = jnp.cos(ang), jnp.sin(ang)
    return jnp.concatenate([cos, cos, zeros], axis=1), jnp.concatenate([sin, sin, zeros], axis=1)


def kernel(x, positions, norm_mix_pre, norm_mix_post, norm_ffn_pre, norm_ffn_post, hyb_w_in, mla_q_norm, mla_w_uq, mla_kv_norm, mla_w_ukv, gdn_conv_w, gdn_a_log, gdn_dt_bias, gdn_out_norm, hyb_w_out, rwkv_mix, rwkv_w_r, rwkv_w_k, rwkv_w_v, rwkv_w_o, rwkv_w0, rwkv_w1, rwkv_w2, rwkv_a0, rwkv_a1, rwkv_a2, rwkv_g1, rwkv_g2, rwkv_k_k, rwkv_k_a, rwkv_r_k, rwkv_ln_w, rwkv_ln_b, rwkv_v0, rwkv_v1, rwkv_v2, ffn_w_gate, ffn_w_up, ffn_w_down):
    batch, seq, d = x.shape
    h = x.reshape(batch * seq, d)
    ct, st = _rope_tables(positions)
    v_first = None
    for layer in range(DEPTH):
        if layer % 2 == 0:
            e = layer // 2
            h = _hybrid_layer(h, ct, st, batch, seq, norm_mix_pre[layer], norm_mix_post[layer], hyb_w_in[e],
                              mla_q_norm[e], mla_w_uq[e], mla_kv_norm[e], mla_w_ukv[e], gdn_conv_w[e],
                              gdn_a_log[e], gdn_dt_bias[e], gdn_out_norm[e], hyb_w_out[e])
        else:
            o = layer // 2
            vres = None if o == 0 else (rwkv_v0[o - 1], rwkv_v1[o - 1], rwkv_v2[o - 1])
            h, v_first = _rwkv_layer(h, v_first, batch, seq, norm_mix_pre[layer], norm_mix_post[layer],
                                     rwkv_mix[o], rwkv_w_r[o], rwkv_w_k[o], rwkv_w_v[o], rwkv_w_o[o],
                                     rwkv_w0[o], rwkv_w1[o], rwkv_w2[o], rwkv_a0[o], rwkv_a1[o], rwkv_a2[o],
                                     rwkv_g1[o], rwkv_g2[o], rwkv_k_k[o], rwkv_k_a[o], rwkv_r_k[o],
                                     rwkv_ln_w[o], rwkv_ln_b[o], vres)
        h = _ffn_layer(h, norm_ffn_pre[layer], norm_ffn_post[layer], ffn_w_gate[layer], ffn_w_up[layer],
                       ffn_w_down[layer])
    return h.reshape(batch, seq, d)
```

```python
import functools

import jax
import jax.numpy as jnp
import numpy as np
from jax import lax
from jax.experimental import pallas as pl
from jax.experimental.pallas import tpu as pltpu

F32 = jnp.float32
BF16 = jnp.bfloat16
HI = lax.Precision.HIGHEST

D_MODEL = 2048
DEPTH = 4
LANES = 128
VMEM_LIMIT = 56 * 1024 * 1024

MLA_HEADS = 8
MLA_Q_LORA = 512
MLA_KV_LORA = 512
MLA_NOPE = 128
MLA_ROPE = 64
MLA_V = 128
MLA_QK = MLA_NOPE + MLA_ROPE
MLA_HEAD_PAD = 256
ROPE_THETA = 10000.0

GDN_HEADS = 8
GDN_DK = 128
GDN_DV = 128
GDN_CONV = 4
GDN_CHUNK = 128

RWKV_HEAD = 64
RWKV_HEADS = D_MODEL // RWKV_HEAD
RWKV_CHUNK = 64
RWKV_LN_EPS = 64e-5
RMS_EPS = 1e-6

HC_CQ, HC_CKV, HC_GQ, HC_GK, HC_GV, HC_GZ, HC_KR, HC_SM = 0, 512, 1024, 2048, 3072, 4096, 5120, 5248
HC_COLS = 5376


def _cparams(sem):
    return pltpu.CompilerParams(dimension_semantics=sem, vmem_limit_bytes=VMEM_LIMIT)


def _tile(n, pref):
    t = min(n, pref)
    while n % t:
        t -= 8
    return t


def _dot(a, b, prec=None):
    return jnp.dot(a, b, preferred_element_type=F32, precision=prec)


def _dot_nt(a, b, prec=None):
    return lax.dot_general(a, b, (((1,), (1,)), ((), ())), preferred_element_type=F32, precision=prec)


def _rms(x, g, eps=RMS_EPS):
    return x * lax.rsqrt(jnp.mean(x * x, axis=-1, keepdims=True) + eps) * g


def _sigmoid(x):
    return 1.0 / (1.0 + jnp.exp(-x))


def _silu(x):
    return x * _sigmoid(x)


def _softplus(x):
    return jnp.maximum(x, 0.0) + jnp.log(1.0 + jnp.exp(-jnp.abs(x)))


def _norm_mm_kernel(x_ref, g_ref, w_ref, o_ref, xn_ref):
    @pl.when(pl.program_id(1) == 0)
    def _():
        xn_ref[...] = _rms(x_ref[...], g_ref[...]).astype(BF16)

    o_ref[...] = _dot(xn_ref[...], w_ref[...]).astype(o_ref.dtype)


def norm_matmul(x, g, w, *, tn, out_dtype=F32, tm=512):
    m, k = x.shape
    n = w.shape[1]
    tm = _tile(m, tm)
    return pl.pallas_call(
        _norm_mm_kernel,
        grid=(m // tm, n // tn),
        in_specs=[pl.BlockSpec((tm, k), lambda i, j: (i, 0)),
                  pl.BlockSpec((1, k), lambda i, j: (0, 0)),
                  pl.BlockSpec((k, tn), lambda i, j: (0, j))],
        out_specs=pl.BlockSpec((tm, tn), lambda i, j: (i, j)),
        out_shape=jax.ShapeDtypeStruct((m, n), out_dtype),
        scratch_shapes=[pltpu.VMEM((tm, k), BF16)],
        compiler_params=_cparams(("parallel", "arbitrary")),
        name="norm_matmul",
    )(x, g, w)


def _ffn_up_kernel(x_ref, g_ref, wg_ref, wu_ref, o_ref, xn_ref):
    @pl.when(pl.program_id(1) == 0)
    def _():
        xn_ref[...] = _rms(x_ref[...], g_ref[...]).astype(BF16)

    xn = xn_ref[...]
    gate = _dot(xn, wg_ref[...])
    up = _dot(xn, wu_ref[...])
    o_ref[...] = (_silu(gate) * up).astype(o_ref.dtype)


def ffn_up(h, g, wg, wu, *, tm=512, tn=512):
    m, k = h.shape
    n = wg.shape[1]
    tm = _tile(m, tm)
    tn = _tile(n, tn)
    return pl.pallas_call(
        _ffn_up_kernel,
        grid=(m // tm, n // tn),
        in_specs=[pl.BlockSpec((tm, k), lambda i, j: (i, 0)),
                  pl.BlockSpec((1, k), lambda i, j: (0, 0)),
                  pl.BlockSpec((k, tn), lambda i, j: (0, j)),
                  pl.BlockSpec((k, tn), lambda i, j: (0, j))],
        out_specs=pl.BlockSpec((tm, tn), lambda i, j: (i, j)),
        out_shape=jax.ShapeDtypeStruct((m, n), BF16),
        scratch_shapes=[pltpu.VMEM((tm, k), BF16)],
        compiler_params=_cparams(("parallel", "arbitrary")),
        name="ffn_up",
    )(h, g, wg, wu)


def _mm_norm_res_kernel(a_ref, w_ref, g_ref, h_ref, o_ref, acc_ref):
    kk = pl.program_id(1)

    @pl.when(kk == 0)
    def _():
        acc_ref[...] = jnp.zeros_like(acc_ref)

    acc_ref[...] += _dot(a_ref[...], w_ref[...])

    @pl.when(kk == pl.num_programs(1) - 1)
    def _():
        o_ref[...] = h_ref[...] + _rms(acc_ref[...], g_ref[...])


def matmul_norm_residual(a, w, g, h, *, tm=512, tk=512):
    m, k = a.shape
    n = w.shape[1]
    tm = _tile(m, tm)
    tk = _tile(k, tk)
    return pl.pallas_call(
        _mm_norm_res_kernel,
        grid=(m // tm, k // tk),
        in_specs=[pl.BlockSpec((tm, tk), lambda i, j: (i, j)),
                  pl.BlockSpec((tk, n), lambda i, j: (j, 0)),
                  pl.BlockSpec((1, n), lambda i, j: (0, 0)),
                  pl.BlockSpec((tm, n), lambda i, j: (i, 0))],
        out_specs=pl.BlockSpec((tm, n), lambda i, j: (i, 0)),
        out_shape=jax.ShapeDtypeStruct((m, n), F32),
        scratch_shapes=[pltpu.VMEM((tm, n), F32)],
        compiler_params=_cparams(("parallel", "arbitrary")),
        name="matmul_norm_residual",
    )(a, w, g, h)


def _mm_kernel(a_ref, w_ref, o_ref):
    o_ref[...] = _dot(a_ref[...], w_ref[...]).astype(o_ref.dtype)


def matmul(a, idx, w, *, tm=512, tn=1024, out_dtype=F32):
    _, m, k = a.shape
    n = w.shape[1]
    tm = _tile(m, tm)
    tn = _tile(n, tn)
    return pl.pallas_call(
        _mm_kernel,
        grid=(m // tm, n // tn),
        in_specs=[pl.BlockSpec((None, tm, k), lambda i, j: (idx, i, 0)),
                  pl.BlockSpec((k, tn), lambda i, j: (0, j))],
        out_specs=pl.BlockSpec((tm, tn), lambda i, j: (i, j)),
        out_shape=jax.ShapeDtypeStruct((m, n), out_dtype),
        compiler_params=_cparams(("parallel", "arbitrary")),
        name="matmul",
    )(a, w)


def _rope_group(grp, ct, st):
    return grp * ct + pltpu.roll(grp, 64, 1) * st


def _mla_proj_kernel(cq_ref, ckv_ref, kr_ref, qn_ref, kvn_ref, wq_ref, wkv_ref, ct_ref, st_ref,
                     q_ref, k_ref, v_ref):
    scale = MLA_QK ** -0.5
    ct = ct_ref[...]
    st = st_ref[...]
    xq = _rms(cq_ref[...], qn_ref[...]).astype(BF16)
    xkv = _rms(ckv_ref[...], kvn_ref[...]).astype(BF16)
    kr = _rope_group(kr_ref[...], ct, st).astype(BF16)
    for h in range(MLA_HEADS):
        c0 = h * MLA_HEAD_PAD
        yq = _dot(xq, wq_ref[:, c0:c0 + MLA_HEAD_PAD])
        q_ref[:, c0:c0 + LANES] = (yq[:, :LANES] * scale).astype(BF16)
        q_ref[:, c0 + LANES:c0 + 2 * LANES] = (_rope_group(yq[:, LANES:], ct, st) * scale).astype(BF16)
        ykv = _dot(xkv, wkv_ref[:, c0:c0 + MLA_HEAD_PAD])
        k_ref[:, c0:c0 + LANES] = ykv[:, :LANES].astype(BF16)
        k_ref[:, c0 + LANES:c0 + 2 * LANES] = kr
        v_ref[:, h * MLA_V:(h + 1) * MLA_V] = ykv[:, LANES:].astype(BF16)


def mla_proj(hcat, q_norm, kv_norm, wq, wkv, ct, st, *, tm=512):
    m = hcat.shape[0]
    tm = _tile(m, tm)
    hp = MLA_HEADS * MLA_HEAD_PAD
    row = lambda i: (i, 0)
    const = lambda i: (0, 0)
    return pl.pallas_call(
        _mla_proj_kernel,
        grid=(m // tm,),
        in_specs=[pl.BlockSpec((tm, MLA_Q_LORA), lambda i: (i, HC_CQ // MLA_Q_LORA)),
                  pl.BlockSpec((tm, MLA_KV_LORA), lambda i: (i, HC_CKV // MLA_KV_LORA)),
                  pl.BlockSpec((tm, LANES), lambda i: (i, HC_KR // LANES)),
                  pl.BlockSpec((1, MLA_Q_LORA), const),
                  pl.BlockSpec((1, MLA_KV_LORA), const),
                  pl.BlockSpec((MLA_Q_LORA, hp), const),
                  pl.BlockSpec((MLA_KV_LORA, hp), const),
                  pl.BlockSpec((tm, LANES), row),
                  pl.BlockSpec((tm, LANES), row)],
        out_specs=[pl.BlockSpec((tm, hp), row),
                   pl.BlockSpec((tm, hp), row),
                   pl.BlockSpec((tm, MLA_HEADS * MLA_V), row)],
        out_shape=[jax.ShapeDtypeStruct((m, hp), BF16),
                   jax.ShapeDtypeStruct((m, hp), BF16),
                   jax.ShapeDtypeStruct((m, MLA_HEADS * MLA_V), BF16)],
        compiler_params=_cparams(("parallel",)),
        name="mla_proj",
    )(hcat, hcat, hcat, q_norm, kv_norm, wq, wkv, ct, st)


def _attn_kernel(q_ref, k_ref, v_ref, o_ref, *, tq, tk):
    qi = pl.program_id(2)
    q = q_ref[...]

    def step(ki, carry, masked):
        m_prev, l_prev, acc = carry
        start = pl.multiple_of(ki * tk, tk)
        k = k_ref[pl.ds(start, tk), :]
        v = v_ref[pl.ds(start, tk), :]
        s = _dot_nt(q, k)
        if masked:
            row = lax.broadcasted_iota(jnp.int32, (tq, tk), 0)
            col = lax.broadcasted_iota(jnp.int32, (tq, tk), 1)
            s = jnp.where(col <= row, s, -1e30)
        m_new = jnp.maximum(m_prev, jnp.max(s, axis=-1, keepdims=True))
        alpha = jnp.exp(m_prev - m_new)
        p = jnp.exp(s - m_new)
        l_new = alpha * l_prev + jnp.sum(p, axis=-1, keepdims=True)
        acc = alpha * acc + _dot(p.astype(BF16), v)
        return m_new, l_new, acc

    init = (jnp.full((tq, 1), -1e30, F32), jnp.zeros((tq, 1), F32), jnp.zeros((tq, MLA_V), F32))
    carry = lax.fori_loop(0, qi, lambda ki, c: step(ki, c, False), init)
    _, l_fin, acc = step(qi, carry, True)
    o_ref[...] = (acc / l_fin).astype(o_ref.dtype)


def mla_attention(q, k, v, batch, seq, *, tq=512, tk=512):
    tq = tk = _tile(seq, min(tq, tk))
    nq = seq // tq
    return pl.pallas_call(
        functools.partial(_attn_kernel, tq=tq, tk=tk),
        grid=(batch, MLA_HEADS, nq),
        in_specs=[pl.BlockSpec((tq, MLA_HEAD_PAD), lambda b, h, i: (b * nq + i, h)),
                  pl.BlockSpec((seq, MLA_HEAD_PAD), lambda b, h, i: (b, h)),
                  pl.BlockSpec((seq, MLA_V), lambda b, h, i: (b, h))],
        out_specs=pl.BlockSpec((tq, MLA_V), lambda b, h, i: (b * nq + i, h)),
        out_shape=jax.ShapeDtypeStruct((batch * seq, MLA_HEADS * MLA_V), BF16),
        compiler_params=_cparams(("parallel", "parallel", "arbitrary")),
        name="mla_attention",
    )(q, k, v)


def _unit_lower_inverse(a_list, levels, prec):
    n = a_list[0].shape[0]
    eye = (lax.broadcasted_iota(jnp.int32, (n, n), 0) == lax.broadcasted_iota(jnp.int32, (n, n), 1)).astype(F32)
    cast = (lambda x: x.astype(BF16)) if prec is None else (lambda x: x)
    p = [-a for a in a_list]
    r = [eye + x for x in p]
    for _ in range(levels - 1):
        pc = [cast(x) for x in p]
        p = [_dot(x, x, prec) for x in pc]
        pc = [cast(x) for x in p]
        r = [ri + _dot(cast(ri), pi, prec) for ri, pi in zip(r, pc)]
    return r


def _cumsum_rows(tri, x):
    hi = x.astype(BF16)
    r1 = x - hi.astype(F32)
    mid = r1.astype(BF16)
    lo = (r1 - mid.astype(F32)).astype(BF16)
    return _dot(tri, hi) + _dot(tri, mid) + _dot(tri, lo)


def _col_of_row(row_vec, n):
    eye = lax.broadcasted_iota(jnp.int32, (n, n), 0) == lax.broadcasted_iota(jnp.int32, (n, n), 1)
    return jnp.sum(jnp.where(eye, jnp.broadcast_to(row_vec, (n, n)), 0.0), axis=1, keepdims=True)


def _gdn_kernel(gq_ref, gk_ref, gv_ref, gz_ref, sm_ref, cw_ref, alog_ref, dtb_ref, on_ref, o_ref,
                xq_scr, xk_scr, xv_scr, state_scr, *, chunk, prec_intra):
    c = chunk
    heads = range(GDN_HEADS)
    hs = [slice(h * LANES, (h + 1) * LANES) for h in heads]

    @pl.when(pl.program_id(1) == 0)
    def _():
        for scr in (xq_scr, xk_scr, xv_scr):
            scr[0:8, :] = jnp.zeros((8, scr.shape[1]), F32)
        state_scr[...] = jnp.zeros_like(state_scr)

    def conv_silu(x_ref, scr, widx):
        scr[8:8 + c, :] = x_ref[...]
        acc = cw_ref[widx, GDN_CONV - 1:GDN_CONV, :] * scr[8:8 + c, :]
        for j in range(GDN_CONV - 1):
            s = GDN_CONV - 1 - j
            acc = acc + cw_ref[widx, j:j + 1, :] * scr[8 - s:8 - s + c, :]
        scr[0:8, :] = scr[c:c + 8, :]
        return _silu(acc)

    def l2n(x):
        return x * lax.rsqrt(jnp.sum(x * x, axis=-1, keepdims=True) + 1e-6)

    def bf(x):
        return x.astype(BF16)

    casti = bf if prec_intra is None else (lambda x: x)

    qa = conv_silu(gq_ref, xq_scr, 0)
    ka = conv_silu(gk_ref, xk_scr, 1)
    va = conv_silu(gv_ref, xv_scr, 2)
    q = [l2n(qa[:, s]) * (GDN_DK ** -0.5) for s in hs]
    k = [l2n(ka[:, s]) for s in hs]
    v = [va[:, s] for s in hs]

    sm = sm_ref[...]
    lane = lax.broadcasted_iota(jnp.int32, (c, LANES), 1)
    beta = [_sigmoid(jnp.sum(jnp.where(lane == h, sm, 0.0), axis=-1, keepdims=True)) for h in heads]
    ga = [jnp.sum(jnp.where(lane == GDN_HEADS + h, sm, 0.0), axis=-1, keepdims=True) for h in heads]
    g = [-jnp.exp(alog_ref[:, hs[h]]) * _softplus(jnp.broadcast_to(ga[h], (c, LANES)) + dtb_ref[:, hs[h]])
         for h in heads]

    ri = lax.broadcasted_iota(jnp.int32, (c, c), 0)
    cj = lax.broadcasted_iota(jnp.int32, (c, c), 1)
    tri = (cj <= ri).astype(BF16)
    gam = [_cumsum_rows(tri, x) for x in g]
    decay = [jnp.exp(jnp.where(cj <= ri, x - x.T, -1e30)) for x in gam]
    eg = [jnp.exp(x) for x in gam]
    gl = [x[c - 1:c, :] for x in gam]
    kb = [k[h] * beta[h] for h in heads]
    vb = [v[h] * beta[h] for h in heads]

    a_mat = [jnp.where(cj < ri, _dot_nt(casti(kb[h]), casti(k[h]), prec_intra) * decay[h], 0.0) for h in heads]
    tinv = [casti(x) for x in _unit_lower_inverse(a_mat, 7, prec_intra)]
    rhs = [casti(jnp.concatenate([kb[h] * eg[h], vb[h]], axis=1)) for h in heads]
    wu = [bf(_dot(tinv[h], rhs[h], prec_intra)) for h in heads]
    attn = [_dot_nt(bf(q[h]), bf(k[h])) * decay[h] for h in heads]
    k_dec = [k[h] * jnp.exp(gl[h] - gam[h]) for h in heads]
    lhs = [bf(jnp.concatenate([attn[h], k_dec[h].T], axis=0)) for h in heads]
    z = [_dot(lhs[h], wu[h]) for h in heads]
    state = [state_scr[h] for h in heads]
    l2 = [bf(jnp.concatenate([q[h] * eg[h] - z[h][:c, :LANES], -z[h][c:, :LANES]], axis=0)) for h in heads]
    rr = [_dot(l2[h], bf(state[h])) for h in heads]
    for h in heads:
        state_scr[h] = state[h] * jnp.exp(gl[h]) + rr[h][c:] + z[h][c:, LANES:]
        o = rr[h][:c] + z[h][:c, LANES:]
        o_ref[:, hs[h]] = (_rms(o, on_ref[...]) * _silu(gz_ref[:, hs[h]])).astype(o_ref.dtype)


def gdn_mixer(hcat, conv_w, a_log, dt_bias, out_norm, batch, seq, *, prec_intra=HI):
    c = GDN_CHUNK
    assert c == LANES and seq % c == 0
    nc = seq // c
    width = GDN_HEADS * LANES
    blk = lambda off: pl.BlockSpec((c, width), lambda b, i, off=off: (b * nc + i, off // width))
    const2 = lambda b, i: (0, 0)
    cw = conv_w.reshape(GDN_CONV, 3, width).transpose(1, 0, 2)
    alog = jnp.repeat(a_log.astype(F32), LANES).reshape(1, width)
    dtb = jnp.repeat(dt_bias.astype(F32), LANES).reshape(1, width)
    return pl.pallas_call(
        functools.partial(_gdn_kernel, chunk=c, prec_intra=prec_intra),
        grid=(batch, nc),
        in_specs=[blk(HC_GQ), blk(HC_GK), blk(HC_GV), blk(HC_GZ),
                  pl.BlockSpec((c, LANES), lambda b, i: (b * nc + i, HC_SM // LANES)),
                  pl.BlockSpec((3, GDN_CONV, width), lambda b, i: (0, 0, 0)),
                  pl.BlockSpec((1, width), const2),
                  pl.BlockSpec((1, width), const2),
                  pl.BlockSpec((1, GDN_DV), const2)],
        out_specs=pl.BlockSpec((c, width), lambda b, i: (b * nc + i, 0)),
        out_shape=jax.ShapeDtypeStruct((batch * seq, width), BF16),
        scratch_shapes=[pltpu.VMEM((c + 8, width), F32)] * 3 + [pltpu.VMEM((GDN_HEADS, GDN_DK, GDN_DV), F32)],
        compiler_params=_cparams(("parallel", "arbitrary")),
        name="gdn_mixer",
    )(hcat, hcat, hcat, hcat, hcat, cw, alog, dtb, out_norm)


def _rwkv_mix_kernel(h_ref, hp_ref, g_ref, mix_ref, o_ref, *, tm, seq):
    i = pl.program_id(0)
    g = g_ref[...]
    xn = _rms(h_ref[...], g)
    prev_last = _rms(hp_ref[...], g)[7:8, :]
    prev_last = jnp.where((i * tm) % seq == 0, 0.0, prev_last)
    rows = lax.broadcasted_iota(jnp.int32, xn.shape, 0)
    xprev = jnp.where(rows == 0, prev_last, pltpu.roll(xn, 1, 0))
    xx = xprev - xn
    for j in range(6):
        o_ref[j] = (xn + xx * mix_ref[j:j + 1, :]).astype(o_ref.dtype)


def rwkv_mix(h, g, mix, seq, *, tm=256):
    m, d = h.shape
    tm = _tile(seq, tm)
    return pl.pallas_call(
        functools.partial(_rwkv_mix_kernel, tm=tm, seq=seq),
        grid=(m // tm,),
        in_specs=[pl.BlockSpec((tm, d), lambda i: (i, 0)),
                  pl.BlockSpec((8, d), lambda i: (jnp.maximum(i * (tm // 8) - 1, 0), 0)),
                  pl.BlockSpec((1, d), lambda i: (0, 0)),
                  pl.BlockSpec((6, d), lambda i: (0, 0))],
        out_specs=pl.BlockSpec((6, tm, d), lambda i: (0, i, 0)),
        out_shape=jax.ShapeDtypeStruct((6, m, d), BF16),
        compiler_params=_cparams(("parallel",)),
        name="rwkv_mix",
    )(h, h, g, mix)


def _lora_kernel(x_ref, a_ref, b_ref, bias_ref, *rest, mid, post):
    o_ref = rest[-1]
    z = _dot(x_ref[...], a_ref[...])
    if mid == "tanh":
        z = jnp.tanh(z)
    elif mid == "sigmoid":
        z = _sigmoid(z)
    y = _dot(z.astype(BF16), b_ref[...]) + bias_ref[...]
    if post == "logdecay":
        y = -jnp.exp(-_softplus(-y) - 0.5)
    elif post == "sigmoid":
        y = _sigmoid(y)
    elif post == "vres":
        v = rest[0][...]
        y = v + (rest[1][...] - v) * _sigmoid(y)
    o_ref[...] = y.astype(o_ref.dtype)


def lora(x, idx, a, b, bias, *extras, mid, post, tm=512):
    _, m, d = x.shape
    r = a.shape[1]
    n = b.shape[1]
    tm = _tile(m, tm)
    row = lambda i: (i, 0)
    const = lambda i: (0, 0)
    return pl.pallas_call(
        functools.partial(_lora_kernel, mid=mid, post=post),
        grid=(m // tm,),
        in_specs=[pl.BlockSpec((None, tm, d), lambda i: (idx, i, 0)),
                  pl.BlockSpec((d, r), const), pl.BlockSpec((r, n), const),
                  pl.BlockSpec((1, n), const)] + [pl.BlockSpec((tm, n), row) for _ in extras],
        out_specs=pl.BlockSpec((tm, n), row),
        out_shape=jax.ShapeDtypeStruct((m, n), F32),
        compiler_params=_cparams(("parallel",)),
        name="lora_" + post,
    )(x, a, b, bias, *extras)


def _rwkv_kernel(r_ref, k_ref, v_ref, a_ref, lw_ref, gt_ref, kk_ref, ka_ref, rk_ref, lnw_ref, lnb_ref,
                 o_ref, state_scr, *, chunk, pairs):
    c = chunk
    ps = range(pairs)
    sls = [slice(p * LANES, (p + 1) * LANES) for p in ps]

    @pl.when(pl.program_id(2) == 0)
    def _():
        state_scr[...] = jnp.zeros_like(state_scr)

    lane = lax.broadcasted_iota(jnp.int32, (c, LANES), 1)
    m0 = lane < RWKV_HEAD
    ri = lax.broadcasted_iota(jnp.int32, (c, c), 0)
    cj = lax.broadcasted_iota(jnp.int32, (c, c), 1)
    tri = (cj <= ri).astype(BF16)
    r2 = lax.broadcasted_iota(jnp.int32, (2 * c, 2 * c), 0)
    c2 = lax.broadcasted_iota(jnp.int32, (2 * c, 2 * c), 1)
    strict2 = c2 < r2
    rw = lax.broadcasted_iota(jnp.int32, (c, 2 * c), 0)
    cw = lax.broadcasted_iota(jnp.int32, (c, 2 * c), 1)
    incl_w = jnp.where(cw >= c, cw - c, cw) <= rw
    zero_blk = jnp.zeros((2 * c, LANES), BF16)

    def seg_sum(x):
        s0 = jnp.sum(jnp.where(m0, x, 0.0), axis=-1, keepdims=True)
        s1 = jnp.sum(jnp.where(m0, 0.0, x), axis=-1, keepdims=True)
        return jnp.where(m0, s0, s1)

    def stack(x):
        return jnp.concatenate([jnp.where(m0, x, 0.0), jnp.where(m0, 0.0, x)], axis=0)

    def bf(x):
        return x.astype(BF16)

    r = [r_ref[:, s] for s in sls]
    k = [k_ref[:, s] for s in sls]
    v = [v_ref[:, s] for s in sls]
    a = [a_ref[:, s] for s in sls]
    lw = [lw_ref[:, s] for s in sls]
    kk = [k[p] * kk_ref[:, sls[p]] for p in ps]
    kk = [x * lax.rsqrt(seg_sum(x * x) + 1e-12) for x in kk]
    kh = [k[p] * (1.0 + (a[p] - 1.0) * ka_ref[:, sls[p]]) for p in ps]
    bv = [kk[p] * a[p] for p in ps]

    cum = [_cumsum_rows(tri, x) for x in lw]
    cl = [x[c - 1:c, :] for x in cum]
    e_neg = [jnp.exp(-x) for x in cum]
    e_end = [jnp.exp(cl[p] - cum[p]) for p in ps]
    rg = [r[p] * jnp.exp(cum[p]) for p in ps]
    rgc = [bf(x) for x in rg]
    xs = [bf(stack(kk[p] * jnp.exp(cum[p] - lw[p]))) for p in ps]
    bs = [bf(stack(bv[p] * e_neg[p])) for p in ps]
    ks = [bf(stack(kh[p] * e_neg[p])) for p in ps]
    vs = [bf(stack(v[p])) for p in ps]
    kht = [stack(kh[p] * e_end[p]).T for p in ps]
    bht = [stack(bv[p] * e_end[p]).T for p in ps]

    ab = [_dot_nt(xs[p], jnp.concatenate([bs[p], ks[p]], axis=0)) for p in ps]
    akk = [jnp.where(strict2, x[:, :2 * c], 0.0) for x in ab]
    akv = [bf(jnp.where(strict2, x[:, 2 * c:], 0.0)) for x in ab]
    tinv = [bf(x) for x in _unit_lower_inverse(akk, 6, None)]
    w1 = [bf(_dot(akv[p], vs[p])) for p in ps]
    m12 = [bf(_dot(tinv[p], jnp.concatenate([xs[p], w1[p]], axis=1))) for p in ps]
    bb = [_dot_nt(rgc[p], jnp.concatenate([ks[p], bs[p]], axis=0)) for p in ps]
    lhs = [bf(jnp.concatenate([
        jnp.concatenate([jnp.where(incl_w, x[:, :2 * c], 0.0), -jnp.where(incl_w, x[:, 2 * c:], 0.0)], axis=1),
        jnp.concatenate([kht[p], -bht[p]], axis=1)], axis=0)) for p, x in enumerate(bb)]
    rhs = [jnp.concatenate([jnp.concatenate([vs[p], zero_blk], axis=1),
                            jnp.concatenate([m12[p][:, LANES:], m12[p][:, :LANES]], axis=1)], axis=0)
           for p in ps]
    z = [_dot(lhs[p], rhs[p]) for p in ps]
    state = [state_scr[p] for p in ps]
    l2 = [bf(jnp.concatenate([rg[p] + z[p][:c, LANES:], z[p][c:, LANES:]], axis=0)) for p in ps]
    rr = [_dot(l2[p], bf(state[p])) for p in ps]
    for p in ps:
        sl = sls[p]
        state_scr[p] = state[p] * _col_of_row(jnp.exp(cl[p]), LANES) + rr[p][c:] + z[p][c:, :LANES]
        y = rr[p][:c] + z[p][:c, :LANES]
        mu = seg_sum(y) * (1.0 / RWKV_HEAD)
        yc = y - mu
        var = seg_sum(yc * yc) * (1.0 / RWKV_HEAD)
        yn = yc * lax.rsqrt(var + RWKV_LN_EPS) * lnw_ref[:, sl] + lnb_ref[:, sl]
        bonus = seg_sum(r[p] * kh[p] * rk_ref[:, sl]) * v[p]
        o_ref[:, sl] = ((yn + bonus) * gt_ref[:, sl]).astype(o_ref.dtype)


def rwkv_recurrence(r, k, v, a, lw, gate, k_k, k_a, r_k, ln_w, ln_b, batch, seq, *, pairs=8):
    c = RWKV_CHUNK
    m, d = r.shape
    nc = seq // c
    width = pairs * LANES
    blk = pl.BlockSpec((c, width), lambda b, hp, i: (b * nc + i, hp))
    par = pl.BlockSpec((1, width), lambda b, hp, i: (0, hp))
    return pl.pallas_call(
        functools.partial(_rwkv_kernel, chunk=c, pairs=pairs),
        grid=(batch, d // width, nc),
        in_specs=[blk] * 6 + [par] * 5,
        out_specs=blk,
        out_shape=jax.ShapeDtypeStruct((m, d), BF16),
        scratch_shapes=[pltpu.VMEM((pairs, LANES, LANES), F32)],
        compiler_params=_cparams(("parallel", "parallel", "arbitrary")),
        name="rwkv_recurrence",
    )(r, k, v, a, lw, gate, k_k, k_a, r_k, ln_w, ln_b)


def _pad_cols(w, n):
    return jnp.pad(w, ((0, 0), (0, n - w.shape[1])))


def _rot_cols(w):
    half = MLA_ROPE // 2
    return jnp.concatenate([-w[..., half:], w[..., :half]], axis=-1)


def _hybrid_layer(h, ct, st, batch, seq, g_pre, g_post, w_in, q_norm, w_uq, kv_norm, w_ukv, conv_w, a_log,
                  dt_bias, out_norm, w_out):
    d = D_MODEL
    o_cq, o_ckv, o_kr = 0, MLA_Q_LORA, MLA_Q_LORA + MLA_KV_LORA
    o_g = o_kr + MLA_ROPE
    o_sm = o_g + 4 * GDN_HEADS * GDN_DK
    w_kr = w_in[:, o_kr:o_g]
    w_cat = jnp.concatenate([w_in[:, :o_kr], w_in[:, o_g:o_sm], w_kr, _rot_cols(w_kr),
                             _pad_cols(w_in[:, o_sm:], LANES)], axis=1).astype(BF16)
    hcat = norm_matmul(h, g_pre.reshape(1, d), w_cat, tn=768)

    wq = w_uq.reshape(MLA_Q_LORA, MLA_HEADS, MLA_QK)
    wq_rope = wq[..., MLA_NOPE:]
    wq = jnp.concatenate([wq[..., :MLA_NOPE], wq_rope, _rot_cols(wq_rope)], axis=-1)
    wq = wq.reshape(MLA_Q_LORA, MLA_HEADS * MLA_HEAD_PAD).astype(BF16)
    q, k, v = mla_proj(hcat, q_norm.reshape(1, -1), kv_norm.reshape(1, -1), wq, w_ukv.astype(BF16), ct, st)
    mla_out = mla_attention(q, k, v, batch, seq)
    gdn_out = gdn_mixer(hcat, conv_w, a_log, dt_bias, out_norm.reshape(1, -1), batch, seq)
    mix = jnp.concatenate([mla_out, gdn_out], axis=1)
    return matmul_norm_residual(mix, w_out.astype(BF16), g_post.reshape(1, d), h, tk=2048)


def _rwkv_layer(h, v_first, batch, seq, g_pre, g_post, mix, w_r, w_k, w_v, w_o, w0, w1, w2, a0, a1, a2,
                g1, g2, k_k, k_a, r_k, ln_w, ln_b, vres):
    d = D_MODEL
    row = lambda t: t.reshape(1, d).astype(F32)
    padc = lambda w: _pad_cols(w, -(-w.shape[1] // LANES) * LANES).astype(BF16)
    padr = lambda w: jnp.pad(w, ((0, -(-w.shape[0] // LANES) * LANES - w.shape[0]), (0, 0))).astype(BF16)
    xs = rwkv_mix(h, g_pre.reshape(1, d), mix, seq)
    r = matmul(xs, 0, w_r.astype(BF16))
    k = matmul(xs, 2, w_k.astype(BF16))
    v = matmul(xs, 3, w_v.astype(BF16))
    lw = lora(xs, 1, padc(w1), padr(w2), row(w0), mid="tanh", post="logdecay")
    a = lora(xs, 4, padc(a1), padr(a2), row(a0), mid="none", post="sigmoid")
    gate = lora(xs, 5, padc(g1), padr(g2), jnp.zeros((1, d), F32), mid="sigmoid", post="none")
    if vres is None:
        v_first = v
    else:
        v0, v1, v2 = vres
        v = lora(xs, 3, padc(v1), padr(v2), row(v0), v, v_first, mid="none", post="vres")
    y = rwkv_recurrence(r, k, v, a, lw, gate, row(k_k), row(k_a), row(r_k), row(ln_w), row(ln_b), batch, seq)
    return matmul_norm_residual(y, w_o.astype(BF16), g_post.reshape(1, d), h, tk=2048), v_first


def _ffn_layer(h, g_pre, g_post, w_gate, w_up, w_down):
    d = D_MODEL
    act = ffn_up(h, g_pre.reshape(1, d), w_gate.astype(BF16), w_up.astype(BF16))
    return matmul_norm_residual(act, w_down.astype(BF16), g_post.reshape(1, d), h)


def _rope_tables(positions):
    inv_freq = 1.0 / (ROPE_THETA ** (jnp.arange(0, MLA_ROPE, 2, dtype=F32) / MLA_ROPE))
    ang = positions.astype(F32).reshape(-1, 1) * inv_freq
    zeros = jnp.zeros((ang.shape[0], LANES - MLA_ROPE), F32)
    cos, sin = jnp.cos(ang), jnp.sin(ang)
    return jnp.concatenate([cos, cos, zeros], axis=1), jnp.concatenate([sin, sin, zeros], axis=1)


def kernel(x, positions, norm_mix_pre, norm_mix_post, norm_ffn_pre, norm_ffn_post, hyb_w_in, mla_q_norm, mla_w_uq, mla_kv_norm, mla_w_ukv, gdn_conv_w, gdn_a_log, gdn_dt_bias, gdn_out_norm, hyb_w_out, rwkv_mix, rwkv_w_r, rwkv_w_k, rwkv_w_v, rwkv_w_o, rwkv_w0, rwkv_w1, rwkv_w2, rwkv_a0, rwkv_a1, rwkv_a2, rwkv_g1, rwkv_g2, rwkv_k_k, rwkv_k_a, rwkv_r_k, rwkv_ln_w, rwkv_ln_b, rwkv_v0, rwkv_v1, rwkv_v2, ffn_w_gate, ffn_w_up, ffn_w_down):
    batch, seq, d = x.shape
    h = x.reshape(batch * seq, d)
    ct, st = _rope_tables(positions)
    v_first = None
    for layer in range(DEPTH):
        if layer % 2 == 0:
            e = layer // 2
            h = _hybrid_layer(h, ct, st, batch, seq, norm_mix_pre[layer], norm_mix_post[layer], hyb_w_in[e],
                              mla_q_norm[e], mla_w_uq[e], mla_kv_norm[e], mla_w_ukv[e], gdn_conv_w[e],
                              gdn_a_log[e], gdn_dt_bias[e], gdn_out_norm[e], hyb_w_out[e])
        else:
            o = layer // 2
            vres = None if o == 0 else (rwkv_v0[o - 1], rwkv_v1[o - 1], rwkv_v2[o - 1])
            h, v_first = _rwkv_layer(h, v_first, batch, seq, norm_mix_pre[layer], norm_mix_post[layer],
                                     rwkv_mix[o], rwkv_w_r[o], rwkv_w_k[o], rwkv_w_v[o], rwkv_w_o[o],
                                     rwkv_w0[o], rwkv_w1[o], rwkv_w2[o], rwkv_a0[o], rwkv_a1[o], rwkv_a2[o],
                                     rwkv_g1[o], rwkv_g2[o], rwkv_k_k[o], rwkv_k_a[o], rwkv_r_k[o],
                                     rwkv_ln_w[o], rwkv_ln_b[o], vres)
        h = _ffn_layer(h, norm_ffn_pre[layer], norm_ffn_post[layer], ffn_w_gate[layer], ffn_w_up[layer],
                       ffn_w_down[layer])
    return h.reshape(batch, seq, d)
```

```python
import functools

import jax
import jax.numpy as jnp
import numpy as np
from jax import lax
from jax.experimental import pallas as pl
from jax.experimental.pallas import tpu as pltpu

F32 = jnp.float32
BF16 = jnp.bfloat16
HI = lax.Precision.HIGHEST

D_MODEL = 2048
DEPTH = 4
LANES = 128
VMEM_LIMIT = 56 * 1024 * 1024

MLA_HEADS = 8
MLA_Q_LORA = 512
MLA_KV_LORA = 512
MLA_NOPE = 128
MLA_ROPE = 64
MLA_V = 128
MLA_QK = MLA_NOPE + MLA_ROPE
MLA_HEAD_PAD = 256
ROPE_THETA = 10000.0

GDN_HEADS = 8
GDN_DK = 128
GDN_DV = 128
GDN_CONV = 4
GDN_CHUNK = 128

RWKV_HEAD = 64
RWKV_HEADS = D_MODEL // RWKV_HEAD
RWKV_CHUNK = 64
RWKV_LN_EPS = 64e-5
RMS_EPS = 1e-6

HC_CQ, HC_CKV, HC_GQ, HC_GK, HC_GV, HC_GZ, HC_KR, HC_SM = 0, 512, 1024, 2048, 3072, 4096, 5120, 5248
HC_COLS = 5376


def _cparams(sem):
    return pltpu.CompilerParams(dimension_semantics=sem, vmem_limit_bytes=VMEM_LIMIT)


def _tile(n, pref):
    t = min(n, pref)
    while n % t:
        t -= 8
    return t


def _dot(a, b, prec=None):
    return jnp.dot(a, b, preferred_element_type=F32, precision=prec)


def _dot_nt(a, b, prec=None):
    return lax.dot_general(a, b, (((1,), (1,)), ((), ())), preferred_element_type=F32, precision=prec)


def _rms(x, g, eps=RMS_EPS):
    return x * lax.rsqrt(jnp.mean(x * x, axis=-1, keepdims=True) + eps) * g


def _sigmoid(x):
    return 1.0 / (1.0 + jnp.exp(-x))


def _silu(x):
    return x * _sigmoid(x)


def _softplus(x):
    return jnp.maximum(x, 0.0) + jnp.log(1.0 + jnp.exp(-jnp.abs(x)))


def _norm_mm_kernel(x_ref, g_ref, w_ref, o_ref, xn_ref):
    @pl.when(pl.program_id(1) == 0)
    def _():
        xn_ref[...] = _rms(x_ref[...], g_ref[...]).astype(BF16)

    o_ref[...] = _dot(xn_ref[...], w_ref[...]).astype(o_ref.dtype)


def norm_matmul(x, g, w, *, tn, out_dtype=F32, tm=512):
    m, k = x.shape
    n = w.shape[1]
    tm = _tile(m, tm)
    return pl.pallas_call(
        _norm_mm_kernel,
        grid=(m // tm, n // tn),
        in_specs=[pl.BlockSpec((tm, k), lambda i, j: (i, 0)),
                  pl.BlockSpec((1, k), lambda i, j: (0, 0)),
                  pl.BlockSpec((k, tn), lambda i, j: (0, j))],
        out_specs=pl.BlockSpec((tm, tn), lambda i, j: (i, j)),
        out_shape=jax.ShapeDtypeStruct((m, n), out_dtype),
        scratch_shapes=[pltpu.VMEM((tm, k), BF16)],
        compiler_params=_cparams(("parallel", "arbitrary")),
        name="norm_matmul",
    )(x, g, w)


def _ffn_up_kernel(x_ref, g_ref, wg_ref, wu_ref, o_ref, xn_ref):
    @pl.when(pl.program_id(1) == 0)
    def _():
        xn_ref[...] = _rms(x_ref[...], g_ref[...]).astype(BF16)

    xn = xn_ref[...]
    gate = _dot(xn, wg_ref[...])
    up = _dot(xn, wu_ref[...])
    o_ref[...] = (_silu(gate) * up).astype(o_ref.dtype)


def ffn_up(h, g, wg, wu, *, tm=512, tn=1408):
    m, k = h.shape
    n = wg.shape[1]
    tm = _tile(m, tm)
    tn = _tile(n, tn)
    return pl.pallas_call(
        _ffn_up_kernel,
        grid=(m // tm, n // tn),
        in_specs=[pl.BlockSpec((tm, k), lambda i, j: (i, 0)),
                  pl.BlockSpec((1, k), lambda i, j: (0, 0)),
                  pl.BlockSpec((k, tn), lambda i, j: (0, j)),
                  pl.BlockSpec((k, tn), lambda i, j: (0, j))],
        out_specs=pl.BlockSpec((tm, tn), lambda i, j: (i, j)),
        out_shape=jax.ShapeDtypeStruct((m, n), BF16),
        scratch_shapes=[pltpu.VMEM((tm, k), BF16)],
        compiler_params=_cparams(("parallel", "arbitrary")),
        name="ffn_up",
    )(h, g, wg, wu)


def _mm_norm_res_kernel(a_ref, w_ref, g_ref, h_ref, o_ref, f_ref, *, tn, nb):
    j = pl.program_id(1)
    f_ref[j] = _dot(a_ref[...], w_ref[...])

    @pl.when(j == nb - 1)
    def _():
        ssq = jnp.sum(f_ref[0] * f_ref[0], axis=-1, keepdims=True)
        for b in range(1, nb):
            ssq = ssq + jnp.sum(f_ref[b] * f_ref[b], axis=-1, keepdims=True)
        inv = lax.rsqrt(ssq * (1.0 / (nb * tn)) + RMS_EPS)
        for b in range(nb):
            cs = slice(b * tn, (b + 1) * tn)
            o_ref[:, cs] = h_ref[:, cs] + f_ref[b] * inv * g_ref[:, cs]


def matmul_norm_residual(a, w, g, h, *, tm=512, tn=512):
    m, k = a.shape
    n = w.shape[1]
    tm = _tile(m, tm)
    tn = _tile(n, tn)
    nb = n // tn
    return pl.pallas_call(
        functools.partial(_mm_norm_res_kernel, tn=tn, nb=nb),
        grid=(m // tm, nb),
        in_specs=[pl.BlockSpec((tm, k), lambda i, j: (i, 0)),
                  pl.BlockSpec((k, tn), lambda i, j: (0, j)),
                  pl.BlockSpec((1, n), lambda i, j: (0, 0)),
                  pl.BlockSpec((tm, n), lambda i, j: (i, 0))],
        out_specs=pl.BlockSpec((tm, n), lambda i, j: (i, 0)),
        out_shape=jax.ShapeDtypeStruct((m, n), F32),
        scratch_shapes=[pltpu.VMEM((nb, tm, tn), F32)],
        compiler_params=_cparams(("parallel", "arbitrary")),
        name="matmul_norm_residual",
    )(a, w, g, h)


def _mm_kernel(a_ref, w_ref, o_ref):
    o_ref[...] = _dot(a_ref[...], w_ref[...]).astype(o_ref.dtype)


def matmul(a, idx, w, *, tm=512, tn=1024, out_dtype=F32):
    _, m, k = a.shape
    n = w.shape[1]
    tm = _tile(m, tm)
    tn = _tile(n, tn)
    return pl.pallas_call(
        _mm_kernel,
        grid=(m // tm, n // tn),
        in_specs=[pl.BlockSpec((None, tm, k), lambda i, j: (idx, i, 0)),
                  pl.BlockSpec((k, tn), lambda i, j: (0, j))],
        out_specs=pl.BlockSpec((tm, tn), lambda i, j: (i, j)),
        out_shape=jax.ShapeDtypeStruct((m, n), out_dtype),
        compiler_params=_cparams(("parallel", "arbitrary")),
        name="matmul",
    )(a, w)


def _rope_group(grp, ct, st):
    return grp * ct + pltpu.roll(grp, 64, 1) * st


def _mla_proj_kernel(cq_ref, ckv_ref, kr_ref, qn_ref, kvn_ref, wq_ref, wkv_ref, ct_ref, st_ref,
                     q_ref, k_ref, v_ref):
    scale = MLA_QK ** -0.5 * np.log2(np.e)
    ct = ct_ref[...]
    st = st_ref[...]
    xq = _rms(cq_ref[...], qn_ref[...]).astype(BF16)
    xkv = _rms(ckv_ref[...], kvn_ref[...]).astype(BF16)
    kr = _rope_group(kr_ref[...], ct, st).astype(BF16)
    for h in range(MLA_HEADS):
        c0 = h * MLA_HEAD_PAD
        yq = _dot(xq, wq_ref[:, c0:c0 + MLA_HEAD_PAD])
        q_ref[:, c0:c0 + LANES] = (yq[:, :LANES] * scale).astype(BF16)
        q_ref[:, c0 + LANES:c0 + 2 * LANES] = (_rope_group(yq[:, LANES:], ct, st) * scale).astype(BF16)
        ykv = _dot(xkv, wkv_ref[:, c0:c0 + MLA_HEAD_PAD])
        k_ref[:, c0:c0 + LANES] = ykv[:, :LANES].astype(BF16)
        k_ref[:, c0 + LANES:c0 + 2 * LANES] = kr
        v_ref[:, h * MLA_V:(h + 1) * MLA_V] = ykv[:, LANES:].astype(BF16)


def mla_proj(hcat, q_norm, kv_norm, wq, wkv, ct, st, *, tm=512):
    m = hcat.shape[0]
    tm = _tile(m, tm)
    hp = MLA_HEADS * MLA_HEAD_PAD
    row = lambda i: (i, 0)
    const = lambda i: (0, 0)
    return pl.pallas_call(
        _mla_proj_kernel,
        grid=(m // tm,),
        in_specs=[pl.BlockSpec((tm, MLA_Q_LORA), lambda i: (i, HC_CQ // MLA_Q_LORA)),
                  pl.BlockSpec((tm, MLA_KV_LORA), lambda i: (i, HC_CKV // MLA_KV_LORA)),
                  pl.BlockSpec((tm, LANES), lambda i: (i, HC_KR // LANES)),
                  pl.BlockSpec((1, MLA_Q_LORA), const),
                  pl.BlockSpec((1, MLA_KV_LORA), const),
                  pl.BlockSpec((MLA_Q_LORA, hp), const),
                  pl.BlockSpec((MLA_KV_LORA, hp), const),
                  pl.BlockSpec((tm, LANES), row),
                  pl.BlockSpec((tm, LANES), row)],
        out_specs=[pl.BlockSpec((tm, hp), row),
                   pl.BlockSpec((tm, hp), row),
                   pl.BlockSpec((tm, MLA_HEADS * MLA_V), row)],
        out_shape=[jax.ShapeDtypeStruct((m, hp), BF16),
                   jax.ShapeDtypeStruct((m, hp), BF16),
                   jax.ShapeDtypeStruct((m, MLA_HEADS * MLA_V), BF16)],
        compiler_params=_cparams(("parallel",)),
        name="mla_proj",
    )(hcat, hcat, hcat, q_norm, kv_norm, wq, wkv, ct, st)


def _attn_kernel(q_ref, k_ref, v_ref, o_ref, *, tq, tk):
    qi = pl.program_id(2)
    q = q_ref[...]

    def step(ki, carry, masked):
        m_prev, l_prev, acc = carry
        start = pl.multiple_of(ki * tk, tk)
        k = k_ref[pl.ds(start, tk), :]
        v = v_ref[pl.ds(start, tk), :]
        s = _dot_nt(q, k)
        if masked is not None:
            row = lax.broadcasted_iota(jnp.int32, (tq, tk), 0)
            col = lax.broadcasted_iota(jnp.int32, (tq, tk), 1) + masked * tk
            s = jnp.where(col <= row, s, -1e30)
        m_new = jnp.maximum(m_prev, jnp.max(s, axis=-1, keepdims=True))
        alpha = jnp.exp2(m_prev - m_new)
        p = jnp.exp2(s - m_new)
        l_new = alpha * l_prev + jnp.sum(p, axis=-1, keepdims=True)
        acc = alpha * acc + _dot(p.astype(BF16), v)
        return m_new, l_new, acc

    init = (jnp.full((tq, 1), -1e30, F32), jnp.zeros((tq, 1), F32), jnp.zeros((tq, MLA_V), F32))
    per_q = tq // tk
    carry = lax.fori_loop(0, qi * per_q, lambda ki, c: step(ki, c, None), init)
    for j in range(per_q):
        carry = step(qi * per_q + j, carry, j)
    _, l_fin, acc = carry
    o_ref[...] = (acc / l_fin).astype(o_ref.dtype)


def mla_attention(q, k, v, batch, seq, *, tq=512, tk=512):
    tq = _tile(seq, tq)
    tk = _tile(tq, tk)
    nq = seq // tq
    return pl.pallas_call(
        functools.partial(_attn_kernel, tq=tq, tk=tk),
        grid=(batch, MLA_HEADS, nq),
        in_specs=[pl.BlockSpec((tq, MLA_HEAD_PAD), lambda b, h, i: (b * nq + i, h)),
                  pl.BlockSpec((seq, MLA_HEAD_PAD), lambda b, h, i: (b, h)),
                  pl.BlockSpec((seq, MLA_V), lambda b, h, i: (b, h))],
        out_specs=pl.BlockSpec((tq, MLA_V), lambda b, h, i: (b * nq + i, h)),
        out_shape=jax.ShapeDtypeStruct((batch * seq, MLA_HEADS * MLA_V), BF16),
        compiler_params=_cparams(("parallel", "parallel", "arbitrary")),
        name="mla_attention",
    )(q, k, v)


def _split2(x):
    hi = x.astype(BF16)
    return hi, (x - hi.astype(F32)).astype(BF16)


def _dot_x3(a, b):
    ah, al = _split2(a)
    bh, bl = _split2(b)
    return _dot(jnp.concatenate([ah, al, ah], axis=1), jnp.concatenate([bh, bh, bl], axis=0))


def _dot_nt_x3(a, b):
    ah, al = _split2(a)
    bh, bl = _split2(b)
    return _dot_nt(jnp.concatenate([ah, al, ah], axis=1), jnp.concatenate([bh, bh, bl], axis=1))


def _dot_bf(a, b):
    return _dot(a.astype(BF16), b.astype(BF16))


def _unit_lower_inverse(a_list, levels, dot):
    n = a_list[0].shape[0]
    eye = (lax.broadcasted_iota(jnp.int32, (n, n), 0) == lax.broadcasted_iota(jnp.int32, (n, n), 1)).astype(F32)
    p = [-a for a in a_list]
    r = [eye + x for x in p]
    p = [dot(x, x) for x in p]
    for _ in range(levels - 2):
        s = [dot(jnp.concatenate([ri, pi], axis=0), pi) for ri, pi in zip(r, p)]
        r = [ri + si[:n] for ri, si in zip(r, s)]
        p = [si[n:] for si in s]
    return [ri + dot(ri, pi) for ri, pi in zip(r, p)]


def _cumsum_rows(tri, x):
    hi = x.astype(BF16)
    r1 = x - hi.astype(F32)
    mid = r1.astype(BF16)
    lo = (r1 - mid.astype(F32)).astype(BF16)
    return _dot(tri, hi) + _dot(tri, mid) + _dot(tri, lo)


def _col_of_row(row_vec, n):
    eye = lax.broadcasted_iota(jnp.int32, (n, n), 0) == lax.broadcasted_iota(jnp.int32, (n, n), 1)
    return jnp.sum(jnp.where(eye, jnp.broadcast_to(row_vec, (n, n)), 0.0), axis=1, keepdims=True)


def _gdn_kernel(gq_ref, gk_ref, gv_ref, gz_ref, sm_ref, cw_ref, alog_ref, dtb_ref, on_ref, o_ref,
                xq_scr, xk_scr, xv_scr, state_scr, *, chunk):
    c = chunk
    heads = range(GDN_HEADS)
    hs = [slice(h * LANES, (h + 1) * LANES) for h in heads]

    @pl.when(pl.program_id(1) == 0)
    def _():
        for scr in (xq_scr, xk_scr, xv_scr):
            scr[0:8, :] = jnp.zeros((8, scr.shape[1]), F32)
        state_scr[...] = jnp.zeros_like(state_scr)

    def conv_silu(x_ref, scr, widx):
        scr[8:8 + c, :] = x_ref[...]
        acc = cw_ref[widx, GDN_CONV - 1:GDN_CONV, :] * scr[8:8 + c, :]
        for j in range(GDN_CONV - 1):
            s = GDN_CONV - 1 - j
            acc = acc + cw_ref[widx, j:j + 1, :] * scr[8 - s:8 - s + c, :]
        scr[0:8, :] = scr[c:c + 8, :]
        return _silu(acc)

    def l2n(x):
        return x * lax.rsqrt(jnp.sum(x * x, axis=-1, keepdims=True) + 1e-6)

    def bf(x):
        return x.astype(BF16)

    qa = conv_silu(gq_ref, xq_scr, 0)
    ka = conv_silu(gk_ref, xk_scr, 1)
    va = conv_silu(gv_ref, xv_scr, 2)
    q = [l2n(qa[:, s]) * (GDN_DK ** -0.5) for s in hs]
    k = [l2n(ka[:, s]) for s in hs]
    v = [va[:, s] for s in hs]

    sm = sm_ref[...]
    lane = lax.broadcasted_iota(jnp.int32, (c, LANES), 1)
    beta = [_sigmoid(jnp.sum(jnp.where(lane == h, sm, 0.0), axis=-1, keepdims=True)) for h in heads]
    ga = [jnp.sum(jnp.where(lane == GDN_HEADS + h, sm, 0.0), axis=-1, keepdims=True) for h in heads]
    g = [-jnp.exp(alog_ref[:, hs[h]]) * _softplus(jnp.broadcast_to(ga[h], (c, LANES)) + dtb_ref[:, hs[h]])
         for h in heads]

    ri = lax.broadcasted_iota(jnp.int32, (c, c), 0)
    cj = lax.broadcasted_iota(jnp.int32, (c, c), 1)
    tri = (cj <= ri).astype(BF16)
    gam = [_cumsum_rows(tri, x) for x in g]
    decay = [jnp.exp(jnp.where(cj <= ri, x - x.T, -1e30)) for x in gam]
    eg = [jnp.exp(x) for x in gam]
    gl = [x[c - 1:c, :] for x in gam]
    kb = [k[h] * beta[h] for h in heads]
    vb = [v[h] * beta[h] for h in heads]

    a_mat = [jnp.where(cj < ri, _dot_nt_x3(kb[h], k[h]) * decay[h], 0.0) for h in heads]
    tinv = _unit_lower_inverse(a_mat, 7, _dot_x3)
    rhs = [jnp.concatenate([kb[h] * eg[h], vb[h]], axis=1) for h in heads]
    wu = [bf(_dot_x3(tinv[h], rhs[h])) for h in heads]
    attn = [_dot_nt(bf(q[h]), bf(k[h])) * decay[h] for h in heads]
    k_dec = [k[h] * jnp.exp(gl[h] - gam[h]) for h in heads]
    lhs = [bf(jnp.concatenate([attn[h], k_dec[h].T], axis=0)) for h in heads]
    z = [_dot(lhs[h], wu[h]) for h in heads]
    state = [state_scr[h] for h in heads]
    l2 = [bf(jnp.concatenate([q[h] * eg[h] - z[h][:c, :LANES], -z[h][c:, :LANES]], axis=0)) for h in heads]
    rr = [_dot(l2[h], bf(state[h])) for h in heads]
    for h in heads:
        state_scr[h] = state[h] * jnp.exp(gl[h]) + rr[h][c:] + z[h][c:, LANES:]
        o = rr[h][:c] + z[h][:c, LANES:]
        o_ref[:, hs[h]] = (_rms(o, on_ref[...]) * _silu(gz_ref[:, hs[h]])).astype(o_ref.dtype)


def gdn_mixer(hcat, conv_w, a_log, dt_bias, out_norm, batch, seq):
    c = GDN_CHUNK
    assert c == LANES and seq % c == 0
    nc = seq // c
    width = GDN_HEADS * LANES
    blk = lambda off: pl.BlockSpec((c, width), lambda b, i, off=off: (b * nc + i, off // width))
    const2 = lambda b, i: (0, 0)
    cw = conv_w.reshape(GDN_CONV, 3, width).transpose(1, 0, 2)
    alog = jnp.repeat(a_log.astype(F32), LANES).reshape(1, width)
    dtb = jnp.repeat(dt_bias.astype(F32), LANES).reshape(1, width)
    return pl.pallas_call(
        functools.partial(_gdn_kernel, chunk=c),
        grid=(batch, nc),
        in_specs=[blk(HC_GQ), blk(HC_GK), blk(HC_GV), blk(HC_GZ),
                  pl.BlockSpec((c, LANES), lambda b, i: (b * nc + i, HC_SM // LANES)),
                  pl.BlockSpec((3, GDN_CONV, width), lambda b, i: (0, 0, 0)),
                  pl.BlockSpec((1, width), const2),
                  pl.BlockSpec((1, width), const2),
                  pl.BlockSpec((1, GDN_DV), const2)],
        out_specs=pl.BlockSpec((c, width), lambda b, i: (b * nc + i, 0)),
        out_shape=jax.ShapeDtypeStruct((batch * seq, width), BF16),
        scratch_shapes=[pltpu.VMEM((c + 8, width), F32)] * 3 + [pltpu.VMEM((GDN_HEADS, GDN_DK, GDN_DV), F32)],
        compiler_params=_cparams(("parallel", "arbitrary")),
        name="gdn_mixer",
    )(hcat, hcat, hcat, hcat, hcat, cw, alog, dtb, out_norm)


def _rwkv_mix_kernel(h_ref, hp_ref, g_ref, mix_ref, o_ref, *, tm, seq):
    i = pl.program_id(0)
    g = g_ref[...]
    xn = _rms(h_ref[...], g)
    prev_last = _rms(hp_ref[...], g)[7:8, :]
    prev_last = jnp.where((i * tm) % seq == 0, 0.0, prev_last)
    rows = lax.broadcasted_iota(jnp.int32, xn.shape, 0)
    xprev = jnp.where(rows == 0, prev_last, pltpu.roll(xn, 1, 0))
    xx = xprev - xn
    for j in range(6):
        o_ref[j] = (xn + xx * mix_ref[j:j + 1, :]).astype(o_ref.dtype)


def rwkv_mix(h, g, mix, seq, *, tm=256):
    m, d = h.shape
    tm = _tile(seq, tm)
    return pl.pallas_call(
        functools.partial(_rwkv_mix_kernel, tm=tm, seq=seq),
        grid=(m // tm,),
        in_specs=[pl.BlockSpec((tm, d), lambda i: (i, 0)),
                  pl.BlockSpec((8, d), lambda i: (jnp.maximum(i * (tm // 8) - 1, 0), 0)),
                  pl.BlockSpec((1, d), lambda i: (0, 0)),
                  pl.BlockSpec((6, d), lambda i: (0, 0))],
        out_specs=pl.BlockSpec((6, tm, d), lambda i: (0, i, 0)),
        out_shape=jax.ShapeDtypeStruct((6, m, d), BF16),
        compiler_params=_cparams(("parallel",)),
        name="rwkv_mix",
    )(h, h, g, mix)


def _lora_kernel(x_ref, a_ref, b_ref, bias_ref, *rest, mid, post):
    o_ref = rest[-1]
    z = _dot(x_ref[...], a_ref[...])
    if mid == "tanh":
        z = jnp.tanh(z)
    elif mid == "sigmoid":
        z = _sigmoid(z)
    y = _dot(z.astype(BF16), b_ref[...]) + bias_ref[...]
    if post == "logdecay":
        y = -jnp.exp(-_softplus(-y) - 0.5)
    elif post == "sigmoid":
        y = _sigmoid(y)
    elif post == "vres":
        v = rest[0][...]
        y = v + (rest[1][...] - v) * _sigmoid(y)
    o_ref[...] = y.astype(o_ref.dtype)


def lora(x, idx, a, b, bias, *extras, mid, post, tm=512):
    _, m, d = x.shape
    r = a.shape[1]
    n = b.shape[1]
    tm = _tile(m, tm)
    row = lambda i: (i, 0)
    const = lambda i: (0, 0)
    return pl.pallas_call(
        functools.partial(_lora_kernel, mid=mid, post=post),
        grid=(m // tm,),
        in_specs=[pl.BlockSpec((None, tm, d), lambda i: (idx, i, 0)),
                  pl.BlockSpec((d, r), const), pl.BlockSpec((r, n), const),
                  pl.BlockSpec((1, n), const)] + [pl.BlockSpec((tm, n), row) for _ in extras],
        out_specs=pl.BlockSpec((tm, n), row),
        out_shape=jax.ShapeDtypeStruct((m, n), F32),
        compiler_params=_cparams(("parallel",)),
        name="lora_" + post,
    )(x, a, b, bias, *extras)


def _rwkv_kernel(r_ref, k_ref, v_ref, a_ref, lw_ref, gt_ref, kk_ref, ka_ref, rk_ref, lnw_ref, lnb_ref,
                 o_ref, state_scr, *, chunk, pairs):
    c = chunk
    ps = range(pairs)
    sls = [slice(p * LANES, (p + 1) * LANES) for p in ps]

    @pl.when(pl.program_id(2) == 0)
    def _():
        state_scr[...] = jnp.zeros_like(state_scr)

    lane = lax.broadcasted_iota(jnp.int32, (c, LANES), 1)
    m0 = lane < RWKV_HEAD
    ri = lax.broadcasted_iota(jnp.int32, (c, c), 0)
    cj = lax.broadcasted_iota(jnp.int32, (c, c), 1)
    tri = (cj <= ri).astype(BF16)
    r2 = lax.broadcasted_iota(jnp.int32, (2 * c, 2 * c), 0)
    c2 = lax.broadcasted_iota(jnp.int32, (2 * c, 2 * c), 1)
    strict2 = c2 < r2
    rw = lax.broadcasted_iota(jnp.int32, (c, 2 * c), 0)
    cw = lax.broadcasted_iota(jnp.int32, (c, 2 * c), 1)
    incl_w = jnp.where(cw >= c, cw - c, cw) <= rw
    zero_blk = jnp.zeros((2 * c, LANES), BF16)

    def seg_sum(x):
        s0 = jnp.sum(jnp.where(m0, x, 0.0), axis=-1, keepdims=True)
        s1 = jnp.sum(jnp.where(m0, 0.0, x), axis=-1, keepdims=True)
        return jnp.where(m0, s0, s1)

    def stack(x):
        return jnp.concatenate([jnp.where(m0, x, 0.0), jnp.where(m0, 0.0, x)], axis=0)

    def bf(x):
        return x.astype(BF16)

    r = [r_ref[:, s] for s in sls]
    k = [k_ref[:, s] for s in sls]
    v = [v_ref[:, s] for s in sls]
    a = [a_ref[:, s] for s in sls]
    lw = [lw_ref[:, s] for s in sls]
    kk = [k[p] * kk_ref[:, sls[p]] for p in ps]
    kk = [x * lax.rsqrt(seg_sum(x * x) + 1e-12) for x in kk]
    kh = [k[p] * (1.0 + (a[p] - 1.0) * ka_ref[:, sls[p]]) for p in ps]
    bv = [kk[p] * a[p] for p in ps]

    cum = [_cumsum_rows(tri, x) for x in lw]
    cl = [x[c - 1:c, :] for x in cum]
    e_neg = [jnp.exp(-x) for x in cum]
    e_end = [jnp.exp(cl[p] - cum[p]) for p in ps]
    rg = [r[p] * jnp.exp(cum[p]) for p in ps]
    rgc = [bf(x) for x in rg]
    xs = [bf(stack(kk[p] * jnp.exp(cum[p] - lw[p]))) for p in ps]
    bs = [bf(stack(bv[p] * e_neg[p])) for p in ps]
    ks = [bf(stack(kh[p] * e_neg[p])) for p in ps]
    vs = [bf(stack(v[p])) for p in ps]
    kht = [stack(kh[p] * e_end[p]).T for p in ps]
    bht = [stack(bv[p] * e_end[p]).T for p in ps]

    ab = [_dot_nt(xs[p], jnp.concatenate([bs[p], ks[p]], axis=0)) for p in ps]
    akk = [jnp.where(strict2, x[:, :2 * c], 0.0) for x in ab]
    akv = [bf(jnp.where(strict2, x[:, 2 * c:], 0.0)) for x in ab]
    tinv = [bf(x) for x in _unit_lower_inverse(akk, 6, _dot_bf)]
    w1 = [bf(_dot(akv[p], vs[p])) for p in ps]
    m12 = [bf(_dot(tinv[p], jnp.concatenate([xs[p], w1[p]], axis=1))) for p in ps]
    bb = [_dot_nt(rgc[p], jnp.concatenate([ks[p], bs[p]], axis=0)) for p in ps]
    lhs = [bf(jnp.concatenate([
        jnp.concatenate([jnp.where(incl_w, x[:, :2 * c], 0.0), -jnp.where(incl_w, x[:, 2 * c:], 0.0)], axis=1),
        jnp.concatenate([kht[p], -bht[p]], axis=1)], axis=0)) for p, x in enumerate(bb)]
    rhs = [jnp.concatenate([jnp.concatenate([vs[p], zero_blk], axis=1),
                            jnp.concatenate([m12[p][:, LANES:], m12[p][:, :LANES]], axis=1)], axis=0)
           for p in ps]
    z = [_dot(lhs[p], rhs[p]) for p in ps]
    state = [state_scr[p] for p in ps]
    l2 = [bf(jnp.concatenate([rg[p] + z[p][:c, LANES:], z[p][c:, LANES:]], axis=0)) for p in ps]
    rr = [_dot(l2[p], bf(state[p])) for p in ps]
    for p in ps:
        sl = sls[p]
        state_scr[p] = state[p] * _col_of_row(jnp.exp(cl[p]), LANES) + rr[p][c:] + z[p][c:, :LANES]
        y = rr[p][:c] + z[p][:c, :LANES]
        mu = seg_sum(y) * (1.0 / RWKV_HEAD)
        yc = y - mu
        var = seg_sum(yc * yc) * (1.0 / RWKV_HEAD)
        yn = yc * lax.rsqrt(var + RWKV_LN_EPS) * lnw_ref[:, sl] + lnb_ref[:, sl]
        bonus = seg_sum(r[p] * kh[p] * rk_ref[:, sl]) * v[p]
        o_ref[:, sl] = ((yn + bonus) * gt_ref[:, sl]).astype(o_ref.dtype)


def rwkv_recurrence(r, k, v, a, lw, gate, k_k, k_a, r_k, ln_w, ln_b, batch, seq, *, pairs=8):
    c = RWKV_CHUNK
    m, d = r.shape
    nc = seq // c
    width = pairs * LANES
    blk = pl.BlockSpec((c, width), lambda b, hp, i: (b * nc + i, hp))
    par = pl.BlockSpec((1, width), lambda b, hp, i: (0, hp))
    return pl.pallas_call(
        functools.partial(_rwkv_kernel, chunk=c, pairs=pairs),
        grid=(batch, d // width, nc),
        in_specs=[blk] * 6 + [par] * 5,
        out_specs=blk,
        out_shape=jax.ShapeDtypeStruct((m, d), BF16),
        scratch_shapes=[pltpu.VMEM((pairs, LANES, LANES), F32)],
        compiler_params=_cparams(("parallel", "parallel", "arbitrary")),
        name="rwkv_recurrence",
    )(r, k, v, a, lw, gate, k_k, k_a, r_k, ln_w, ln_b)


def _pad_cols(w, n):
    return jnp.pad(w, ((0, 0), (0, n - w.shape[1])))


def _rot_cols(w):
    half = MLA_ROPE // 2
    return jnp.concatenate([-w[..., half:], w[..., :half]], axis=-1)


def _hybrid_layer(h, ct, st, batch, seq, g_pre, g_post, w_in, q_norm, w_uq, kv_norm, w_ukv, conv_w, a_log,
                  dt_bias, out_norm, w_out):
    d = D_MODEL
    o_cq, o_ckv, o_kr = 0, MLA_Q_LORA, MLA_Q_LORA + MLA_KV_LORA
    o_g = o_kr + MLA_ROPE
    o_sm = o_g + 4 * GDN_HEADS * GDN_DK
    w_kr = w_in[:, o_kr:o_g]
    w_cat = jnp.concatenate([w_in[:, :o_kr], w_in[:, o_g:o_sm], w_kr, _rot_cols(w_kr),
                             _pad_cols(w_in[:, o_sm:], LANES)], axis=1).astype(BF16)
    hcat = norm_matmul(h, g_pre.reshape(1, d), w_cat, tn=768)

    wq = w_uq.reshape(MLA_Q_LORA, MLA_HEADS, MLA_QK)
    wq_rope = wq[..., MLA_NOPE:]
    wq = jnp.concatenate([wq[..., :MLA_NOPE], wq_rope, _rot_cols(wq_rope)], axis=-1)
    wq = wq.reshape(MLA_Q_LORA, MLA_HEADS * MLA_HEAD_PAD).astype(BF16)
    q, k, v = mla_proj(hcat, q_norm.reshape(1, -1), kv_norm.reshape(1, -1), wq, w_ukv.astype(BF16), ct, st)
    mla_out = mla_attention(q, k, v, batch, seq)
    gdn_out = gdn_mixer(hcat, conv_w, a_log, dt_bias, out_norm.reshape(1, -1), batch, seq)
    mix = jnp.concatenate([mla_out, gdn_out], axis=1)
    return matmul_norm_residual(mix, w_out.astype(BF16), g_post.reshape(1, d), h)


def _rwkv_layer(h, v_first, batch, seq, g_pre, g_post, mix, w_r, w_k, w_v, w_o, w0, w1, w2, a0, a1, a2,
                g1, g2, k_k, k_a, r_k, ln_w, ln_b, vres):
    d = D_MODEL
    row = lambda t: t.reshape(1, d).astype(F32)
    padc = lambda w: _pad_cols(w, -(-w.shape[1] // LANES) * LANES).astype(BF16)
    padr = lambda w: jnp.pad(w, ((0, -(-w.shape[0] // LANES) * LANES - w.shape[0]), (0, 0))).astype(BF16)
    xs = rwkv_mix(h, g_pre.reshape(1, d), mix, seq)
    r = matmul(xs, 0, w_r.astype(BF16))
    k = matmul(xs, 2, w_k.astype(BF16))
    v = matmul(xs, 3, w_v.astype(BF16))
    lw = lora(xs, 1, padc(w1), padr(w2), row(w0), mid="tanh", post="logdecay")
    a = lora(xs, 4, padc(a1), padr(a2), row(a0), mid="none", post="sigmoid")
    gate = lora(xs, 5, padc(g1), padr(g2), jnp.zeros((1, d), F32), mid="sigmoid", post="none")
    if vres is None:
        v_first = v
    else:
        v0, v1, v2 = vres
        v = lora(xs, 3, padc(v1), padr(v2), row(v0), v, v_first, mid="none", post="vres")
    y = rwkv_recurrence(r, k, v, a, lw, gate, row(k_k), row(k_a), row(r_k), row(ln_w), row(ln_b), batch, seq)
    return matmul_norm_residual(y, w_o.astype(BF16), g_post.reshape(1, d), h), v_first


def _ffn_layer(h, g_pre, g_post, w_gate, w_up, w_down):
    d = D_MODEL
    act = ffn_up(h, g_pre.reshape(1, d), w_gate.astype(BF16), w_up.astype(BF16))
    return matmul_norm_residual(act, w_down.astype(BF16), g_post.reshape(1, d), h)


def _rope_tables(positions):
    inv_freq = 1.0 / (ROPE_THETA ** (jnp.arange(0, MLA_ROPE, 2, dtype=F32) / MLA_ROPE))
    ang = positions.astype(F32).reshape(-1, 1) * inv_freq
    zeros = jnp.zeros((ang.shape[0], LANES - MLA_ROPE), F32)
    cos, sin = jnp.cos(ang), jnp.sin(ang)
    return jnp.concatenate([cos, cos, zeros], axis=1), jnp.concatenate([sin, sin, zeros], axis=1)


def kernel(x, positions, norm_mix_pre, norm_mix_post, norm_ffn_pre, norm_ffn_post, hyb_w_in, mla_q_norm, mla_w_uq, mla_kv_norm, mla_w_ukv, gdn_conv_w, gdn_a_log, gdn_dt_bias, gdn_out_norm, hyb_w_out, rwkv_mix, rwkv_w_r, rwkv_w_k, rwkv_w_v, rwkv_w_o, rwkv_w0, rwkv_w1, rwkv_w2, rwkv_a0, rwkv_a1, rwkv_a2, rwkv_g1, rwkv_g2, rwkv_k_k, rwkv_k_a, rwkv_r_k, rwkv_ln_w, rwkv_ln_b, rwkv_v0, rwkv_v1, rwkv_v2, ffn_w_gate, ffn_w_up, ffn_w_down):
    batch, seq, d = x.shape
    h = x.reshape(batch * seq, d)
    ct, st = _rope_tables(positions)
    v_first = None
    for layer in range(DEPTH):
        if layer % 2 == 0:
            e = layer // 2
            h = _hybrid_layer(h, ct, st, batch, seq, norm_mix_pre[layer], norm_mix_post[layer], hyb_w_in[e],
                              mla_q_norm[e], mla_w_uq[e], mla_kv_norm[e], mla_w_ukv[e], gdn_conv_w[e],
                              gdn_a_log[e], gdn_dt_bias[e], gdn_out_norm[e], hyb_w_out[e])
        else:
            o = layer // 2
            vres = None if o == 0 else (rwkv_v0[o - 1], rwkv_v1[o - 1], rwkv_v2[o - 1])
            h, v_first = _rwkv_layer(h, v_first, batch, seq, norm_mix_pre[layer], norm_mix_post[layer],
                                     rwkv_mix[o], rwkv_w_r[o], rwkv_w_k[o], rwkv_w_v[o], rwkv_w_o[o],
                                     rwkv_w0[o], rwkv_w1[o], rwkv_w2[o], rwkv_a0[o], rwkv_a1[o], rwkv_a2[o],
                                     rwkv_g1[o], rwkv_g2[o], rwkv_k_k[o], rwkv_k_a[o], rwkv_r_k[o],
                                     rwkv_ln_w[o], rwkv_ln_b[o], vres)
        h = _ffn_layer(h, norm_ffn_pre[layer], norm_ffn_post[layer], ffn_w_gate[layer], ffn_w_up[layer],
                       ffn_w_down[layer])
    return h.reshape(batch, seq, d)
```

```python
import functools

import jax
import jax.numpy as jnp
import numpy as np
from jax import lax
from jax.experimental import pallas as pl
from jax.experimental.pallas import tpu as pltpu

F32 = jnp.float32
BF16 = jnp.bfloat16
HI = lax.Precision.HIGHEST

D_MODEL = 2048
DEPTH = 4
LANES = 128
VMEM_LIMIT = 56 * 1024 * 1024

MLA_HEADS = 8
MLA_Q_LORA = 512
MLA_KV_LORA = 512
MLA_NOPE = 128
MLA_ROPE = 64
MLA_V = 128
MLA_QK = MLA_NOPE + MLA_ROPE
MLA_HEAD_PAD = 256
ROPE_THETA = 10000.0

GDN_HEADS = 8
GDN_DK = 128
GDN_DV = 128
GDN_CONV = 4
GDN_CHUNK = 128

RWKV_HEAD = 64
RWKV_HEADS = D_MODEL // RWKV_HEAD
RWKV_CHUNK = 64
RWKV_LN_EPS = 64e-5
RMS_EPS = 1e-6

HC_CQ, HC_CKV, HC_GQ, HC_GK, HC_GV, HC_GZ, HC_KR, HC_SM = 0, 512, 1024, 2048, 3072, 4096, 5120, 5248
HC_COLS = 5376


def _cparams(sem):
    return pltpu.CompilerParams(dimension_semantics=sem, vmem_limit_bytes=VMEM_LIMIT)


def _tile(n, pref):
    t = min(n, pref)
    while n % t:
        t -= 8
    return t


def _dot(a, b, prec=None):
    return jnp.dot(a, b, preferred_element_type=F32, precision=prec)


def _dot_nt(a, b, prec=None):
    return lax.dot_general(a, b, (((1,), (1,)), ((), ())), preferred_element_type=F32, precision=prec)


def _rms(x, g, eps=RMS_EPS):
    return x * lax.rsqrt(jnp.mean(x * x, axis=-1, keepdims=True) + eps) * g


def _sigmoid(x):
    return 1.0 / (1.0 + jnp.exp(-x))


def _silu(x):
    return x * _sigmoid(x)


def _softplus(x):
    return jnp.maximum(x, 0.0) + jnp.log(1.0 + jnp.exp(-jnp.abs(x)))


def _norm_mm_kernel(x_ref, g_ref, w_ref, o_ref, xn_ref):
    @pl.when(pl.program_id(1) == 0)
    def _():
        xn_ref[...] = _rms(x_ref[...], g_ref[...]).astype(BF16)

    o_ref[...] = _dot(xn_ref[...], w_ref[...]).astype(o_ref.dtype)


def norm_matmul(x, g, w, *, tn, out_dtype=F32, tm=512):
    m, k = x.shape
    n = w.shape[1]
    tm = _tile(m, tm)
    return pl.pallas_call(
        _norm_mm_kernel,
        grid=(m // tm, n // tn),
        in_specs=[pl.BlockSpec((tm, k), lambda i, j: (i, 0)),
                  pl.BlockSpec((1, k), lambda i, j: (0, 0)),
                  pl.BlockSpec((k, tn), lambda i, j: (0, j))],
        out_specs=pl.BlockSpec((tm, tn), lambda i, j: (i, j)),
        out_shape=jax.ShapeDtypeStruct((m, n), out_dtype),
        scratch_shapes=[pltpu.VMEM((tm, k), BF16)],
        compiler_params=_cparams(("parallel", "arbitrary")),
        name="norm_matmul",
    )(x, g, w)


def _ffn_up_kernel(x_ref, g_ref, wg_ref, wu_ref, o_ref, xn_ref):
    @pl.when(pl.program_id(1) == 0)
    def _():
        xn_ref[...] = _rms(x_ref[...], g_ref[...]).astype(BF16)

    xn = xn_ref[...]
    gate = _dot(xn, wg_ref[...])
    up = _dot(xn, wu_ref[...])
    o_ref[...] = (_silu(gate) * up).astype(o_ref.dtype)


def ffn_up(h, g, wg, wu, *, tm=512, tn=1408):
    m, k = h.shape
    n = wg.shape[1]
    tm = _tile(m, tm)
    tn = _tile(n, tn)
    return pl.pallas_call(
        _ffn_up_kernel,
        grid=(m // tm, n // tn),
        in_specs=[pl.BlockSpec((tm, k), lambda i, j: (i, 0)),
                  pl.BlockSpec((1, k), lambda i, j: (0, 0)),
                  pl.BlockSpec((k, tn), lambda i, j: (0, j)),
                  pl.BlockSpec((k, tn), lambda i, j: (0, j))],
        out_specs=pl.BlockSpec((tm, tn), lambda i, j: (i, j)),
        out_shape=jax.ShapeDtypeStruct((m, n), BF16),
        scratch_shapes=[pltpu.VMEM((tm, k), BF16)],
        compiler_params=_cparams(("parallel", "arbitrary")),
        name="ffn_up",
    )(h, g, wg, wu)


def _mm_norm_res_kernel(a_ref, w_ref, g_ref, h_ref, o_ref, f_ref, *, tn, nb):
    j = pl.program_id(1)
    f_ref[j] = _dot(a_ref[...], w_ref[...])

    @pl.when(j == nb - 1)
    def _():
        ssq = jnp.sum(f_ref[0] * f_ref[0], axis=-1, keepdims=True)
        for b in range(1, nb):
            ssq = ssq + jnp.sum(f_ref[b] * f_ref[b], axis=-1, keepdims=True)
        inv = lax.rsqrt(ssq * (1.0 / (nb * tn)) + RMS_EPS)
        for b in range(nb):
            cs = slice(b * tn, (b + 1) * tn)
            o_ref[:, cs] = h_ref[:, cs] + f_ref[b] * inv * g_ref[:, cs]


def matmul_norm_residual(a, w, g, h, *, tm=512, tn=512):
    m, k = a.shape
    n = w.shape[1]
    tm = _tile(m, tm)
    tn = _tile(n, tn)
    nb = n // tn
    return pl.pallas_call(
        functools.partial(_mm_norm_res_kernel, tn=tn, nb=nb),
        grid=(m // tm, nb),
        in_specs=[pl.BlockSpec((tm, k), lambda i, j: (i, 0)),
                  pl.BlockSpec((k, tn), lambda i, j: (0, j)),
                  pl.BlockSpec((1, n), lambda i, j: (0, 0)),
                  pl.BlockSpec((tm, n), lambda i, j: (i, 0))],
        out_specs=pl.BlockSpec((tm, n), lambda i, j: (i, 0)),
        out_shape=jax.ShapeDtypeStruct((m, n), F32),
        scratch_shapes=[pltpu.VMEM((nb, tm, tn), F32)],
        compiler_params=_cparams(("parallel", "arbitrary")),
        name="matmul_norm_residual",
    )(a, w, g, h)


def _mm_kernel(a_ref, w_ref, o_ref):
    o_ref[...] = _dot(a_ref[...], w_ref[...]).astype(o_ref.dtype)


def matmul(a, idx, w, *, tm=512, tn=1024, out_dtype=F32):
    _, m, k = a.shape
    n = w.shape[1]
    tm = _tile(m, tm)
    tn = _tile(n, tn)
    return pl.pallas_call(
        _mm_kernel,
        grid=(m // tm, n // tn),
        in_specs=[pl.BlockSpec((None, tm, k), lambda i, j: (idx, i, 0)),
                  pl.BlockSpec((k, tn), lambda i, j: (0, j))],
        out_specs=pl.BlockSpec((tm, tn), lambda i, j: (i, j)),
        out_shape=jax.ShapeDtypeStruct((m, n), out_dtype),
        compiler_params=_cparams(("parallel", "arbitrary")),
        name="matmul",
    )(a, w)


def _rope_group(grp, ct, st):
    return grp * ct + pltpu.roll(grp, 64, 1) * st


def _mla_proj_kernel(cq_ref, ckv_ref, kr_ref, qn_ref, kvn_ref, wq_ref, wkv_ref, ct_ref, st_ref,
                     q_ref, k_ref, v_ref):
    scale = MLA_QK ** -0.5 * np.log2(np.e)
    ct = ct_ref[...]
    st = st_ref[...]
    xq = _rms(cq_ref[...], qn_ref[...]).astype(BF16)
    xkv = _rms(ckv_ref[...], kvn_ref[...]).astype(BF16)
    kr = _rope_group(kr_ref[...], ct, st).astype(BF16)
    for h in range(MLA_HEADS):
        c0 = h * MLA_HEAD_PAD
        yq = _dot(xq, wq_ref[:, c0:c0 + MLA_HEAD_PAD])
        q_ref[:, c0:c0 + LANES] = (yq[:, :LANES] * scale).astype(BF16)
        q_ref[:, c0 + LANES:c0 + 2 * LANES] = (_rope_group(yq[:, LANES:], ct, st) * scale).astype(BF16)
        ykv = _dot(xkv, wkv_ref[:, c0:c0 + MLA_HEAD_PAD])
        k_ref[:, c0:c0 + LANES] = ykv[:, :LANES].astype(BF16)
        k_ref[:, c0 + LANES:c0 + 2 * LANES] = kr
        v_ref[:, h * MLA_V:(h + 1) * MLA_V] = ykv[:, LANES:].astype(BF16)


def mla_proj(hcat, q_norm, kv_norm, wq, wkv, ct, st, *, tm=512):
    m = hcat.shape[0]
    tm = _tile(m, tm)
    hp = MLA_HEADS * MLA_HEAD_PAD
    row = lambda i: (i, 0)
    const = lambda i: (0, 0)
    return pl.pallas_call(
        _mla_proj_kernel,
        grid=(m // tm,),
        in_specs=[pl.BlockSpec((tm, MLA_Q_LORA), lambda i: (i, HC_CQ // MLA_Q_LORA)),
                  pl.BlockSpec((tm, MLA_KV_LORA), lambda i: (i, HC_CKV // MLA_KV_LORA)),
                  pl.BlockSpec((tm, LANES), lambda i: (i, HC_KR // LANES)),
                  pl.BlockSpec((1, MLA_Q_LORA), const),
                  pl.BlockSpec((1, MLA_KV_LORA), const),
                  pl.BlockSpec((MLA_Q_LORA, hp), const),
                  pl.BlockSpec((MLA_KV_LORA, hp), const),
                  pl.BlockSpec((tm, LANES), row),
                  pl.BlockSpec((tm, LANES), row)],
        out_specs=[pl.BlockSpec((tm, hp), row),
                   pl.BlockSpec((tm, hp), row),
                   pl.BlockSpec((tm, MLA_HEADS * MLA_V), row)],
        out_shape=[jax.ShapeDtypeStruct((m, hp), BF16),
                   jax.ShapeDtypeStruct((m, hp), BF16),
                   jax.ShapeDtypeStruct((m, MLA_HEADS * MLA_V), BF16)],
        compiler_params=_cparams(("parallel",)),
        name="mla_proj",
    )(hcat, hcat, hcat, q_norm, kv_norm, wq, wkv, ct, st)


def _attn_kernel(q_ref, k_ref, v_ref, o_ref, *, tq, tk, hpg):
    qi = pl.program_id(2)
    hs = range(hpg)
    q = [q_ref[:, h * MLA_HEAD_PAD:(h + 1) * MLA_HEAD_PAD] for h in hs]

    def step(ki, carry, masked):
        start = pl.multiple_of(ki * tk, tk)
        s = [_dot_nt(q[h], k_ref[pl.ds(start, tk), h * MLA_HEAD_PAD:(h + 1) * MLA_HEAD_PAD]) for h in hs]
        if masked is not None:
            row = lax.broadcasted_iota(jnp.int32, (tq, tk), 0)
            col = lax.broadcasted_iota(jnp.int32, (tq, tk), 1) + masked * tk
            s = [jnp.where(col <= row, x, -1e30) for x in s]
        m_new = [jnp.maximum(carry[h][0], jnp.max(s[h], axis=-1, keepdims=True)) for h in hs]
        alpha = [jnp.exp2(carry[h][0] - m_new[h]) for h in hs]
        p = [jnp.exp2(s[h] - m_new[h]) for h in hs]
        l_new = [alpha[h] * carry[h][1] + jnp.sum(p[h], axis=-1, keepdims=True) for h in hs]
        pv = [_dot(p[h].astype(BF16), v_ref[pl.ds(start, tk), h * MLA_V:(h + 1) * MLA_V]) for h in hs]
        return tuple((m_new[h], l_new[h], alpha[h] * carry[h][2] + pv[h]) for h in hs)

    init = tuple((jnp.full((tq, 1), -1e30, F32), jnp.zeros((tq, 1), F32), jnp.zeros((tq, MLA_V), F32))
                 for _ in hs)
    per_q = tq // tk
    carry = lax.fori_loop(0, qi * per_q, lambda ki, c: step(ki, c, None), init)
    for j in range(per_q):
        carry = step(qi * per_q + j, carry, j)
    for h in hs:
        o_ref[:, h * MLA_V:(h + 1) * MLA_V] = (carry[h][2] / carry[h][1]).astype(o_ref.dtype)


def mla_attention(q, k, v, batch, seq, *, tq=512, tk=512, hpg=2):
    tq = _tile(seq, tq)
    tk = _tile(tq, tk)
    nq = seq // tq
    return pl.pallas_call(
        functools.partial(_attn_kernel, tq=tq, tk=tk, hpg=hpg),
        grid=(batch, MLA_HEADS // hpg, nq),
        in_specs=[pl.BlockSpec((tq, hpg * MLA_HEAD_PAD), lambda b, h, i: (b * nq + i, h)),
                  pl.BlockSpec((seq, hpg * MLA_HEAD_PAD), lambda b, h, i: (b, h)),
                  pl.BlockSpec((seq, hpg * MLA_V), lambda b, h, i: (b, h))],
        out_specs=pl.BlockSpec((tq, hpg * MLA_V), lambda b, h, i: (b * nq + i, h)),
        out_shape=jax.ShapeDtypeStruct((batch * seq, MLA_HEADS * MLA_V), BF16),
        compiler_params=_cparams(("parallel", "parallel", "arbitrary")),
        name="mla_attention",
    )(q, k, v)


def _split2(x):
    hi = x.astype(BF16)
    return hi, (x - hi.astype(F32)).astype(BF16)


def _dot_x3(a, b):
    ah, al = _split2(a)
    bh, bl = _split2(b)
    return _dot(jnp.concatenate([ah, al, ah], axis=1), jnp.concatenate([bh, bh, bl], axis=0))


def _dot_nt_x3(a, b):
    ah, al = _split2(a)
    bh, bl = _split2(b)
    return _dot_nt(jnp.concatenate([ah, al, ah], axis=1), jnp.concatenate([bh, bh, bl], axis=1))


def _dot_bf(a, b):
    return _dot(a.astype(BF16), b.astype(BF16))


def _unit_lower_inverse(a_list, levels, dot):
    n = a_list[0].shape[0]
    eye = (lax.broadcasted_iota(jnp.int32, (n, n), 0) == lax.broadcasted_iota(jnp.int32, (n, n), 1)).astype(F32)
    p = [-a for a in a_list]
    r = [eye + x for x in p]
    p = [dot(x, x) for x in p]
    for _ in range(levels - 2):
        s = [dot(jnp.concatenate([ri, pi], axis=0), pi) for ri, pi in zip(r, p)]
        r = [ri + si[:n] for ri, si in zip(r, s)]
        p = [si[n:] for si in s]
    return [ri + dot(ri, pi) for ri, pi in zip(r, p)]


def _cumsum_rows(tri, x):
    hi = x.astype(BF16)
    r1 = x - hi.astype(F32)
    mid = r1.astype(BF16)
    lo = (r1 - mid.astype(F32)).astype(BF16)
    return _dot(tri, hi) + _dot(tri, mid) + _dot(tri, lo)


def _col_of_row(row_vec, n):
    eye = lax.broadcasted_iota(jnp.int32, (n, n), 0) == lax.broadcasted_iota(jnp.int32, (n, n), 1)
    return jnp.sum(jnp.where(eye, jnp.broadcast_to(row_vec, (n, n)), 0.0), axis=1, keepdims=True)


def _gdn_kernel(gq_ref, gk_ref, gv_ref, gz_ref, sm_ref, cw_ref, alog_ref, dtb_ref, on_ref, o_ref,
                xq_scr, xk_scr, xv_scr, state_scr, *, chunk):
    c = chunk
    heads = range(GDN_HEADS)
    hs = [slice(h * LANES, (h + 1) * LANES) for h in heads]

    @pl.when(pl.program_id(1) == 0)
    def _():
        for scr in (xq_scr, xk_scr, xv_scr):
            scr[0:8, :] = jnp.zeros((8, scr.shape[1]), F32)
        state_scr[...] = jnp.zeros_like(state_scr)

    def conv_silu(x_ref, scr, widx):
        scr[8:8 + c, :] = x_ref[...]
        acc = cw_ref[widx, GDN_CONV - 1:GDN_CONV, :] * scr[8:8 + c, :]
        for j in range(GDN_CONV - 1):
            s = GDN_CONV - 1 - j
            acc = acc + cw_ref[widx, j:j + 1, :] * scr[8 - s:8 - s + c, :]
        scr[0:8, :] = scr[c:c + 8, :]
        return _silu(acc)

    def l2n(x):
        return x * lax.rsqrt(jnp.sum(x * x, axis=-1, keepdims=True) + 1e-6)

    def bf(x):
        return x.astype(BF16)

    qa = conv_silu(gq_ref, xq_scr, 0)
    ka = conv_silu(gk_ref, xk_scr, 1)
    va = conv_silu(gv_ref, xv_scr, 2)
    q = [l2n(qa[:, s]) * (GDN_DK ** -0.5) for s in hs]
    k = [l2n(ka[:, s]) for s in hs]
    v = [va[:, s] for s in hs]

    sm = sm_ref[...]
    lane = lax.broadcasted_iota(jnp.int32, (c, LANES), 1)
    beta = [_sigmoid(jnp.sum(jnp.where(lane == h, sm, 0.0), axis=-1, keepdims=True)) for h in heads]
    ga = [jnp.sum(jnp.where(lane == GDN_HEADS + h, sm, 0.0), axis=-1, keepdims=True) for h in heads]
    g = [-jnp.exp(alog_ref[:, hs[h]]) * _softplus(jnp.broadcast_to(ga[h], (c, LANES)) + dtb_ref[:, hs[h]])
         for h in heads]

    ri = lax.broadcasted_iota(jnp.int32, (c, c), 0)
    cj = lax.broadcasted_iota(jnp.int32, (c, c), 1)
    tri = (cj <= ri).astype(BF16)
    gam = [_cumsum_rows(tri, x) for x in g]
    decay = [jnp.exp(jnp.where(cj <= ri, x - x.T, -1e30)) for x in gam]
    eg = [jnp.exp(x) for x in gam]
    gl = [x[c - 1:c, :] for x in gam]
    kb = [k[h] * beta[h] for h in heads]
    vb = [v[h] * beta[h] for h in heads]

    a_mat = [jnp.where(cj < ri, _dot_nt_x3(kb[h], k[h]) * decay[h], 0.0) for h in heads]
    tinv = _unit_lower_inverse(a_mat, 7, _dot_x3)
    rhs = [jnp.concatenate([kb[h] * eg[h], vb[h]], axis=1) for h in heads]
    wu = [bf(_dot_x3(tinv[h], rhs[h])) for h in heads]
    attn = [_dot_nt(bf(q[h]), bf(k[h])) * decay[h] for h in heads]
    k_dec = [k[h] * jnp.exp(gl[h] - gam[h]) for h in heads]
    lhs = [bf(jnp.concatenate([attn[h], k_dec[h].T], axis=0)) for h in heads]
    z = [_dot(lhs[h], wu[h]) for h in heads]
    state = [state_scr[h] for h in heads]
    l2 = [bf(jnp.concatenate([q[h] * eg[h] - z[h][:c, :LANES], -z[h][c:, :LANES]], axis=0)) for h in heads]
    rr = [_dot(l2[h], bf(state[h])) for h in heads]
    for h in heads:
        state_scr[h] = state[h] * jnp.exp(gl[h]) + rr[h][c:] + z[h][c:, LANES:]
        o = rr[h][:c] + z[h][:c, LANES:]
        o_ref[:, hs[h]] = (_rms(o, on_ref[...]) * _silu(gz_ref[:, hs[h]])).astype(o_ref.dtype)


def gdn_mixer(hcat, conv_w, a_log, dt_bias, out_norm, batch, seq):
    c = GDN_CHUNK
    assert c == LANES and seq % c == 0
    nc = seq // c
    width = GDN_HEADS * LANES
    blk = lambda off: pl.BlockSpec((c, width), lambda b, i, off=off: (b * nc + i, off // width))
    const2 = lambda b, i: (0, 0)
    cw = conv_w.reshape(GDN_CONV, 3, width).transpose(1, 0, 2)
    alog = jnp.repeat(a_log.astype(F32), LANES).reshape(1, width)
    dtb = jnp.repeat(dt_bias.astype(F32), LANES).reshape(1, width)
    return pl.pallas_call(
        functools.partial(_gdn_kernel, chunk=c),
        grid=(batch, nc),
        in_specs=[blk(HC_GQ), blk(HC_GK), blk(HC_GV), blk(HC_GZ),
                  pl.BlockSpec((c, LANES), lambda b, i: (b * nc + i, HC_SM // LANES)),
                  pl.BlockSpec((3, GDN_CONV, width), lambda b, i: (0, 0, 0)),
                  pl.BlockSpec((1, width), const2),
                  pl.BlockSpec((1, width), const2),
                  pl.BlockSpec((1, GDN_DV), const2)],
        out_specs=pl.BlockSpec((c, width), lambda b, i: (b * nc + i, 0)),
        out_shape=jax.ShapeDtypeStruct((batch * seq, width), BF16),
        scratch_shapes=[pltpu.VMEM((c + 8, width), F32)] * 3 + [pltpu.VMEM((GDN_HEADS, GDN_DK, GDN_DV), F32)],
        compiler_params=_cparams(("parallel", "arbitrary")),
        name="gdn_mixer",
    )(hcat, hcat, hcat, hcat, hcat, cw, alog, dtb, out_norm)


def _rwkv_mix_kernel(h_ref, hp_ref, g_ref, mix_ref, o_ref, *, tm, seq):
    i = pl.program_id(0)
    g = g_ref[...]
    xn = _rms(h_ref[...], g)
    prev_last = _rms(hp_ref[...], g)[7:8, :]
    prev_last = jnp.where((i * tm) % seq == 0, 0.0, prev_last)
    rows = lax.broadcasted_iota(jnp.int32, xn.shape, 0)
    xprev = jnp.where(rows == 0, prev_last, pltpu.roll(xn, 1, 0))
    xx = xprev - xn
    for j in range(6):
        o_ref[j] = (xn + xx * mix_ref[j:j + 1, :]).astype(o_ref.dtype)


def rwkv_mix(h, g, mix, seq, *, tm=256):
    m, d = h.shape
    tm = _tile(seq, tm)
    return pl.pallas_call(
        functools.partial(_rwkv_mix_kernel, tm=tm, seq=seq),
        grid=(m // tm,),
        in_specs=[pl.BlockSpec((tm, d), lambda i: (i, 0)),
                  pl.BlockSpec((8, d), lambda i: (jnp.maximum(i * (tm // 8) - 1, 0), 0)),
                  pl.BlockSpec((1, d), lambda i: (0, 0)),
                  pl.BlockSpec((6, d), lambda i: (0, 0))],
        out_specs=pl.BlockSpec((6, tm, d), lambda i: (0, i, 0)),
        out_shape=jax.ShapeDtypeStruct((6, m, d), BF16),
        compiler_params=_cparams(("parallel",)),
        name="rwkv_mix",
    )(h, h, g, mix)


def _lora_kernel(x_ref, a_ref, b_ref, bias_ref, *rest, mid, post):
    o_ref = rest[-1]
    z = _dot(x_ref[...], a_ref[...])
    if mid == "tanh":
        z = jnp.tanh(z)
    elif mid == "sigmoid":
        z = _sigmoid(z)
    y = _dot(z.astype(BF16), b_ref[...]) + bias_ref[...]
    if post == "logdecay":
        y = -jnp.exp(-_softplus(-y) - 0.5)
    elif post == "sigmoid":
        y = _sigmoid(y)
    elif post == "vres":
        v = rest[0][...]
        y = v + (rest[1][...] - v) * _sigmoid(y)
    o_ref[...] = y.astype(o_ref.dtype)


def lora(x, idx, a, b, bias, *extras, mid, post, tm=512):
    _, m, d = x.shape
    r = a.shape[1]
    n = b.shape[1]
    tm = _tile(m, tm)
    row = lambda i: (i, 0)
    const = lambda i: (0, 0)
    return pl.pallas_call(
        functools.partial(_lora_kernel, mid=mid, post=post),
        grid=(m // tm,),
        in_specs=[pl.BlockSpec((None, tm, d), lambda i: (idx, i, 0)),
                  pl.BlockSpec((d, r), const), pl.BlockSpec((r, n), const),
                  pl.BlockSpec((1, n), const)] + [pl.BlockSpec((tm, n), row) for _ in extras],
        out_specs=pl.BlockSpec((tm, n), row),
        out_shape=jax.ShapeDtypeStruct((m, n), F32),
        compiler_params=_cparams(("parallel",)),
        name="lora_" + post,
    )(x, a, b, bias, *extras)


def _rwkv_kernel(r_ref, k_ref, v_ref, a_ref, lw_ref, gt_ref, kk_ref, ka_ref, rk_ref, lnw_ref, lnb_ref,
                 o_ref, state_scr, *, chunk, pairs):
    c = chunk
    ps = range(pairs)
    sls = [slice(p * LANES, (p + 1) * LANES) for p in ps]

    @pl.when(pl.program_id(2) == 0)
    def _():
        state_scr[...] = jnp.zeros_like(state_scr)

    lane = lax.broadcasted_iota(jnp.int32, (c, LANES), 1)
    m0 = lane < RWKV_HEAD
    ri = lax.broadcasted_iota(jnp.int32, (c, c), 0)
    cj = lax.broadcasted_iota(jnp.int32, (c, c), 1)
    tri = (cj <= ri).astype(BF16)
    r2 = lax.broadcasted_iota(jnp.int32, (2 * c, 2 * c), 0)
    c2 = lax.broadcasted_iota(jnp.int32, (2 * c, 2 * c), 1)
    strict2 = c2 < r2
    rw = lax.broadcasted_iota(jnp.int32, (c, 2 * c), 0)
    cw = lax.broadcasted_iota(jnp.int32, (c, 2 * c), 1)
    incl_w = jnp.where(cw >= c, cw - c, cw) <= rw
    zero_blk = jnp.zeros((2 * c, LANES), BF16)

    def seg_sum(x):
        s0 = jnp.sum(jnp.where(m0, x, 0.0), axis=-1, keepdims=True)
        s1 = jnp.sum(jnp.where(m0, 0.0, x), axis=-1, keepdims=True)
        return jnp.where(m0, s0, s1)

    def stack(x):
        return jnp.concatenate([jnp.where(m0, x, 0.0), jnp.where(m0, 0.0, x)], axis=0)

    def bf(x):
        return x.astype(BF16)

    r = [r_ref[:, s] for s in sls]
    k = [k_ref[:, s] for s in sls]
    v = [v_ref[:, s] for s in sls]
    a = [a_ref[:, s] for s in sls]
    lw = [lw_ref[:, s] for s in sls]
    kk = [k[p] * kk_ref[:, sls[p]] for p in ps]
    kk = [x * lax.rsqrt(seg_sum(x * x) + 1e-12) for x in kk]
    kh = [k[p] * (1.0 + (a[p] - 1.0) * ka_ref[:, sls[p]]) for p in ps]
    bv = [kk[p] * a[p] for p in ps]

    cum = [_cumsum_rows(tri, x) for x in lw]
    cl = [x[c - 1:c, :] for x in cum]
    e_neg = [jnp.exp(-x) for x in cum]
    e_end = [jnp.exp(cl[p] - cum[p]) for p in ps]
    rg = [r[p] * jnp.exp(cum[p]) for p in ps]
    rgc = [bf(x) for x in rg]
    xs = [bf(stack(kk[p] * jnp.exp(cum[p] - lw[p]))) for p in ps]
    bs = [bf(stack(bv[p] * e_neg[p])) for p in ps]
    ks = [bf(stack(kh[p] * e_neg[p])) for p in ps]
    vs = [bf(stack(v[p])) for p in ps]
    kht = [stack(kh[p] * e_end[p]).T for p in ps]
    bht = [stack(bv[p] * e_end[p]).T for p in ps]

    ab = [_dot_nt(xs[p], jnp.concatenate([bs[p], ks[p]], axis=0)) for p in ps]
    akk = [jnp.where(strict2, x[:, :2 * c], 0.0) for x in ab]
    akv = [bf(jnp.where(strict2, x[:, 2 * c:], 0.0)) for x in ab]
    tinv = [bf(x) for x in _unit_lower_inverse(akk, 6, _dot_bf)]
    w1 = [bf(_dot(akv[p], vs[p])) for p in ps]
    m12 = [bf(_dot(tinv[p], jnp.concatenate([xs[p], w1[p]], axis=1))) for p in ps]
    bb = [_dot_nt(rgc[p], jnp.concatenate([ks[p], bs[p]], axis=0)) for p in ps]
    lhs = [bf(jnp.concatenate([
        jnp.concatenate([jnp.where(incl_w, x[:, :2 * c], 0.0), -jnp.where(incl_w, x[:, 2 * c:], 0.0)], axis=1),
        jnp.concatenate([kht[p], -bht[p]], axis=1)], axis=0)) for p, x in enumerate(bb)]
    rhs = [jnp.concatenate([jnp.concatenate([vs[p], zero_blk], axis=1),
                            jnp.concatenate([m12[p][:, LANES:], m12[p][:, :LANES]], axis=1)], axis=0)
           for p in ps]
    z = [_dot(lhs[p], rhs[p]) for p in ps]
    state = [state_scr[p] for p in ps]
    l2 = [bf(jnp.concatenate([rg[p] + z[p][:c, LANES:], z[p][c:, LANES:]], axis=0)) for p in ps]
    rr = [_dot(l2[p], bf(state[p])) for p in ps]
    for p in ps:
        sl = sls[p]
        state_scr[p] = state[p] * _col_of_row(jnp.exp(cl[p]), LANES) + rr[p][c:] + z[p][c:, :LANES]
        y = rr[p][:c] + z[p][:c, :LANES]
        mu = seg_sum(y) * (1.0 / RWKV_HEAD)
        yc = y - mu
        var = seg_sum(yc * yc) * (1.0 / RWKV_HEAD)
        yn = yc * lax.rsqrt(var + RWKV_LN_EPS) * lnw_ref[:, sl] + lnb_ref[:, sl]
        bonus = seg_sum(r[p] * kh[p] * rk_ref[:, sl]) * v[p]
        o_ref[:, sl] = ((yn + bonus) * gt_ref[:, sl]).astype(o_ref.dtype)


def rwkv_recurrence(r, k, v, a, lw, gate, k_k, k_a, r_k, ln_w, ln_b, batch, seq, *, pairs=8):
    c = RWKV_CHUNK
    m, d = r.shape
    nc = seq // c
    width = pairs * LANES
    blk = pl.BlockSpec((c, width), lambda b, hp, i: (b * nc + i, hp))
    par = pl.BlockSpec((1, width), lambda b, hp, i: (0, hp))
    return pl.pallas_call(
        functools.partial(_rwkv_kernel, chunk=c, pairs=pairs),
        grid=(batch, d // width, nc),
        in_specs=[blk] * 6 + [par] * 5,
        out_specs=blk,
        out_shape=jax.ShapeDtypeStruct((m, d), BF16),
        scratch_shapes=[pltpu.VMEM((pairs, LANES, LANES), F32)],
        compiler_params=_cparams(("parallel", "parallel", "arbitrary")),
        name="rwkv_recurrence",
    )(r, k, v, a, lw, gate, k_k, k_a, r_k, ln_w, ln_b)


def _pad_cols(w, n):
    return jnp.pad(w, ((0, 0), (0, n - w.shape[1])))


def _rot_cols(w):
    half = MLA_ROPE // 2
    return jnp.concatenate([-w[..., half:], w[..., :half]], axis=-1)


def _hybrid_layer(h, ct, st, batch, seq, g_pre, g_post, w_in, q_norm, w_uq, kv_norm, w_ukv, conv_w, a_log,
                  dt_bias, out_norm, w_out):
    d = D_MODEL
    o_cq, o_ckv, o_kr = 0, MLA_Q_LORA, MLA_Q_LORA + MLA_KV_LORA
    o_g = o_kr + MLA_ROPE
    o_sm = o_g + 4 * GDN_HEADS * GDN_DK
    w_kr = w_in[:, o_kr:o_g]
    w_cat = jnp.concatenate([w_in[:, :o_kr], w_in[:, o_g:o_sm], w_kr, _rot_cols(w_kr),
                             _pad_cols(w_in[:, o_sm:], LANES)], axis=1).astype(BF16)
    hcat = norm_matmul(h, g_pre.reshape(1, d), w_cat, tn=768)

    wq = w_uq.reshape(MLA_Q_LORA, MLA_HEADS, MLA_QK)
    wq_rope = wq[..., MLA_NOPE:]
    wq = jnp.concatenate([wq[..., :MLA_NOPE], wq_rope, _rot_cols(wq_rope)], axis=-1)
    wq = wq.reshape(MLA_Q_LORA, MLA_HEADS * MLA_HEAD_PAD).astype(BF16)
    q, k, v = mla_proj(hcat, q_norm.reshape(1, -1), kv_norm.reshape(1, -1), wq, w_ukv.astype(BF16), ct, st)
    mla_out = mla_attention(q, k, v, batch, seq)
    gdn_out = gdn_mixer(hcat, conv_w, a_log, dt_bias, out_norm.reshape(1, -1), batch, seq)
    mix = jnp.concatenate([mla_out, gdn_out], axis=1)
    return matmul_norm_residual(mix, w_out.astype(BF16), g_post.reshape(1, d), h, tn=d)


def _rwkv_layer(h, v_first, batch, seq, g_pre, g_post, mix, w_r, w_k, w_v, w_o, w0, w1, w2, a0, a1, a2,
                g1, g2, k_k, k_a, r_k, ln_w, ln_b, vres):
    d = D_MODEL
    row = lambda t: t.reshape(1, d).astype(F32)
    padc = lambda w: _pad_cols(w, -(-w.shape[1] // LANES) * LANES).astype(BF16)
    padr = lambda w: jnp.pad(w, ((0, -(-w.shape[0] // LANES) * LANES - w.shape[0]), (0, 0))).astype(BF16)
    xs = rwkv_mix(h, g_pre.reshape(1, d), mix, seq)
    r = matmul(xs, 0, w_r.astype(BF16))
    k = matmul(xs, 2, w_k.astype(BF16))
    v = matmul(xs, 3, w_v.astype(BF16))
    lw = lora(xs, 1, padc(w1), padr(w2), row(w0), mid="tanh", post="logdecay")
    a = lora(xs, 4, padc(a1), padr(a2), row(a0), mid="none", post="sigmoid")
    gate = lora(xs, 5, padc(g1), padr(g2), jnp.zeros((1, d), F32), mid="sigmoid", post="none")
    if vres is None:
        v_first = v
    else:
        v0, v1, v2 = vres
        v = lora(xs, 3, padc(v1), padr(v2), row(v0), v, v_first, mid="none", post="vres")
    y = rwkv_recurrence(r, k, v, a, lw, gate, row(k_k), row(k_a), row(r_k), row(ln_w), row(ln_b), batch, seq)
    return matmul_norm_residual(y, w_o.astype(BF16), g_post.reshape(1, d), h, tn=d), v_first


def _ffn_layer(h, g_pre, g_post, w_gate, w_up, w_down):
    d = D_MODEL
    act = ffn_up(h, g_pre.reshape(1, d), w_gate.astype(BF16), w_up.astype(BF16))
    return matmul_norm_residual(act, w_down.astype(BF16), g_post.reshape(1, d), h)


def _rope_tables(positions):
    inv_freq = 1.0 / (ROPE_THETA ** (jnp.arange(0, MLA_ROPE, 2, dtype=F32) / MLA_ROPE))
    ang = positions.astype(F32).reshape(-1, 1) * inv_freq
    zeros = jnp.zeros((ang.shape[0], LANES - MLA_ROPE), F32)
    cos, sin = jnp.cos(ang), jnp.sin(ang)
    return jnp.concatenate([cos, cos, zeros], axis=1), jnp.concatenate([sin, sin, zeros], axis=1)


def kernel(x, positions, norm_mix_pre, norm_mix_post, norm_ffn_pre, norm_ffn_post, hyb_w_in, mla_q_norm, mla_w_uq, mla_kv_norm, mla_w_ukv, gdn_conv_w, gdn_a_log, gdn_dt_bias, gdn_out_norm, hyb_w_out, rwkv_mix, rwkv_w_r, rwkv_w_k, rwkv_w_v, rwkv_w_o, rwkv_w0, rwkv_w1, rwkv_w2, rwkv_a0, rwkv_a1, rwkv_a2, rwkv_g1, rwkv_g2, rwkv_k_k, rwkv_k_a, rwkv_r_k, rwkv_ln_w, rwkv_ln_b, rwkv_v0, rwkv_v1, rwkv_v2, ffn_w_gate, ffn_w_up, ffn_w_down):
    batch, seq, d = x.shape
    h = x.reshape(batch * seq, d)
    ct, st = _rope_tables(positions)
    v_first = None
    for layer in range(DEPTH):
        if layer % 2 == 0:
            e = layer // 2
            h = _hybrid_layer(h, ct, st, batch, seq, norm_mix_pre[layer], norm_mix_post[layer], hyb_w_in[e],
                              mla_q_norm[e], mla_w_uq[e], mla_kv_norm[e], mla_w_ukv[e], gdn_conv_w[e],
                              gdn_a_log[e], gdn_dt_bias[e], gdn_out_norm[e], hyb_w_out[e])
        else:
            o = layer // 2
            vres = None if o == 0 else (rwkv_v0[o - 1], rwkv_v1[o - 1], rwkv_v2[o - 1])
            h, v_first = _rwkv_layer(h, v_first, batch, seq, norm_mix_pre[layer], norm_mix_post[layer],
                                     rwkv_mix[o], rwkv_w_r[o], rwkv_w_k[o], rwkv_w_v[o], rwkv_w_o[o],
                                     rwkv_w0[o], rwkv_w1[o], rwkv_w2[o], rwkv_a0[o], rwkv_a1[o], rwkv_a2[o],
                                     rwkv_g1[o], rwkv_g2[o], rwkv_k_k[o], rwkv_k_a[o], rwkv_r_k[o],
                                     rwkv_ln_w[o], rwkv_ln_b[o], vres)
        h = _ffn_layer(h, norm_ffn_pre[layer], norm_ffn_post[layer], ffn_w_gate[layer], ffn_w_up[layer],
                       ffn_w_down[layer])
    return h.reshape(batch, seq, d)
```

```python
import functools

import jax
import jax.numpy as jnp
import numpy as np
from jax import lax
from jax.experimental import pallas as pl
from jax.experimental.pallas import tpu as pltpu

F32 = jnp.float32
BF16 = jnp.bfloat16
HI = lax.Precision.HIGHEST

D_MODEL = 2048
DEPTH = 4
LANES = 128
VMEM_LIMIT = 56 * 1024 * 1024

MLA_HEADS = 8
MLA_Q_LORA = 512
MLA_KV_LORA = 512
MLA_NOPE = 128
MLA_ROPE = 64
MLA_V = 128
MLA_QK = MLA_NOPE + MLA_ROPE
MLA_HEAD_PAD = 256
ROPE_THETA = 10000.0

GDN_HEADS = 8
GDN_DK = 128
GDN_DV = 128
GDN_CONV = 4
GDN_CHUNK = 128

RWKV_HEAD = 64
RWKV_HEADS = D_MODEL // RWKV_HEAD
RWKV_CHUNK = 64
RWKV_LN_EPS = 64e-5
RMS_EPS = 1e-6

HC_CQ, HC_CKV, HC_GQ, HC_GK, HC_GV, HC_GZ, HC_KR, HC_SM = 0, 512, 1024, 2048, 3072, 4096, 5120, 5248
HC_COLS = 5376


def _cparams(sem):
    return pltpu.CompilerParams(dimension_semantics=sem, vmem_limit_bytes=VMEM_LIMIT)


def _tile(n, pref):
    t = min(n, pref)
    while n % t:
        t -= 8
    return t


def _dot(a, b, prec=None):
    return jnp.dot(a, b, preferred_element_type=F32, precision=prec)


def _dot_nt(a, b, prec=None):
    return lax.dot_general(a, b, (((1,), (1,)), ((), ())), preferred_element_type=F32, precision=prec)


def _rms(x, g, eps=RMS_EPS):
    return x * lax.rsqrt(jnp.mean(x * x, axis=-1, keepdims=True) + eps) * g


def _sigmoid(x):
    return 1.0 / (1.0 + jnp.exp(-x))


def _silu(x):
    return x * _sigmoid(x)


def _softplus(x):
    return jnp.maximum(x, 0.0) + jnp.log(1.0 + jnp.exp(-jnp.abs(x)))


def _norm_mm_kernel(x_ref, g_ref, w_ref, o_ref, xn_ref):
    @pl.when(pl.program_id(1) == 0)
    def _():
        xn_ref[...] = _rms(x_ref[...], g_ref[...]).astype(BF16)

    o_ref[...] = _dot(xn_ref[...], w_ref[...]).astype(o_ref.dtype)


def norm_matmul(x, g, w, *, tn, out_dtype=F32, tm=512):
    m, k = x.shape
    n = w.shape[1]
    tm = _tile(m, tm)
    return pl.pallas_call(
        _norm_mm_kernel,
        grid=(m // tm, n // tn),
        in_specs=[pl.BlockSpec((tm, k), lambda i, j: (i, 0)),
                  pl.BlockSpec((1, k), lambda i, j: (0, 0)),
                  pl.BlockSpec((k, tn), lambda i, j: (0, j))],
        out_specs=pl.BlockSpec((tm, tn), lambda i, j: (i, j)),
        out_shape=jax.ShapeDtypeStruct((m, n), out_dtype),
        scratch_shapes=[pltpu.VMEM((tm, k), BF16)],
        compiler_params=_cparams(("parallel", "arbitrary")),
        name="norm_matmul",
    )(x, g, w)


def _ffn_up_kernel(x_ref, g_ref, wg_ref, wu_ref, o_ref, xn_ref):
    @pl.when(pl.program_id(1) == 0)
    def _():
        xn_ref[...] = _rms(x_ref[...], g_ref[...]).astype(BF16)

    xn = xn_ref[...]
    gate = _dot(xn, wg_ref[...])
    up = _dot(xn, wu_ref[...])
    o_ref[...] = (_silu(gate) * up).astype(o_ref.dtype)


def ffn_up(h, g, wg, wu, *, tm=512, tn=1408):
    m, k = h.shape
    n = wg.shape[1]
    tm = _tile(m, tm)
    tn = _tile(n, tn)
    return pl.pallas_call(
        _ffn_up_kernel,
        grid=(m // tm, n // tn),
        in_specs=[pl.BlockSpec((tm, k), lambda i, j: (i, 0)),
                  pl.BlockSpec((1, k), lambda i, j: (0, 0)),
                  pl.BlockSpec((k, tn), lambda i, j: (0, j)),
                  pl.BlockSpec((k, tn), lambda i, j: (0, j))],
        out_specs=pl.BlockSpec((tm, tn), lambda i, j: (i, j)),
        out_shape=jax.ShapeDtypeStruct((m, n), BF16),
        scratch_shapes=[pltpu.VMEM((tm, k), BF16)],
        compiler_params=_cparams(("parallel", "arbitrary")),
        name="ffn_up",
    )(h, g, wg, wu)


def _mm_norm_res_kernel(*refs, tn, nb, n_a):
    a_refs = refs[:n_a]
    w_ref, g_ref, h_ref, o_ref, f_ref = refs[n_a:]
    j = pl.program_id(1)
    k0 = 0
    f = None
    for a_ref in a_refs:
        part = _dot(a_ref[...], w_ref[k0:k0 + a_ref.shape[1], :])
        f = part if f is None else f + part
        k0 += a_ref.shape[1]
    f_ref[j] = f

    @pl.when(j == nb - 1)
    def _():
        ssq = jnp.sum(f_ref[0] * f_ref[0], axis=-1, keepdims=True)
        for b in range(1, nb):
            ssq = ssq + jnp.sum(f_ref[b] * f_ref[b], axis=-1, keepdims=True)
        inv = lax.rsqrt(ssq * (1.0 / (nb * tn)) + RMS_EPS)
        for b in range(nb):
            cs = slice(b * tn, (b + 1) * tn)
            o_ref[:, cs] = h_ref[:, cs] + f_ref[b] * inv * g_ref[:, cs]


def matmul_norm_residual(a_parts, w, g, h, *, tm=512, tn=512):
    m = a_parts[0].shape[0]
    k, n = w.shape
    tm = _tile(m, tm)
    tn = _tile(n, tn)
    nb = n // tn
    return pl.pallas_call(
        functools.partial(_mm_norm_res_kernel, tn=tn, nb=nb, n_a=len(a_parts)),
        grid=(m // tm, nb),
        in_specs=[pl.BlockSpec((tm, a.shape[1]), lambda i, j: (i, 0)) for a in a_parts] + [
                  pl.BlockSpec((k, tn), lambda i, j: (0, j)),
                  pl.BlockSpec((1, n), lambda i, j: (0, 0)),
                  pl.BlockSpec((tm, n), lambda i, j: (i, 0))],
        out_specs=pl.BlockSpec((tm, n), lambda i, j: (i, 0)),
        out_shape=jax.ShapeDtypeStruct((m, n), F32),
        scratch_shapes=[pltpu.VMEM((nb, tm, tn), F32)],
        compiler_params=_cparams(("parallel", "arbitrary")),
        name="matmul_norm_residual",
    )(*a_parts, w, g, h)


def _mm_kernel(a_ref, w_ref, o_ref):
    o_ref[...] = _dot(a_ref[...], w_ref[...]).astype(o_ref.dtype)


def matmul(a, idx, w, *, tm=1024, tn=1024, out_dtype=F32):
    _, m, k = a.shape
    n = w.shape[1]
    tm = _tile(m, tm)
    tn = _tile(n, tn)
    return pl.pallas_call(
        _mm_kernel,
        grid=(m // tm, n // tn),
        in_specs=[pl.BlockSpec((None, tm, k), lambda i, j: (idx, i, 0)),
                  pl.BlockSpec((k, tn), lambda i, j: (0, j))],
        out_specs=pl.BlockSpec((tm, tn), lambda i, j: (i, j)),
        out_shape=jax.ShapeDtypeStruct((m, n), out_dtype),
        compiler_params=_cparams(("parallel", "arbitrary")),
        name="matmul",
    )(a, w)


def _rope_group(grp, ct, st):
    return grp * ct + pltpu.roll(grp, 64, 1) * st


def _mla_proj_kernel(cq_ref, ckv_ref, kr_ref, qn_ref, kvn_ref, wq_ref, wkv_ref, ct_ref, st_ref,
                     q_ref, k_ref, v_ref):
    scale = MLA_QK ** -0.5 * np.log2(np.e)
    ct = ct_ref[...]
    st = st_ref[...]
    xq = _rms(cq_ref[...].astype(F32), qn_ref[...]).astype(BF16)
    xkv = _rms(ckv_ref[...].astype(F32), kvn_ref[...]).astype(BF16)
    kr = _rope_group(kr_ref[...].astype(F32), ct, st).astype(BF16)
    for h in range(MLA_HEADS):
        c0 = h * MLA_HEAD_PAD
        yq = _dot(xq, wq_ref[:, c0:c0 + MLA_HEAD_PAD])
        q_ref[:, c0:c0 + LANES] = (yq[:, :LANES] * scale).astype(BF16)
        q_ref[:, c0 + LANES:c0 + 2 * LANES] = (_rope_group(yq[:, LANES:], ct, st) * scale).astype(BF16)
        ykv = _dot(xkv, wkv_ref[:, c0:c0 + MLA_HEAD_PAD])
        k_ref[:, c0:c0 + LANES] = ykv[:, :LANES].astype(BF16)
        k_ref[:, c0 + LANES:c0 + 2 * LANES] = kr
        v_ref[:, h * MLA_V:(h + 1) * MLA_V] = ykv[:, LANES:].astype(BF16)


def mla_proj(hcat, q_norm, kv_norm, wq, wkv, ct, st, *, tm=512):
    m = hcat.shape[0]
    tm = _tile(m, tm)
    hp = MLA_HEADS * MLA_HEAD_PAD
    row = lambda i: (i, 0)
    const = lambda i: (0, 0)
    return pl.pallas_call(
        _mla_proj_kernel,
        grid=(m // tm,),
        in_specs=[pl.BlockSpec((tm, MLA_Q_LORA), lambda i: (i, HC_CQ // MLA_Q_LORA)),
                  pl.BlockSpec((tm, MLA_KV_LORA), lambda i: (i, HC_CKV // MLA_KV_LORA)),
                  pl.BlockSpec((tm, LANES), lambda i: (i, HC_KR // LANES)),
                  pl.BlockSpec((1, MLA_Q_LORA), const),
                  pl.BlockSpec((1, MLA_KV_LORA), const),
                  pl.BlockSpec((MLA_Q_LORA, hp), const),
                  pl.BlockSpec((MLA_KV_LORA, hp), const),
                  pl.BlockSpec((tm, LANES), row),
                  pl.BlockSpec((tm, LANES), row)],
        out_specs=[pl.BlockSpec((tm, hp), row),
                   pl.BlockSpec((tm, hp), row),
                   pl.BlockSpec((tm, MLA_HEADS * MLA_V), row)],
        out_shape=[jax.ShapeDtypeStruct((m, hp), BF16),
                   jax.ShapeDtypeStruct((m, hp), BF16),
                   jax.ShapeDtypeStruct((m, MLA_HEADS * MLA_V), BF16)],
        compiler_params=_cparams(("parallel",)),
        name="mla_proj",
    )(hcat, hcat, hcat, q_norm, kv_norm, wq, wkv, ct, st)


def _attn_kernel(q_ref, k_ref, v_ref, o_ref, *, tq, tk, hpg):
    qi = pl.program_id(2)
    hs = range(hpg)
    q = [q_ref[:, h * MLA_HEAD_PAD:(h + 1) * MLA_HEAD_PAD] for h in hs]

    def step(ki, carry, masked):
        start = pl.multiple_of(ki * tk, tk)
        s = [_dot_nt(q[h], k_ref[pl.ds(start, tk), h * MLA_HEAD_PAD:(h + 1) * MLA_HEAD_PAD]) for h in hs]
        if masked is not None:
            row = lax.broadcasted_iota(jnp.int32, (tq, tk), 0)
            col = lax.broadcasted_iota(jnp.int32, (tq, tk), 1) + masked * tk
            s = [jnp.where(col <= row, x, -1e30) for x in s]
        m_new = [jnp.maximum(carry[h][0], jnp.max(s[h], axis=-1, keepdims=True)) for h in hs]
        alpha = [jnp.exp2(carry[h][0] - m_new[h]) for h in hs]
        p = [jnp.exp2(s[h] - m_new[h]) for h in hs]
        l_new = [alpha[h] * carry[h][1] + jnp.sum(p[h], axis=-1, keepdims=True) for h in hs]
        pv = [_dot(p[h].astype(BF16), v_ref[pl.ds(start, tk), h * MLA_V:(h + 1) * MLA_V]) for h in hs]
        return tuple((m_new[h], l_new[h], alpha[h] * carry[h][2] + pv[h]) for h in hs)

    init = tuple((jnp.full((tq, 1), -1e30, F32), jnp.zeros((tq, 1), F32), jnp.zeros((tq, MLA_V), F32))
                 for _ in hs)
    per_q = tq // tk
    carry = lax.fori_loop(0, qi * per_q, lambda ki, c: step(ki, c, None), init)
    for j in range(per_q):
        carry = step(qi * per_q + j, carry, j)
    for h in hs:
        o_ref[:, h * MLA_V:(h + 1) * MLA_V] = (carry[h][2] / carry[h][1]).astype(o_ref.dtype)


def mla_attention(q, k, v, batch, seq, *, tq=512, tk=512, hpg=2):
    tq = _tile(seq, tq)
    tk = _tile(tq, tk)
    nq = seq // tq
    return pl.pallas_call(
        functools.partial(_attn_kernel, tq=tq, tk=tk, hpg=hpg),
        grid=(batch, MLA_HEADS // hpg, nq),
        in_specs=[pl.BlockSpec((tq, hpg * MLA_HEAD_PAD), lambda b, h, i: (b * nq + i, h)),
                  pl.BlockSpec((seq, hpg * MLA_HEAD_PAD), lambda b, h, i: (b, h)),
                  pl.BlockSpec((seq, hpg * MLA_V), lambda b, h, i: (b, h))],
        out_specs=pl.BlockSpec((tq, hpg * MLA_V), lambda b, h, i: (b * nq + i, h)),
        out_shape=jax.ShapeDtypeStruct((batch * seq, MLA_HEADS * MLA_V), BF16),
        compiler_params=_cparams(("parallel", "parallel", "arbitrary")),
        name="mla_attention",
    )(q, k, v)


def _split2(x):
    hi = x.astype(BF16)
    return hi, (x - hi.astype(F32)).astype(BF16)


def _dot_x3(a, b):
    ah, al = _split2(a)
    bh, bl = _split2(b)
    return _dot(jnp.concatenate([ah, al, ah], axis=1), jnp.concatenate([bh, bh, bl], axis=0))


def _dot_nt_x3(a, b):
    ah, al = _split2(a)
    bh, bl = _split2(b)
    return _dot_nt(jnp.concatenate([ah, al, ah], axis=1), jnp.concatenate([bh, bh, bl], axis=1))


def _dot_bf(a, b):
    return _dot(a.astype(BF16), b.astype(BF16))


def _unit_lower_inverse(a_list, levels, dot):
    n = a_list[0].shape[0]
    eye = (lax.broadcasted_iota(jnp.int32, (n, n), 0) == lax.broadcasted_iota(jnp.int32, (n, n), 1)).astype(F32)
    p = [-a for a in a_list]
    r = [eye + x for x in p]
    p = [dot(x, x) for x in p]
    for _ in range(levels - 2):
        s = [dot(jnp.concatenate([ri, pi], axis=0), pi) for ri, pi in zip(r, p)]
        r = [ri + si[:n] for ri, si in zip(r, s)]
        p = [si[n:] for si in s]
    return [ri + dot(ri, pi) for ri, pi in zip(r, p)]


def _cumsum_rows(tri, x):
    hi = x.astype(BF16)
    r1 = x - hi.astype(F32)
    mid = r1.astype(BF16)
    lo = (r1 - mid.astype(F32)).astype(BF16)
    return _dot(tri, hi) + _dot(tri, mid) + _dot(tri, lo)


def _col_of_row(row_vec, n):
    eye = lax.broadcasted_iota(jnp.int32, (n, n), 0) == lax.broadcasted_iota(jnp.int32, (n, n), 1)
    return jnp.sum(jnp.where(eye, jnp.broadcast_to(row_vec, (n, n)), 0.0), axis=1, keepdims=True)


def _gdn_kernel(gq_ref, gk_ref, gv_ref, gz_ref, sm_ref, cw_ref, alog_ref, dtb_ref, on_ref, o_ref,
                xq_scr, xk_scr, xv_scr, state_scr, *, chunk):
    c = chunk
    heads = range(GDN_HEADS)
    hs = [slice(h * LANES, (h + 1) * LANES) for h in heads]

    @pl.when(pl.program_id(1) == 0)
    def _():
        for scr in (xq_scr, xk_scr, xv_scr):
            scr[0:8, :] = jnp.zeros((8, scr.shape[1]), F32)
        state_scr[...] = jnp.zeros_like(state_scr)

    def conv_silu(x_ref, scr, widx):
        scr[8:8 + c, :] = x_ref[...].astype(F32)
        acc = cw_ref[widx, GDN_CONV - 1:GDN_CONV, :] * scr[8:8 + c, :]
        for j in range(GDN_CONV - 1):
            s = GDN_CONV - 1 - j
            acc = acc + cw_ref[widx, j:j + 1, :] * scr[8 - s:8 - s + c, :]
        scr[0:8, :] = scr[c:c + 8, :]
        return _silu(acc)

    def l2n(x):
        return x * lax.rsqrt(jnp.sum(x * x, axis=-1, keepdims=True) + 1e-6)

    def bf(x):
        return x.astype(BF16)

    qa = conv_silu(gq_ref, xq_scr, 0)
    ka = conv_silu(gk_ref, xk_scr, 1)
    va = conv_silu(gv_ref, xv_scr, 2)
    q = [l2n(qa[:, s]) * (GDN_DK ** -0.5) for s in hs]
    k = [l2n(ka[:, s]) for s in hs]
    v = [va[:, s] for s in hs]

    sm = sm_ref[...].astype(F32)
    lane = lax.broadcasted_iota(jnp.int32, (c, LANES), 1)
    beta = [_sigmoid(jnp.sum(jnp.where(lane == h, sm, 0.0), axis=-1, keepdims=True)) for h in heads]
    ga = [jnp.sum(jnp.where(lane == GDN_HEADS + h, sm, 0.0), axis=-1, keepdims=True) for h in heads]
    g = [-jnp.exp(alog_ref[:, hs[h]]) * _softplus(jnp.broadcast_to(ga[h], (c, LANES)) + dtb_ref[:, hs[h]])
         for h in heads]

    ri = lax.broadcasted_iota(jnp.int32, (c, c), 0)
    cj = lax.broadcasted_iota(jnp.int32, (c, c), 1)
    tri = (cj <= ri).astype(BF16)
    gam = [_cumsum_rows(tri, x) for x in g]
    decay = [jnp.exp(jnp.where(cj <= ri, x - x.T, -1e30)) for x in gam]
    eg = [jnp.exp(x) for x in gam]
    gl = [x[c - 1:c, :] for x in gam]
    kb = [k[h] * beta[h] for h in heads]
    vb = [v[h] * beta[h] for h in heads]

    a_mat = [jnp.where(cj < ri, _dot_nt_x3(kb[h], k[h]) * decay[h], 0.0) for h in heads]
    tinv = _unit_lower_inverse(a_mat, 7, _dot_x3)
    rhs = [jnp.concatenate([kb[h] * eg[h], vb[h]], axis=1) for h in heads]
    wu = [bf(_dot_x3(tinv[h], rhs[h])) for h in heads]
    attn = [_dot_nt(bf(q[h]), bf(k[h])) * decay[h] for h in heads]
    k_dec = [k[h] * jnp.exp(gl[h] - gam[h]) for h in heads]
    lhs = [bf(jnp.concatenate([attn[h], k_dec[h].T], axis=0)) for h in heads]
    z = [_dot(lhs[h], wu[h]) for h in heads]
    state = [state_scr[h] for h in heads]
    l2 = [bf(jnp.concatenate([q[h] * eg[h] - z[h][:c, :LANES], -z[h][c:, :LANES]], axis=0)) for h in heads]
    rr = [_dot(l2[h], bf(state[h])) for h in heads]
    for h in heads:
        state_scr[h] = state[h] * jnp.exp(gl[h]) + rr[h][c:] + z[h][c:, LANES:]
        o = rr[h][:c] + z[h][:c, LANES:]
        o_ref[:, hs[h]] = (_rms(o, on_ref[...]) * _silu(gz_ref[:, hs[h]].astype(F32))).astype(o_ref.dtype)


def gdn_mixer(hcat, conv_w, a_log, dt_bias, out_norm, batch, seq):
    c = GDN_CHUNK
    assert c == LANES and seq % c == 0
    nc = seq // c
    width = GDN_HEADS * LANES
    blk = lambda off: pl.BlockSpec((c, width), lambda b, i, off=off: (b * nc + i, off // width))
    const2 = lambda b, i: (0, 0)
    cw = conv_w.reshape(GDN_CONV, 3, width).transpose(1, 0, 2)
    alog = jnp.repeat(a_log.astype(F32), LANES).reshape(1, width)
    dtb = jnp.repeat(dt_bias.astype(F32), LANES).reshape(1, width)
    return pl.pallas_call(
        functools.partial(_gdn_kernel, chunk=c),
        grid=(batch, nc),
        in_specs=[blk(HC_GQ), blk(HC_GK), blk(HC_GV), blk(HC_GZ),
                  pl.BlockSpec((c, LANES), lambda b, i: (b * nc + i, HC_SM // LANES)),
                  pl.BlockSpec((3, GDN_CONV, width), lambda b, i: (0, 0, 0)),
                  pl.BlockSpec((1, width), const2),
                  pl.BlockSpec((1, width), const2),
                  pl.BlockSpec((1, GDN_DV), const2)],
        out_specs=pl.BlockSpec((c, width), lambda b, i: (b * nc + i, 0)),
        out_shape=jax.ShapeDtypeStruct((batch * seq, width), BF16),
        scratch_shapes=[pltpu.VMEM((c + 8, width), F32)] * 3 + [pltpu.VMEM((GDN_HEADS, GDN_DK, GDN_DV), F32)],
        compiler_params=_cparams(("parallel", "arbitrary")),
        name="gdn_mixer",
    )(hcat, hcat, hcat, hcat, hcat, cw, alog, dtb, out_norm)


def _rwkv_mix_kernel(h_ref, hp_ref, g_ref, mix_ref, o_ref, *, tm, seq):
    i = pl.program_id(0)
    g = g_ref[...]
    xn = _rms(h_ref[...], g)
    prev_last = _rms(hp_ref[...], g)[7:8, :]
    prev_last = jnp.where((i * tm) % seq == 0, 0.0, prev_last)
    rows = lax.broadcasted_iota(jnp.int32, xn.shape, 0)
    xprev = jnp.where(rows == 0, prev_last, pltpu.roll(xn, 1, 0))
    xx = xprev - xn
    for j in range(6):
        o_ref[j] = (xn + xx * mix_ref[j:j + 1, :]).astype(o_ref.dtype)


def rwkv_mix(h, g, mix, seq, *, tm=256):
    m, d = h.shape
    tm = _tile(seq, tm)
    return pl.pallas_call(
        functools.partial(_rwkv_mix_kernel, tm=tm, seq=seq),
        grid=(m // tm,),
        in_specs=[pl.BlockSpec((tm, d), lambda i: (i, 0)),
                  pl.BlockSpec((8, d), lambda i: (jnp.maximum(i * (tm // 8) - 1, 0), 0)),
                  pl.BlockSpec((1, d), lambda i: (0, 0)),
                  pl.BlockSpec((6, d), lambda i: (0, 0))],
        out_specs=pl.BlockSpec((6, tm, d), lambda i: (0, i, 0)),
        out_shape=jax.ShapeDtypeStruct((6, m, d), BF16),
        compiler_params=_cparams(("parallel",)),
        name="rwkv_mix",
    )(h, h, g, mix)


def _lora_kernel(x_ref, a_ref, b_ref, bias_ref, *rest, mid, post):
    o_ref = rest[-1]
    z = _dot(x_ref[...], a_ref[...])
    if mid == "tanh":
        z = jnp.tanh(z)
    elif mid == "sigmoid":
        z = _sigmoid(z)
    y = _dot(z.astype(BF16), b_ref[...]) + bias_ref[...]
    if post == "logdecay":
        y = -jnp.exp(-_softplus(-y) - 0.5)
    elif post == "sigmoid":
        y = _sigmoid(y)
    elif post == "vres":
        v = rest[0][...].astype(F32)
        y = v + (rest[1][...].astype(F32) - v) * _sigmoid(y)
    o_ref[...] = y.astype(o_ref.dtype)


def lora(x, idx, a, b, bias, *extras, mid, post, out_dtype, tm=512):
    _, m, d = x.shape
    r = a.shape[1]
    n = b.shape[1]
    tm = _tile(m, tm)
    row = lambda i: (i, 0)
    const = lambda i: (0, 0)
    return pl.pallas_call(
        functools.partial(_lora_kernel, mid=mid, post=post),
        grid=(m // tm,),
        in_specs=[pl.BlockSpec((None, tm, d), lambda i: (idx, i, 0)),
                  pl.BlockSpec((d, r), const), pl.BlockSpec((r, n), const),
                  pl.BlockSpec((1, n), const)] + [pl.BlockSpec((tm, n), row) for _ in extras],
        out_specs=pl.BlockSpec((tm, n), row),
        out_shape=jax.ShapeDtypeStruct((m, n), out_dtype),
        compiler_params=_cparams(("parallel",)),
        name="lora_" + post,
    )(x, a, b, bias, *extras)


def _rwkv_kernel(r_ref, k_ref, v_ref, a_ref, lw_ref, gt_ref, kk_ref, ka_ref, rk_ref, lnw_ref, lnb_ref,
                 o_ref, state_scr, *, chunk, pairs):
    c = chunk
    ps = range(pairs)
    sls = [slice(p * LANES, (p + 1) * LANES) for p in ps]

    @pl.when(pl.program_id(2) == 0)
    def _():
        state_scr[...] = jnp.zeros_like(state_scr)

    lane = lax.broadcasted_iota(jnp.int32, (c, LANES), 1)
    m0 = lane < RWKV_HEAD
    ri = lax.broadcasted_iota(jnp.int32, (c, c), 0)
    cj = lax.broadcasted_iota(jnp.int32, (c, c), 1)
    tri = (cj <= ri).astype(BF16)
    r2 = lax.broadcasted_iota(jnp.int32, (2 * c, 2 * c), 0)
    c2 = lax.broadcasted_iota(jnp.int32, (2 * c, 2 * c), 1)
    strict2 = c2 < r2
    rw = lax.broadcasted_iota(jnp.int32, (c, 2 * c), 0)
    cw = lax.broadcasted_iota(jnp.int32, (c, 2 * c), 1)
    incl_w = jnp.where(cw >= c, cw - c, cw) <= rw
    zero_blk = jnp.zeros((2 * c, LANES), BF16)

    def seg_sum(x):
        s0 = jnp.sum(jnp.where(m0, x, 0.0), axis=-1, keepdims=True)
        s1 = jnp.sum(jnp.where(m0, 0.0, x), axis=-1, keepdims=True)
        return jnp.where(m0, s0, s1)

    def stack(x):
        return jnp.concatenate([jnp.where(m0, x, 0.0), jnp.where(m0, 0.0, x)], axis=0)

    def bf(x):
        return x.astype(BF16)

    r = [r_ref[:, s].astype(F32) for s in sls]
    k = [k_ref[:, s].astype(F32) for s in sls]
    v = [v_ref[:, s].astype(F32) for s in sls]
    a = [a_ref[:, s].astype(F32) for s in sls]
    lw = [lw_ref[:, s] for s in sls]
    kk = [k[p] * kk_ref[:, sls[p]] for p in ps]
    kk = [x * lax.rsqrt(seg_sum(x * x) + 1e-12) for x in kk]
    kh = [k[p] * (1.0 + (a[p] - 1.0) * ka_ref[:, sls[p]]) for p in ps]
    bv = [kk[p] * a[p] for p in ps]

    cum = [_cumsum_rows(tri, x) for x in lw]
    cl = [x[c - 1:c, :] for x in cum]
    e_neg = [jnp.exp(-x) for x in cum]
    e_end = [jnp.exp(cl[p] - cum[p]) for p in ps]
    rg = [r[p] * jnp.exp(cum[p]) for p in ps]
    rgc = [bf(x) for x in rg]
    xs = [bf(stack(kk[p] * jnp.exp(cum[p] - lw[p]))) for p in ps]
    bs = [bf(stack(bv[p] * e_neg[p])) for p in ps]
    ks = [bf(stack(kh[p] * e_neg[p])) for p in ps]
    vs = [bf(stack(v[p])) for p in ps]
    kht = [stack(kh[p] * e_end[p]).T for p in ps]
    bht = [stack(bv[p] * e_end[p]).T for p in ps]

    ab = [_dot_nt(xs[p], jnp.concatenate([bs[p], ks[p]], axis=0)) for p in ps]
    akk = [jnp.where(strict2, x[:, :2 * c], 0.0) for x in ab]
    akv = [bf(jnp.where(strict2, x[:, 2 * c:], 0.0)) for x in ab]
    tinv = [bf(x) for x in _unit_lower_inverse(akk, 6, _dot_bf)]
    w1 = [bf(_dot(akv[p], vs[p])) for p in ps]
    m12 = [bf(_dot(tinv[p], jnp.concatenate([xs[p], w1[p]], axis=1))) for p in ps]
    bb = [_dot_nt(rgc[p], jnp.concatenate([ks[p], bs[p]], axis=0)) for p in ps]
    lhs = [bf(jnp.concatenate([
        jnp.concatenate([jnp.where(incl_w, x[:, :2 * c], 0.0), -jnp.where(incl_w, x[:, 2 * c:], 0.0)], axis=1),
        jnp.concatenate([kht[p], -bht[p]], axis=1)], axis=0)) for p, x in enumerate(bb)]
    rhs = [jnp.concatenate([jnp.concatenate([vs[p], zero_blk], axis=1),
                            jnp.concatenate([m12[p][:, LANES:], m12[p][:, :LANES]], axis=1)], axis=0)
           for p in ps]
    z = [_dot(lhs[p], rhs[p]) for p in ps]
    state = [state_scr[p] for p in ps]
    l2 = [bf(jnp.concatenate([rg[p] + z[p][:c, LANES:], z[p][c:, LANES:]], axis=0)) for p in ps]
    rr = [_dot(l2[p], bf(state[p])) for p in ps]
    for p in ps:
        sl = sls[p]
        state_scr[p] = state[p] * _col_of_row(jnp.exp(cl[p]), LANES) + rr[p][c:] + z[p][c:, :LANES]
        y = rr[p][:c] + z[p][:c, :LANES]
        mu = seg_sum(y) * (1.0 / RWKV_HEAD)
        yc = y - mu
        var = seg_sum(yc * yc) * (1.0 / RWKV_HEAD)
        yn = yc * lax.rsqrt(var + RWKV_LN_EPS) * lnw_ref[:, sl] + lnb_ref[:, sl]
        bonus = seg_sum(r[p] * kh[p] * rk_ref[:, sl]) * v[p]
        o_ref[:, sl] = ((yn + bonus) * gt_ref[:, sl].astype(F32)).astype(o_ref.dtype)


def rwkv_recurrence(r, k, v, a, lw, gate, k_k, k_a, r_k, ln_w, ln_b, batch, seq, *, pairs=8):
    c = RWKV_CHUNK
    m, d = r.shape
    nc = seq // c
    width = pairs * LANES
    blk = pl.BlockSpec((c, width), lambda b, hp, i: (b * nc + i, hp))
    par = pl.BlockSpec((1, width), lambda b, hp, i: (0, hp))
    return pl.pallas_call(
        functools.partial(_rwkv_kernel, chunk=c, pairs=pairs),
        grid=(batch, d // width, nc),
        in_specs=[blk] * 6 + [par] * 5,
        out_specs=blk,
        out_shape=jax.ShapeDtypeStruct((m, d), BF16),
        scratch_shapes=[pltpu.VMEM((pairs, LANES, LANES), F32)],
        compiler_params=_cparams(("parallel", "parallel", "arbitrary")),
        name="rwkv_recurrence",
    )(r, k, v, a, lw, gate, k_k, k_a, r_k, ln_w, ln_b)


def _pad_cols(w, n):
    return jnp.pad(w, ((0, 0), (0, n - w.shape[1])))


def _rot_cols(w):
    half = MLA_ROPE // 2
    return jnp.concatenate([-w[..., half:], w[..., :half]], axis=-1)


def _hybrid_layer(h, ct, st, batch, seq, g_pre, g_post, w_in, q_norm, w_uq, kv_norm, w_ukv, conv_w, a_log,
                  dt_bias, out_norm, w_out):
    d = D_MODEL
    o_cq, o_ckv, o_kr = 0, MLA_Q_LORA, MLA_Q_LORA + MLA_KV_LORA
    o_g = o_kr + MLA_ROPE
    o_sm = o_g + 4 * GDN_HEADS * GDN_DK
    w_kr = w_in[:, o_kr:o_g]
    w_cat = jnp.concatenate([w_in[:, :o_kr], w_in[:, o_g:o_sm], w_kr, _rot_cols(w_kr),
                             _pad_cols(w_in[:, o_sm:], LANES)], axis=1).astype(BF16)
    hcat = norm_matmul(h, g_pre.reshape(1, d), w_cat, tn=768, out_dtype=BF16)

    wq = w_uq.reshape(MLA_Q_LORA, MLA_HEADS, MLA_QK)
    wq_rope = wq[..., MLA_NOPE:]
    wq = jnp.concatenate([wq[..., :MLA_NOPE], wq_rope, _rot_cols(wq_rope)], axis=-1)
    wq = wq.reshape(MLA_Q_LORA, MLA_HEADS * MLA_HEAD_PAD).astype(BF16)
    q, k, v = mla_proj(hcat, q_norm.reshape(1, -1), kv_norm.reshape(1, -1), wq, w_ukv.astype(BF16), ct, st)
    mla_out = mla_attention(q, k, v, batch, seq)
    gdn_out = gdn_mixer(hcat, conv_w, a_log, dt_bias, out_norm.reshape(1, -1), batch, seq)
    return matmul_norm_residual([mla_out, gdn_out], w_out.astype(BF16), g_post.reshape(1, d), h, tn=d)


def _rwkv_layer(h, v_first, batch, seq, g_pre, g_post, mix, w_r, w_k, w_v, w_o, w0, w1, w2, a0, a1, a2,
                g1, g2, k_k, k_a, r_k, ln_w, ln_b, vres):
    d = D_MODEL
    row = lambda t: t.reshape(1, d).astype(F32)
    padc = lambda w: _pad_cols(w, -(-w.shape[1] // LANES) * LANES).astype(BF16)
    padr = lambda w: jnp.pad(w, ((0, -(-w.shape[0] // LANES) * LANES - w.shape[0]), (0, 0))).astype(BF16)
    xs = rwkv_mix(h, g_pre.reshape(1, d), mix, seq)
    r = matmul(xs, 0, w_r.astype(BF16), out_dtype=BF16)
    k = matmul(xs, 2, w_k.astype(BF16), out_dtype=BF16)
    v = matmul(xs, 3, w_v.astype(BF16), out_dtype=BF16)
    lw = lora(xs, 1, padc(w1), padr(w2), row(w0), mid="tanh", post="logdecay", out_dtype=F32)
    a = lora(xs, 4, padc(a1), padr(a2), row(a0), mid="none", post="sigmoid", out_dtype=BF16)
    gate = lora(xs, 5, padc(g1), padr(g2), jnp.zeros((1, d), F32), mid="sigmoid", post="none", out_dtype=BF16)
    if vres is None:
        v_first = v
    else:
        v0, v1, v2 = vres
        v = lora(xs, 3, padc(v1), padr(v2), row(v0), v, v_first, mid="none", post="vres", out_dtype=BF16)
    y = rwkv_recurrence(r, k, v, a, lw, gate, row(k_k), row(k_a), row(r_k), row(ln_w), row(ln_b), batch, seq)
    return matmul_norm_residual([y], w_o.astype(BF16), g_post.reshape(1, d), h, tn=d), v_first


def _ffn_layer(h, g_pre, g_post, w_gate, w_up, w_down):
    d = D_MODEL
    act = ffn_up(h, g_pre.reshape(1, d), w_gate.astype(BF16), w_up.astype(BF16))
    return matmul_norm_residual([act], w_down.astype(BF16), g_post.reshape(1, d), h)


def _rope_tables(positions):
    inv_freq = 1.0 / (ROPE_THETA ** (jnp.arange(0, MLA_ROPE, 2, dtype=F32) / MLA_ROPE))
    ang = positions.astype(F32).reshape(-1, 1) * inv_freq
    zeros = jnp.zeros((ang.shape[0], LANES - MLA_ROPE), F32)
    cos, sin = jnp.cos(ang), jnp.sin(ang)
    return jnp.concatenate([cos, cos, zeros], axis=1), jnp.concatenate([sin, sin, zeros], axis=1)


def kernel(x, positions, norm_mix_pre, norm_mix_post, norm_ffn_pre, norm_ffn_post, hyb_w_in, mla_q_norm, mla_w_uq, mla_kv_norm, mla_w_ukv, gdn_conv_w, gdn_a_log, gdn_dt_bias, gdn_out_norm, hyb_w_out, rwkv_mix, rwkv_w_r, rwkv_w_k, rwkv_w_v, rwkv_w_o, rwkv_w0, rwkv_w1, rwkv_w2, rwkv_a0, rwkv_a1, rwkv_a2, rwkv_g1, rwkv_g2, rwkv_k_k, rwkv_k_a, rwkv_r_k, rwkv_ln_w, rwkv_ln_b, rwkv_v0, rwkv_v1, rwkv_v2, ffn_w_gate, ffn_w_up, ffn_w_down):
    batch, seq, d = x.shape
    h = x.reshape(batch * seq, d)
    ct, st = _rope_tables(positions)
    v_first = None
    for layer in range(DEPTH):
        if layer % 2 == 0:
            e = layer // 2
            h = _hybrid_layer(h, ct, st, batch, seq, norm_mix_pre[layer], norm_mix_post[layer], hyb_w_in[e],
                              mla_q_norm[e], mla_w_uq[e], mla_kv_norm[e], mla_w_ukv[e], gdn_conv_w[e],
                              gdn_a_log[e], gdn_dt_bias[e], gdn_out_norm[e], hyb_w_out[e])
        else:
            o = layer // 2
            vres = None if o == 0 else (rwkv_v0[o - 1], rwkv_v1[o - 1], rwkv_v2[o - 1])
            h, v_first = _rwkv_layer(h, v_first, batch, seq, norm_mix_pre[layer], norm_mix_post[layer],
                                     rwkv_mix[o], rwkv_w_r[o], rwkv_w_k[o], rwkv_w_v[o], rwkv_w_o[o],
                                     rwkv_w0[o], rwkv_w1[o], rwkv_w2[o], rwkv_a0[o], rwkv_a1[o], rwkv_a2[o],
                                     rwkv_g1[o], rwkv_g2[o], rwkv_k_k[o], rwkv_k_a[o], rwkv_r_k[o],
                                     rwkv_ln_w[o], rwkv_ln_b[o], vres)
        h = _ffn_layer(h, norm_ffn_pre[layer], norm_ffn_post[layer], ffn_w_gate[layer], ffn_w_up[layer],
                       ffn_w_down[layer])
    return h.reshape(batch, seq, d)
```

```python
import functools

import jax
import jax.numpy as jnp
import numpy as np
from jax import lax
from jax.experimental import pallas as pl
from jax.experimental.pallas import tpu as pltpu

F32 = jnp.float32
BF16 = jnp.bfloat16
HI = lax.Precision.HIGHEST

D_MODEL = 2048
DEPTH = 4
LANES = 128
VMEM_LIMIT = 56 * 1024 * 1024

MLA_HEADS = 8
MLA_Q_LORA = 512
MLA_KV_LORA = 512
MLA_NOPE = 128
MLA_ROPE = 64
MLA_V = 128
MLA_QK = MLA_NOPE + MLA_ROPE
MLA_HEAD_PAD = 256
ROPE_THETA = 10000.0

GDN_HEADS = 8
GDN_DK = 128
GDN_DV = 128
GDN_CONV = 4
GDN_CHUNK = 128

RWKV_HEAD = 64
RWKV_HEADS = D_MODEL // RWKV_HEAD
RWKV_CHUNK = 64
RWKV_LN_EPS = 64e-5
RMS_EPS = 1e-6

HC_CQ, HC_CKV, HC_GQ, HC_GK, HC_GV, HC_GZ, HC_KR, HC_SM = 0, 512, 1024, 2048, 3072, 4096, 5120, 5248
HC_COLS = 5376


def _cparams(sem):
    return pltpu.CompilerParams(dimension_semantics=sem, vmem_limit_bytes=VMEM_LIMIT)


def _tile(n, pref):
    t = min(n, pref)
    while n % t:
        t -= 8
    return t


def _dot(a, b, prec=None):
    return jnp.dot(a, b, preferred_element_type=F32, precision=prec)


def _dot_nt(a, b, prec=None):
    return lax.dot_general(a, b, (((1,), (1,)), ((), ())), preferred_element_type=F32, precision=prec)


def _rms(x, g, eps=RMS_EPS):
    return x * lax.rsqrt(jnp.mean(x * x, axis=-1, keepdims=True) + eps) * g


def _sigmoid(x):
    return 1.0 / (1.0 + jnp.exp(-x))


def _silu(x):
    return x * _sigmoid(x)


def _softplus(x):
    return jnp.maximum(x, 0.0) + jnp.log(1.0 + jnp.exp(-jnp.abs(x)))


def _norm_mm_kernel(x_ref, g_ref, w_ref, o_ref, xn_ref):
    @pl.when(pl.program_id(1) == 0)
    def _():
        xn_ref[...] = _rms(x_ref[...], g_ref[...]).astype(BF16)

    o_ref[...] = _dot(xn_ref[...], w_ref[...]).astype(o_ref.dtype)


def norm_matmul(x, g, w, *, tn, out_dtype=F32, tm=512):
    m, k = x.shape
    n = w.shape[1]
    tm = _tile(m, tm)
    return pl.pallas_call(
        _norm_mm_kernel,
        grid=(m // tm, n // tn),
        in_specs=[pl.BlockSpec((tm, k), lambda i, j: (i, 0)),
                  pl.BlockSpec((1, k), lambda i, j: (0, 0)),
                  pl.BlockSpec((k, tn), lambda i, j: (0, j))],
        out_specs=pl.BlockSpec((tm, tn), lambda i, j: (i, j)),
        out_shape=jax.ShapeDtypeStruct((m, n), out_dtype),
        scratch_shapes=[pltpu.VMEM((tm, k), BF16)],
        compiler_params=_cparams(("parallel", "arbitrary")),
        name="norm_matmul",
    )(x, g, w)


def _ffn_up_kernel(x_ref, g_ref, wg_ref, wu_ref, o_ref, xn_ref):
    @pl.when(pl.program_id(1) == 0)
    def _():
        xn_ref[...] = _rms(x_ref[...], g_ref[...]).astype(BF16)

    xn = xn_ref[...]
    gate = _dot(xn, wg_ref[...])
    up = _dot(xn, wu_ref[...])
    o_ref[...] = (_silu(gate) * up).astype(o_ref.dtype)


def ffn_up(h, g, wg, wu, *, tm=512, tn=1408):
    m, k = h.shape
    n = wg.shape[1]
    tm = _tile(m, tm)
    tn = _tile(n, tn)
    return pl.pallas_call(
        _ffn_up_kernel,
        grid=(m // tm, n // tn),
        in_specs=[pl.BlockSpec((tm, k), lambda i, j: (i, 0)),
                  pl.BlockSpec((1, k), lambda i, j: (0, 0)),
                  pl.BlockSpec((k, tn), lambda i, j: (0, j)),
                  pl.BlockSpec((k, tn), lambda i, j: (0, j))],
        out_specs=pl.BlockSpec((tm, tn), lambda i, j: (i, j)),
        out_shape=jax.ShapeDtypeStruct((m, n), BF16),
        scratch_shapes=[pltpu.VMEM((tm, k), BF16)],
        compiler_params=_cparams(("parallel", "arbitrary")),
        name="ffn_up",
    )(h, g, wg, wu)


def _mm_norm_res_kernel(*refs, tn, nb, n_a):
    a_refs = refs[:n_a]
    w_ref, g_ref, h_ref, o_ref, f_ref = refs[n_a:]
    j = pl.program_id(1)
    k0 = 0
    f = None
    for a_ref in a_refs:
        part = _dot(a_ref[...], w_ref[k0:k0 + a_ref.shape[1], :])
        f = part if f is None else f + part
        k0 += a_ref.shape[1]
    f_ref[j] = f

    @pl.when(j == nb - 1)
    def _():
        ssq = jnp.sum(f_ref[0] * f_ref[0], axis=-1, keepdims=True)
        for b in range(1, nb):
            ssq = ssq + jnp.sum(f_ref[b] * f_ref[b], axis=-1, keepdims=True)
        inv = lax.rsqrt(ssq * (1.0 / (nb * tn)) + RMS_EPS)
        for b in range(nb):
            cs = slice(b * tn, (b + 1) * tn)
            o_ref[:, cs] = h_ref[:, cs] + f_ref[b] * inv * g_ref[:, cs]


def matmul_norm_residual(a_parts, w, g, h, *, tm=512, tn=512):
    m = a_parts[0].shape[0]
    k, n = w.shape
    tm = _tile(m, tm)
    tn = _tile(n, tn)
    nb = n // tn
    return pl.pallas_call(
        functools.partial(_mm_norm_res_kernel, tn=tn, nb=nb, n_a=len(a_parts)),
        grid=(m // tm, nb),
        in_specs=[pl.BlockSpec((tm, a.shape[1]), lambda i, j: (i, 0)) for a in a_parts] + [
                  pl.BlockSpec((k, tn), lambda i, j: (0, j)),
                  pl.BlockSpec((1, n), lambda i, j: (0, 0)),
                  pl.BlockSpec((tm, n), lambda i, j: (i, 0))],
        out_specs=pl.BlockSpec((tm, n), lambda i, j: (i, 0)),
        out_shape=jax.ShapeDtypeStruct((m, n), F32),
        scratch_shapes=[pltpu.VMEM((nb, tm, tn), F32)],
        compiler_params=_cparams(("parallel", "arbitrary")),
        name="matmul_norm_residual",
    )(*a_parts, w, g, h)


def _mm_kernel(a_ref, w_ref, o_ref):
    o_ref[...] = _dot(a_ref[...], w_ref[...]).astype(o_ref.dtype)


def matmul(a, idx, w, *, tm=1024, tn=1024, out_dtype=F32):
    _, m, k = a.shape
    n = w.shape[1]
    tm = _tile(m, tm)
    tn = _tile(n, tn)
    return pl.pallas_call(
        _mm_kernel,
        grid=(m // tm, n // tn),
        in_specs=[pl.BlockSpec((None, tm, k), lambda i, j: (idx, i, 0)),
                  pl.BlockSpec((k, tn), lambda i, j: (0, j))],
        out_specs=pl.BlockSpec((tm, tn), lambda i, j: (i, j)),
        out_shape=jax.ShapeDtypeStruct((m, n), out_dtype),
        compiler_params=_cparams(("parallel", "arbitrary")),
        name="matmul",
    )(a, w)


def _rope_group(grp, ct, st):
    return grp * ct + pltpu.roll(grp, 64, 1) * st


def _mla_proj_kernel(cq_ref, ckv_ref, kr_ref, qn_ref, kvn_ref, wq_ref, wkv_ref, ct_ref, st_ref,
                     q_ref, k_ref, v_ref):
    scale = MLA_QK ** -0.5 * np.log2(np.e)
    ct = ct_ref[...]
    st = st_ref[...]
    xq = _rms(cq_ref[...].astype(F32), qn_ref[...]).astype(BF16)
    xkv = _rms(ckv_ref[...].astype(F32), kvn_ref[...]).astype(BF16)
    kr = _rope_group(kr_ref[...].astype(F32), ct, st).astype(BF16)
    for h in range(MLA_HEADS):
        c0 = h * MLA_HEAD_PAD
        yq = _dot(xq, wq_ref[:, c0:c0 + MLA_HEAD_PAD])
        q_ref[:, c0:c0 + LANES] = (yq[:, :LANES] * scale).astype(BF16)
        q_ref[:, c0 + LANES:c0 + 2 * LANES] = (_rope_group(yq[:, LANES:], ct, st) * scale).astype(BF16)
        ykv = _dot(xkv, wkv_ref[:, c0:c0 + MLA_HEAD_PAD])
        k_ref[:, c0:c0 + LANES] = ykv[:, :LANES].astype(BF16)
        k_ref[:, c0 + LANES:c0 + 2 * LANES] = kr
        v_ref[:, h * MLA_V:(h + 1) * MLA_V] = ykv[:, LANES:].astype(BF16)


def mla_proj(hcat, q_norm, kv_norm, wq, wkv, ct, st, *, tm=512):
    m = hcat.shape[0]
    tm = _tile(m, tm)
    hp = MLA_HEADS * MLA_HEAD_PAD
    row = lambda i: (i, 0)
    const = lambda i: (0, 0)
    return pl.pallas_call(
        _mla_proj_kernel,
        grid=(m // tm,),
        in_specs=[pl.BlockSpec((tm, MLA_Q_LORA), lambda i: (i, HC_CQ // MLA_Q_LORA)),
                  pl.BlockSpec((tm, MLA_KV_LORA), lambda i: (i, HC_CKV // MLA_KV_LORA)),
                  pl.BlockSpec((tm, LANES), lambda i: (i, HC_KR // LANES)),
                  pl.BlockSpec((1, MLA_Q_LORA), const),
                  pl.BlockSpec((1, MLA_KV_LORA), const),
                  pl.BlockSpec((MLA_Q_LORA, hp), const),
                  pl.BlockSpec((MLA_KV_LORA, hp), const),
                  pl.BlockSpec((tm, LANES), row),
                  pl.BlockSpec((tm, LANES), row)],
        out_specs=[pl.BlockSpec((tm, hp), row),
                   pl.BlockSpec((tm, hp), row),
                   pl.BlockSpec((tm, MLA_HEADS * MLA_V), row)],
        out_shape=[jax.ShapeDtypeStruct((m, hp), BF16),
                   jax.ShapeDtypeStruct((m, hp), BF16),
                   jax.ShapeDtypeStruct((m, MLA_HEADS * MLA_V), BF16)],
        compiler_params=_cparams(("parallel",)),
        name="mla_proj",
    )(hcat, hcat, hcat, q_norm, kv_norm, wq, wkv, ct, st)


def _attn_kernel(q_ref, k_ref, v_ref, o_ref, *, tq, tk, hpg):
    qi = pl.program_id(2)
    hs = range(hpg)
    q = [q_ref[:, h * MLA_HEAD_PAD:(h + 1) * MLA_HEAD_PAD] for h in hs]

    def step(ki, carry, masked):
        start = pl.multiple_of(ki * tk, tk)
        s = [_dot_nt(q[h], k_ref[pl.ds(start, tk), h * MLA_HEAD_PAD:(h + 1) * MLA_HEAD_PAD]) for h in hs]
        if masked is not None:
            row = lax.broadcasted_iota(jnp.int32, (tq, tk), 0)
            col = lax.broadcasted_iota(jnp.int32, (tq, tk), 1) + masked * tk
            s = [jnp.where(col <= row, x, -1e30) for x in s]
        m_new = [jnp.maximum(carry[h][0], jnp.max(s[h], axis=-1, keepdims=True)) for h in hs]
        alpha = [jnp.exp2(carry[h][0] - m_new[h]) for h in hs]
        p = [jnp.exp2(s[h] - m_new[h]) for h in hs]
        l_new = [alpha[h] * carry[h][1] + jnp.sum(p[h], axis=-1, keepdims=True) for h in hs]
        pv = [_dot(p[h].astype(BF16), v_ref[pl.ds(start, tk), h * MLA_V:(h + 1) * MLA_V]) for h in hs]
        return tuple((m_new[h], l_new[h], alpha[h] * carry[h][2] + pv[h]) for h in hs)

    init = tuple((jnp.full((tq, 1), -1e30, F32), jnp.zeros((tq, 1), F32), jnp.zeros((tq, MLA_V), F32))
                 for _ in hs)
    per_q = tq // tk
    carry = lax.fori_loop(0, qi * per_q, lambda ki, c: step(ki, c, None), init)
    for j in range(per_q):
        carry = step(qi * per_q + j, carry, j)
    for h in hs:
        o_ref[:, h * MLA_V:(h + 1) * MLA_V] = (carry[h][2] / carry[h][1]).astype(o_ref.dtype)


def mla_attention(q, k, v, batch, seq, *, tq=1024, tk=1024, hpg=2):
    tq = _tile(seq, tq)
    tk = _tile(tq, tk)
    nq = seq // tq
    return pl.pallas_call(
        functools.partial(_attn_kernel, tq=tq, tk=tk, hpg=hpg),
        grid=(batch, MLA_HEADS // hpg, nq),
        in_specs=[pl.BlockSpec((tq, hpg * MLA_HEAD_PAD), lambda b, h, i: (b * nq + i, h)),
                  pl.BlockSpec((seq, hpg * MLA_HEAD_PAD), lambda b, h, i: (b, h)),
                  pl.BlockSpec((seq, hpg * MLA_V), lambda b, h, i: (b, h))],
        out_specs=pl.BlockSpec((tq, hpg * MLA_V), lambda b, h, i: (b * nq + i, h)),
        out_shape=jax.ShapeDtypeStruct((batch * seq, MLA_HEADS * MLA_V), BF16),
        compiler_params=_cparams(("parallel", "parallel", "arbitrary")),
        name="mla_attention",
    )(q, k, v)


def _split2(x):
    hi = x.astype(BF16)
    return hi, (x - hi.astype(F32)).astype(BF16)


def _dot_x3(a, b):
    ah, al = _split2(a)
    bh, bl = _split2(b)
    return _dot(jnp.concatenate([ah, al, ah], axis=1), jnp.concatenate([bh, bh, bl], axis=0))


def _dot_nt_x3(a, b):
    ah, al = _split2(a)
    bh, bl = _split2(b)
    return _dot_nt(jnp.concatenate([ah, al, ah], axis=1), jnp.concatenate([bh, bh, bl], axis=1))


def _dot_bf(a, b):
    return _dot(a.astype(BF16), b.astype(BF16))


def _unit_lower_inverse(a_list, levels, dot):
    n = a_list[0].shape[0]
    eye = (lax.broadcasted_iota(jnp.int32, (n, n), 0) == lax.broadcasted_iota(jnp.int32, (n, n), 1)).astype(F32)
    p = [-a for a in a_list]
    r = [eye + x for x in p]
    p = [dot(x, x) for x in p]
    for _ in range(levels - 2):
        s = [dot(jnp.concatenate([ri, pi], axis=0), pi) for ri, pi in zip(r, p)]
        r = [ri + si[:n] for ri, si in zip(r, s)]
        p = [si[n:] for si in s]
    return [ri + dot(ri, pi) for ri, pi in zip(r, p)]


def _cumsum_rows(x):
    c = x.shape[0]
    row = lax.broadcasted_iota(jnp.int32, x.shape, 0)
    s = 1
    while s < c:
        x = x + jnp.where(row >= s, pltpu.roll(x, s, 0), 0.0)
        s *= 2
    return x


def _col_of_row(row_vec, n):
    eye = lax.broadcasted_iota(jnp.int32, (n, n), 0) == lax.broadcasted_iota(jnp.int32, (n, n), 1)
    return jnp.sum(jnp.where(eye, jnp.broadcast_to(row_vec, (n, n)), 0.0), axis=1, keepdims=True)


def _gdn_kernel(gq_ref, gk_ref, gv_ref, gz_ref, sm_ref, cw_ref, alog_ref, dtb_ref, on_ref, o_ref,
                xq_scr, xk_scr, xv_scr, state_scr, *, chunk):
    c = chunk
    heads = range(GDN_HEADS)
    hs = [slice(h * LANES, (h + 1) * LANES) for h in heads]

    @pl.when(pl.program_id(1) == 0)
    def _():
        for scr in (xq_scr, xk_scr, xv_scr):
            scr[0:8, :] = jnp.zeros((8, scr.shape[1]), F32)
        state_scr[...] = jnp.zeros_like(state_scr)

    def conv_silu(x_ref, scr, widx):
        scr[8:8 + c, :] = x_ref[...].astype(F32)
        acc = cw_ref[widx, GDN_CONV - 1:GDN_CONV, :] * scr[8:8 + c, :]
        for j in range(GDN_CONV - 1):
            s = GDN_CONV - 1 - j
            acc = acc + cw_ref[widx, j:j + 1, :] * scr[8 - s:8 - s + c, :]
        scr[0:8, :] = scr[c:c + 8, :]
        return _silu(acc)

    def l2n(x):
        return x * lax.rsqrt(jnp.sum(x * x, axis=-1, keepdims=True) + 1e-6)

    def bf(x):
        return x.astype(BF16)

    qa = conv_silu(gq_ref, xq_scr, 0)
    ka = conv_silu(gk_ref, xk_scr, 1)
    va = conv_silu(gv_ref, xv_scr, 2)
    q = [l2n(qa[:, s]) * (GDN_DK ** -0.5) for s in hs]
    k = [l2n(ka[:, s]) for s in hs]
    v = [va[:, s] for s in hs]

    sm = sm_ref[...].astype(F32)
    lane = lax.broadcasted_iota(jnp.int32, (c, LANES), 1)
    beta = [_sigmoid(jnp.sum(jnp.where(lane == h, sm, 0.0), axis=-1, keepdims=True)) for h in heads]
    ga = [jnp.sum(jnp.where(lane == GDN_HEADS + h, sm, 0.0), axis=-1, keepdims=True) for h in heads]
    g = [-jnp.exp(alog_ref[:, hs[h]]) * _softplus(jnp.broadcast_to(ga[h], (c, LANES)) + dtb_ref[:, hs[h]])
         for h in heads]

    ri = lax.broadcasted_iota(jnp.int32, (c, c), 0)
    cj = lax.broadcasted_iota(jnp.int32, (c, c), 1)
    gam = [_cumsum_rows(x) for x in g]
    decay = [jnp.exp(jnp.where(cj <= ri, x - x.T, -1e30)) for x in gam]
    eg = [jnp.exp(x) for x in gam]
    gl = [x[c - 1:c, :] for x in gam]
    kb = [k[h] * beta[h] for h in heads]
    vb = [v[h] * beta[h] for h in heads]

    a_mat = [jnp.where(cj < ri, _dot_nt_x3(kb[h], k[h]) * decay[h], 0.0) for h in heads]
    tinv = _unit_lower_inverse(a_mat, 7, _dot_x3)
    rhs = [jnp.concatenate([kb[h] * eg[h], vb[h]], axis=1) for h in heads]
    wu = [bf(_dot_x3(tinv[h], rhs[h])) for h in heads]
    attn = [_dot_nt(bf(q[h]), bf(k[h])) * decay[h] for h in heads]
    k_dec = [k[h] * jnp.exp(gl[h] - gam[h]) for h in heads]
    lhs = [bf(jnp.concatenate([attn[h], k_dec[h].T], axis=0)) for h in heads]
    z = [_dot(lhs[h], wu[h]) for h in heads]
    state = [state_scr[h] for h in heads]
    l2 = [bf(jnp.concatenate([q[h] * eg[h] - z[h][:c, :LANES], -z[h][c:, :LANES]], axis=0)) for h in heads]
    rr = [_dot(l2[h], bf(state[h])) for h in heads]
    for h in heads:
        state_scr[h] = state[h] * jnp.exp(gl[h]) + rr[h][c:] + z[h][c:, LANES:]
        o = rr[h][:c] + z[h][:c, LANES:]
        o_ref[:, hs[h]] = (_rms(o, on_ref[...]) * _silu(gz_ref[:, hs[h]].astype(F32))).astype(o_ref.dtype)


def gdn_mixer(hcat, conv_w, a_log, dt_bias, out_norm, batch, seq):
    c = GDN_CHUNK
    assert c == LANES and seq % c == 0
    nc = seq // c
    width = GDN_HEADS * LANES
    blk = lambda off: pl.BlockSpec((c, width), lambda b, i, off=off: (b * nc + i, off // width))
    const2 = lambda b, i: (0, 0)
    cw = conv_w.reshape(GDN_CONV, 3, width).transpose(1, 0, 2)
    alog = jnp.repeat(a_log.astype(F32), LANES).reshape(1, width)
    dtb = jnp.repeat(dt_bias.astype(F32), LANES).reshape(1, width)
    return pl.pallas_call(
        functools.partial(_gdn_kernel, chunk=c),
        grid=(batch, nc),
        in_specs=[blk(HC_GQ), blk(HC_GK), blk(HC_GV), blk(HC_GZ),
                  pl.BlockSpec((c, LANES), lambda b, i: (b * nc + i, HC_SM // LANES)),
                  pl.BlockSpec((3, GDN_CONV, width), lambda b, i: (0, 0, 0)),
                  pl.BlockSpec((1, width), const2),
                  pl.BlockSpec((1, width), const2),
                  pl.BlockSpec((1, GDN_DV), const2)],
        out_specs=pl.BlockSpec((c, width), lambda b, i: (b * nc + i, 0)),
        out_shape=jax.ShapeDtypeStruct((batch * seq, width), BF16),
        scratch_shapes=[pltpu.VMEM((c + 8, width), F32)] * 3 + [pltpu.VMEM((GDN_HEADS, GDN_DK, GDN_DV), F32)],
        compiler_params=_cparams(("parallel", "arbitrary")),
        name="gdn_mixer",
    )(hcat, hcat, hcat, hcat, hcat, cw, alog, dtb, out_norm)


def _rwkv_mix_kernel(h_ref, hp_ref, g_ref, mix_ref, o_ref, *, tm, seq):
    i = pl.program_id(0)
    g = g_ref[...]
    xn = _rms(h_ref[...], g)
    prev_last = _rms(hp_ref[...], g)[7:8, :]
    prev_last = jnp.where((i * tm) % seq == 0, 0.0, prev_last)
    rows = lax.broadcasted_iota(jnp.int32, xn.shape, 0)
    xprev = jnp.where(rows == 0, prev_last, pltpu.roll(xn, 1, 0))
    xx = xprev - xn
    for j in range(6):
        o_ref[j] = (xn + xx * mix_ref[j:j + 1, :]).astype(o_ref.dtype)


def rwkv_mix(h, g, mix, seq, *, tm=256):
    m, d = h.shape
    tm = _tile(seq, tm)
    return pl.pallas_call(
        functools.partial(_rwkv_mix_kernel, tm=tm, seq=seq),
        grid=(m // tm,),
        in_specs=[pl.BlockSpec((tm, d), lambda i: (i, 0)),
                  pl.BlockSpec((8, d), lambda i: (jnp.maximum(i * (tm // 8) - 1, 0), 0)),
                  pl.BlockSpec((1, d), lambda i: (0, 0)),
                  pl.BlockSpec((6, d), lambda i: (0, 0))],
        out_specs=pl.BlockSpec((6, tm, d), lambda i: (0, i, 0)),
        out_shape=jax.ShapeDtypeStruct((6, m, d), BF16),
        compiler_params=_cparams(("parallel",)),
        name="rwkv_mix",
    )(h, h, g, mix)


def _lora_kernel(x_ref, a_ref, b_ref, bias_ref, *rest, mid, post):
    o_ref = rest[-1]
    z = _dot(x_ref[...], a_ref[...])
    if mid == "tanh":
        z = jnp.tanh(z)
    elif mid == "sigmoid":
        z = _sigmoid(z)
    y = _dot(z.astype(BF16), b_ref[...]) + bias_ref[...]
    if post == "logdecay":
        y = -jnp.exp(-_softplus(-y) - 0.5)
    elif post == "sigmoid":
        y = _sigmoid(y)
    elif post == "vres":
        v = rest[0][...].astype(F32)
        y = v + (rest[1][...].astype(F32) - v) * _sigmoid(y)
    o_ref[...] = y.astype(o_ref.dtype)


def lora(x, idx, a, b, bias, *extras, mid, post, out_dtype, tm=512):
    _, m, d = x.shape
    r = a.shape[1]
    n = b.shape[1]
    tm = _tile(m, tm)
    row = lambda i: (i, 0)
    const = lambda i: (0, 0)
    return pl.pallas_call(
        functools.partial(_lora_kernel, mid=mid, post=post),
        grid=(m // tm,),
        in_specs=[pl.BlockSpec((None, tm, d), lambda i: (idx, i, 0)),
                  pl.BlockSpec((d, r), const), pl.BlockSpec((r, n), const),
                  pl.BlockSpec((1, n), const)] + [pl.BlockSpec((tm, n), row) for _ in extras],
        out_specs=pl.BlockSpec((tm, n), row),
        out_shape=jax.ShapeDtypeStruct((m, n), out_dtype),
        compiler_params=_cparams(("parallel",)),
        name="lora_" + post,
    )(x, a, b, bias, *extras)


def _rwkv_kernel(r_ref, k_ref, v_ref, a_ref, lw_ref, gt_ref, kk_ref, ka_ref, rk_ref, lnw_ref, lnb_ref,
                 o_ref, state_scr, *, chunk, pairs):
    c = chunk
    ps = range(pairs)
    sls = [slice(p * LANES, (p + 1) * LANES) for p in ps]

    @pl.when(pl.program_id(2) == 0)
    def _():
        state_scr[...] = jnp.zeros_like(state_scr)

    lane = lax.broadcasted_iota(jnp.int32, (c, LANES), 1)
    m0 = lane < RWKV_HEAD
    r2 =lax.broadcasted_iota(jnp.int32, (2 * c, 2 * c), 0)
    c2 = lax.broadcasted_iota(jnp.int32, (2 * c, 2 * c), 1)
    strict2 = c2 < r2
    rw = lax.broadcasted_iota(jnp.int32, (c, 2 * c), 0)
    cw = lax.broadcasted_iota(jnp.int32, (c, 2 * c), 1)
    incl_w = jnp.where(cw >= c, cw - c, cw) <= rw
    zero_blk = jnp.zeros((2 * c, LANES), BF16)

    def seg_sum(x):
        s0 = jnp.sum(jnp.where(m0, x, 0.0), axis=-1, keepdims=True)
        s1 = jnp.sum(jnp.where(m0, 0.0, x), axis=-1, keepdims=True)
        return jnp.where(m0, s0, s1)

    def stack(x):
        return jnp.concatenate([jnp.where(m0, x, 0.0), jnp.where(m0, 0.0, x)], axis=0)

    def bf(x):
        return x.astype(BF16)

    r = [r_ref[:, s].astype(F32) for s in sls]
    k = [k_ref[:, s].astype(F32) for s in sls]
    v = [v_ref[:, s].astype(F32) for s in sls]
    a = [a_ref[:, s].astype(F32) for s in sls]
    lw = [lw_ref[:, s] for s in sls]
    kk = [k[p] * kk_ref[:, sls[p]] for p in ps]
    kk = [x * lax.rsqrt(seg_sum(x * x) + 1e-12) for x in kk]
    kh = [k[p] * (1.0 + (a[p] - 1.0) * ka_ref[:, sls[p]]) for p in ps]
    bv = [kk[p] * a[p] for p in ps]

    cum = [_cumsum_rows(x) for x in lw]
    cl = [x[c - 1:c, :] for x in cum]
    e_neg = [jnp.exp(-x) for x in cum]
    e_end = [jnp.exp(cl[p] - cum[p]) for p in ps]
    rg = [r[p] * jnp.exp(cum[p]) for p in ps]
    rgc = [bf(x) for x in rg]
    xs = [bf(stack(kk[p] * jnp.exp(cum[p] - lw[p]))) for p in ps]
    bs = [bf(stack(bv[p] * e_neg[p])) for p in ps]
    ks = [bf(stack(kh[p] * e_neg[p])) for p in ps]
    vs = [bf(stack(v[p])) for p in ps]
    kht = [stack(kh[p] * e_end[p]).T for p in ps]
    bht = [stack(bv[p] * e_end[p]).T for p in ps]

    ab = [_dot_nt(xs[p], jnp.concatenate([bs[p], ks[p]], axis=0)) for p in ps]
    akk = [jnp.where(strict2, x[:, :2 * c], 0.0) for x in ab]
    akv = [bf(jnp.where(strict2, x[:, 2 * c:], 0.0)) for x in ab]
    tinv = [bf(x) for x in _unit_lower_inverse(akk, 6, _dot_bf)]
    w1 = [bf(_dot(akv[p], vs[p])) for p in ps]
    m12 = [bf(_dot(tinv[p], jnp.concatenate([xs[p], w1[p]], axis=1))) for p in ps]
    bb = [_dot_nt(rgc[p], jnp.concatenate([ks[p], bs[p]], axis=0)) for p in ps]
    lhs = [bf(jnp.concatenate([
        jnp.concatenate([jnp.where(incl_w, x[:, :2 * c], 0.0), -jnp.where(incl_w, x[:, 2 * c:], 0.0)], axis=1),
        jnp.concatenate([kht[p], -bht[p]], axis=1)], axis=0)) for p, x in enumerate(bb)]
    rhs = [jnp.concatenate([jnp.concatenate([vs[p], zero_blk], axis=1),
                            jnp.concatenate([m12[p][:, LANES:], m12[p][:, :LANES]], axis=1)], axis=0)
           for p in ps]
    z = [_dot(lhs[p], rhs[p]) for p in ps]
    state = [state_scr[p] for p in ps]
    l2 = [bf(jnp.concatenate([rg[p] + z[p][:c, LANES:], z[p][c:, LANES:]], axis=0)) for p in ps]
    rr = [_dot(l2[p], bf(state[p])) for p in ps]
    for p in ps:
        sl = sls[p]
        state_scr[p] = state[p] * _col_of_row(jnp.exp(cl[p]), LANES) + rr[p][c:] + z[p][c:, :LANES]
        y = rr[p][:c] + z[p][:c, :LANES]
        mu = seg_sum(y) * (1.0 / RWKV_HEAD)
        yc = y - mu
        var = seg_sum(yc * yc) * (1.0 / RWKV_HEAD)
        yn = yc * lax.rsqrt(var + RWKV_LN_EPS) * lnw_ref[:, sl] + lnb_ref[:, sl]
        bonus = seg_sum(r[p] * kh[p] * rk_ref[:, sl]) * v[p]
        o_ref[:, sl] = ((yn + bonus) * gt_ref[:, sl].astype(F32)).astype(o_ref.dtype)


def rwkv_recurrence(r, k, v, a, lw, gate, k_k, k_a, r_k, ln_w, ln_b, batch, seq, *, pairs=16):
    c = RWKV_CHUNK
    m, d = r.shape
    nc = seq // c
    width = pairs * LANES
    blk = pl.BlockSpec((c, width), lambda b, hp, i: (b * nc + i, hp))
    par = pl.BlockSpec((1, width), lambda b, hp, i: (0, hp))
    return pl.pallas_call(
        functools.partial(_rwkv_kernel, chunk=c, pairs=pairs),
        grid=(batch, d // width, nc),
        in_specs=[blk] * 6 + [par] * 5,
        out_specs=blk,
        out_shape=jax.ShapeDtypeStruct((m, d), BF16),
        scratch_shapes=[pltpu.VMEM((pairs, LANES, LANES), F32)],
        compiler_params=_cparams(("parallel", "parallel", "arbitrary")),
        name="rwkv_recurrence",
    )(r, k, v, a, lw, gate, k_k, k_a, r_k, ln_w, ln_b)


def _pad_cols(w, n):
    return jnp.pad(w, ((0, 0), (0, n - w.shape[1])))


def _rot_cols(w):
    half = MLA_ROPE // 2
    return jnp.concatenate([-w[..., half:], w[..., :half]], axis=-1)


def _hybrid_layer(h, ct, st, batch, seq, g_pre, g_post, w_in, q_norm, w_uq, kv_norm, w_ukv, conv_w, a_log,
                  dt_bias, out_norm, w_out):
    d = D_MODEL
    o_cq, o_ckv, o_kr = 0, MLA_Q_LORA, MLA_Q_LORA + MLA_KV_LORA
    o_g = o_kr + MLA_ROPE
    o_sm = o_g + 4 * GDN_HEADS * GDN_DK
    w_kr = w_in[:, o_kr:o_g]
    w_cat = jnp.concatenate([w_in[:, :o_kr], w_in[:, o_g:o_sm], w_kr, _rot_cols(w_kr),
                             _pad_cols(w_in[:, o_sm:], LANES)], axis=1).astype(BF16)
    hcat = norm_matmul(h, g_pre.reshape(1, d), w_cat, tn=768, out_dtype=BF16)

    wq = w_uq.reshape(MLA_Q_LORA, MLA_HEADS, MLA_QK)
    wq_rope = wq[..., MLA_NOPE:]
    wq = jnp.concatenate([wq[..., :MLA_NOPE], wq_rope, _rot_cols(wq_rope)], axis=-1)
    wq = wq.reshape(MLA_Q_LORA, MLA_HEADS * MLA_HEAD_PAD).astype(BF16)
    q, k, v = mla_proj(hcat, q_norm.reshape(1, -1), kv_norm.reshape(1, -1), wq, w_ukv.astype(BF16), ct, st)
    mla_out = mla_attention(q, k, v, batch, seq)
    gdn_out = gdn_mixer(hcat, conv_w, a_log, dt_bias, out_norm.reshape(1, -1), batch, seq)
    return matmul_norm_residual([mla_out, gdn_out], w_out.astype(BF16), g_post.reshape(1, d), h, tn=d)


def _rwkv_layer(h, v_first, batch, seq, g_pre, g_post, mix, w_r, w_k, w_v, w_o, w0, w1, w2, a0, a1, a2,
                g1, g2, k_k, k_a, r_k, ln_w, ln_b, vres):
    d = D_MODEL
    row = lambda t: t.reshape(1, d).astype(F32)
    padc = lambda w: _pad_cols(w, -(-w.shape[1] // LANES) * LANES).astype(BF16)
    padr = lambda w: jnp.pad(w, ((0, -(-w.shape[0] // LANES) * LANES - w.shape[0]), (0, 0))).astype(BF16)
    xs = rwkv_mix(h, g_pre.reshape(1, d), mix, seq)
    r = matmul(xs, 0, w_r.astype(BF16), out_dtype=BF16)
    k = matmul(xs, 2, w_k.astype(BF16), out_dtype=BF16)
    v = matmul(xs, 3, w_v.astype(BF16), out_dtype=BF16)
    lw = lora(xs, 1, padc(w1), padr(w2), row(w0), mid="tanh", post="logdecay", out_dtype=F32)
    a = lora(xs, 4, padc(a1), padr(a2), row(a0), mid="none", post="sigmoid", out_dtype=BF16)
    gate = lora(xs, 5, padc(g1), padr(g2), jnp.zeros((1, d), F32), mid="sigmoid", post="none", out_dtype=BF16)
    if vres is None:
        v_first = v
    else:
        v0, v1, v2 = vres
        v = lora(xs, 3, padc(v1), padr(v2), row(v0), v, v_first, mid="none", post="vres", out_dtype=BF16)
    y = rwkv_recurrence(r, k, v, a, lw, gate, row(k_k), row(k_a), row(r_k), row(ln_w), row(ln_b), batch, seq)
    return matmul_norm_residual([y], w_o.astype(BF16), g_post.reshape(1, d), h, tn=d), v_first


def _ffn_layer(h, g_pre, g_post, w_gate, w_up, w_down):
    d = D_MODEL
    act = ffn_up(h, g_pre.reshape(1, d), w_gate.astype(BF16), w_up.astype(BF16))
    return matmul_norm_residual([act], w_down.astype(BF16), g_post.reshape(1, d), h)


def _rope_tables(positions):
    inv_freq = 1.0 / (ROPE_THETA ** (jnp.arange(0, MLA_ROPE, 2, dtype=F32) / MLA_ROPE))
    ang = positions.astype(F32).reshape(-1, 1) * inv_freq
    zeros = jnp.zeros((ang.shape[0], LANES - MLA_ROPE), F32)
    cos, sin = jnp.cos(ang), jnp.sin(ang)
    return jnp.concatenate([cos, cos, zeros], axis=1), jnp.concatenate([sin, sin, zeros], axis=1)


def kernel(x, positions, norm_mix_pre, norm_mix_post, norm_ffn_pre, norm_ffn_post, hyb_w_in, mla_q_norm, mla_w_uq, mla_kv_norm, mla_w_ukv, gdn_conv_w, gdn_a_log, gdn_dt_bias, gdn_out_norm, hyb_w_out, rwkv_mix, rwkv_w_r, rwkv_w_k, rwkv_w_v, rwkv_w_o, rwkv_w0, rwkv_w1, rwkv_w2, rwkv_a0, rwkv_a1, rwkv_a2, rwkv_g1, rwkv_g2, rwkv_k_k, rwkv_k_a, rwkv_r_k, rwkv_ln_w, rwkv_ln_b, rwkv_v0, rwkv_v1, rwkv_v2, ffn_w_gate, ffn_w_up, ffn_w_down):
    batch, seq, d = x.shape
    h = x.reshape(batch * seq, d)
    ct, st = _rope_tables(positions)
    v_first = None
    for layer in range(DEPTH):
        if layer % 2 == 0:
            e = layer // 2
            h = _hybrid_layer(h, ct, st, batch, seq, norm_mix_pre[layer], norm_mix_post[layer], hyb_w_in[e],
                              mla_q_norm[e], mla_w_uq[e], mla_kv_norm[e], mla_w_ukv[e], gdn_conv_w[e],
                              gdn_a_log[e], gdn_dt_bias[e], gdn_out_norm[e], hyb_w_out[e])
        else:
            o = layer // 2
            vres = None if o == 0 else (rwkv_v0[o - 1], rwkv_v1[o - 1], rwkv_v2[o - 1])
            h, v_first = _rwkv_layer(h, v_first, batch, seq, norm_mix_pre[layer], norm_mix_post[layer],
                                     rwkv_mix[o], rwkv_w_r[o], rwkv_w_k[o], rwkv_w_v[o], rwkv_w_o[o],
                                     rwkv_w0[o], rwkv_w1[o], rwkv_w2[o], rwkv_a0[o], rwkv_a1[o], rwkv_a2[o],
                                     rwkv_g1[o], rwkv_g2[o], rwkv_k_k[o], rwkv_k_a[o], rwkv_r_k[o],
                                     rwkv_ln_w[o], rwkv_ln_b[o], vres)
        h = _ffn_layer(h, norm_ffn_pre[layer], norm_ffn_post[layer], ffn_w_gate[layer], ffn_w_up[layer],
                       ffn_w_down[layer])
    return h.reshape(batch, seq, d)
```

```python
import functools

import jax
import jax.numpy as jnp
import numpy as np
from jax import lax
from jax.experimental import pallas as pl
from jax.experimental.pallas import tpu as pltpu

F32 = jnp.float32
BF16 = jnp.bfloat16
HI = lax.Precision.HIGHEST

D_MODEL = 2048
DEPTH = 4
LANES = 128
VMEM_LIMIT = 56 * 1024 * 1024

MLA_HEADS = 8
MLA_Q_LORA = 512
MLA_KV_LORA = 512
MLA_NOPE = 128
MLA_ROPE = 64
MLA_V = 128
MLA_QK = MLA_NOPE + MLA_ROPE
MLA_HEAD_PAD = 256
ROPE_THETA = 10000.0

GDN_HEADS = 8
GDN_DK = 128
GDN_DV = 128
GDN_CONV = 4
GDN_CHUNK = 128

RWKV_HEAD = 64
RWKV_HEADS = D_MODEL // RWKV_HEAD
RWKV_CHUNK = 64
RWKV_LN_EPS = 64e-5
RMS_EPS = 1e-6

HC_CQ, HC_CKV, HC_GQ, HC_GK, HC_GV, HC_GZ, HC_KR, HC_SM = 0, 512, 1024, 2048, 3072, 4096, 5120, 5248
HC_COLS = 5376


def _cparams(sem):
    return pltpu.CompilerParams(dimension_semantics=sem, vmem_limit_bytes=VMEM_LIMIT)


def _tile(n, pref):
    t = min(n, pref)
    while n % t:
        t -= 8
    return t


def _dot(a, b, prec=None):
    return jnp.dot(a, b, preferred_element_type=F32, precision=prec)


def _dot_nt(a, b, prec=None):
    return lax.dot_general(a, b, (((1,), (1,)), ((), ())), preferred_element_type=F32, precision=prec)


def _rms(x, g, eps=RMS_EPS):
    return x * lax.rsqrt(jnp.mean(x * x, axis=-1, keepdims=True) + eps) * g


def _sigmoid(x):
    return 1.0 / (1.0 + jnp.exp(-x))


def _silu(x):
    return x * _sigmoid(x)


def _softplus(x):
    return jnp.maximum(x, 0.0) + jnp.log(1.0 + jnp.exp(-jnp.abs(x)))


def _norm_mm_kernel(x_ref, g_ref, w_ref, o_ref, xn_ref):
    @pl.when(pl.program_id(1) == 0)
    def _():
        xn_ref[...] = _rms(x_ref[...], g_ref[...]).astype(BF16)

    o_ref[...] = _dot(xn_ref[...], w_ref[...]).astype(o_ref.dtype)


def norm_matmul(x, g, w, *, tn, out_dtype=F32, tm=512):
    m, k = x.shape
    n = w.shape[1]
    tm = _tile(m, tm)
    return pl.pallas_call(
        _norm_mm_kernel,
        grid=(m // tm, n // tn),
        in_specs=[pl.BlockSpec((tm, k), lambda i, j: (i, 0)),
                  pl.BlockSpec((1, k), lambda i, j: (0, 0)),
                  pl.BlockSpec((k, tn), lambda i, j: (0, j))],
        out_specs=pl.BlockSpec((tm, tn), lambda i, j: (i, j)),
        out_shape=jax.ShapeDtypeStruct((m, n), out_dtype),
        scratch_shapes=[pltpu.VMEM((tm, k), BF16)],
        compiler_params=_cparams(("parallel", "arbitrary")),
        name="norm_matmul",
    )(x, g, w)


def _ffn_up_kernel(x_ref, g_ref, wg_ref, wu_ref, o_ref, xn_ref):
    @pl.when(pl.program_id(1) == 0)
    def _():
        xn_ref[...] = _rms(x_ref[...], g_ref[...]).astype(BF16)

    xn = xn_ref[...]
    gate = _dot(xn, wg_ref[...])
    up = _dot(xn, wu_ref[...])
    o_ref[...] = (_silu(gate) * up).astype(o_ref.dtype)


def ffn_up(h, g, wg, wu, *, tm=512, tn=1408):
    m, k = h.shape
    n = wg.shape[1]
    tm = _tile(m, tm)
    tn = _tile(n, tn)
    return pl.pallas_call(
        _ffn_up_kernel,
        grid=(m // tm, n // tn),
        in_specs=[pl.BlockSpec((tm, k), lambda i, j: (i, 0)),
                  pl.BlockSpec((1, k), lambda i, j: (0, 0)),
                  pl.BlockSpec((k, tn), lambda i, j: (0, j)),
                  pl.BlockSpec((k, tn), lambda i, j: (0, j))],
        out_specs=pl.BlockSpec((tm, tn), lambda i, j: (i, j)),
        out_shape=jax.ShapeDtypeStruct((m, n), BF16),
        scratch_shapes=[pltpu.VMEM((tm, k), BF16)],
        compiler_params=_cparams(("parallel", "arbitrary")),
        name="ffn_up",
    )(h, g, wg, wu)


def _mm_norm_res_kernel(*refs, tn, nb, n_a):
    a_refs = refs[:n_a]
    w_ref, g_ref, h_ref, o_ref, f_ref = refs[n_a:]
    j = pl.program_id(1)
    k0 = 0
    f = None
    for a_ref in a_refs:
        part = _dot(a_ref[...], w_ref[k0:k0 + a_ref.shape[1], :])
        f = part if f is None else f + part
        k0 += a_ref.shape[1]
    f_ref[j] = f

    @pl.when(j == nb - 1)
    def _():
        ssq = jnp.sum(f_ref[0] * f_ref[0], axis=-1, keepdims=True)
        for b in range(1, nb):
            ssq = ssq + jnp.sum(f_ref[b] * f_ref[b], axis=-1, keepdims=True)
        inv = lax.rsqrt(ssq * (1.0 / (nb * tn)) + RMS_EPS)
        for b in range(nb):
            cs = slice(b * tn, (b + 1) * tn)
            o_ref[:, cs] = h_ref[:, cs] + f_ref[b] * inv * g_ref[:, cs]


def matmul_norm_residual(a_parts, w, g, h, *, tm=512, tn=512):
    m = a_parts[0].shape[0]
    k, n = w.shape
    tm = _tile(m, tm)
    tn = _tile(n, tn)
    nb = n // tn
    return pl.pallas_call(
        functools.partial(_mm_norm_res_kernel, tn=tn, nb=nb, n_a=len(a_parts)),
        grid=(m // tm, nb),
        in_specs=[pl.BlockSpec((tm, a.shape[1]), lambda i, j: (i, 0)) for a in a_parts] + [
                  pl.BlockSpec((k, tn), lambda i, j: (0, j)),
                  pl.BlockSpec((1, n), lambda i, j: (0, 0)),
                  pl.BlockSpec((tm, n), lambda i, j: (i, 0))],
        out_specs=pl.BlockSpec((tm, n), lambda i, j: (i, 0)),
        out_shape=jax.ShapeDtypeStruct((m, n), F32),
        scratch_shapes=[pltpu.VMEM((nb, tm, tn), F32)],
        compiler_params=_cparams(("parallel", "arbitrary")),
        name="matmul_norm_residual",
    )(*a_parts, w, g, h)


def _mm_kernel(a_ref, w_ref, o_ref):
    o_ref[...] = _dot(a_ref[...], w_ref[...]).astype(o_ref.dtype)


def matmul(a, idx, w, *, tm=1024, tn=1024, out_dtype=F32):
    _, m, k = a.shape
    n = w.shape[1]
    tm = _tile(m, tm)
    tn = _tile(n, tn)
    return pl.pallas_call(
        _mm_kernel,
        grid=(m // tm, n // tn),
        in_specs=[pl.BlockSpec((None, tm, k), lambda i, j: (idx, i, 0)),
                  pl.BlockSpec((k, tn), lambda i, j: (0, j))],
        out_specs=pl.BlockSpec((tm, tn), lambda i, j: (i, j)),
        out_shape=jax.ShapeDtypeStruct((m, n), out_dtype),
        compiler_params=_cparams(("parallel", "arbitrary")),
        name="matmul",
    )(a, w)


def _rope_group(grp, ct, st):
    return grp * ct + pltpu.roll(grp, 64, 1) * st


def _mla_proj_kernel(cq_ref, ckv_ref, kr_ref, qn_ref, kvn_ref, wq_ref, wkv_ref, ct_ref, st_ref,
                     q_ref, k_ref, v_ref):
    scale = MLA_QK ** -0.5 * np.log2(np.e)
    ct = ct_ref[...]
    st = st_ref[...]
    xq = _rms(cq_ref[...].astype(F32), qn_ref[...]).astype(BF16)
    xkv = _rms(ckv_ref[...].astype(F32), kvn_ref[...]).astype(BF16)
    kr = _rope_group(kr_ref[...].astype(F32), ct, st).astype(BF16)
    for h in range(MLA_HEADS):
        c0 = h * MLA_HEAD_PAD
        yq = _dot(xq, wq_ref[:, c0:c0 + MLA_HEAD_PAD])
        q_ref[:, c0:c0 + LANES] = (yq[:, :LANES] * scale).astype(BF16)
        q_ref[:, c0 + LANES:c0 + 2 * LANES] = (_rope_group(yq[:, LANES:], ct, st) * scale).astype(BF16)
        ykv = _dot(xkv, wkv_ref[:, c0:c0 + MLA_HEAD_PAD])
        k_ref[:, c0:c0 + LANES] = ykv[:, :LANES].astype(BF16)
        k_ref[:, c0 + LANES:c0 + 2 * LANES] = kr
        v_ref[:, h * MLA_V:(h + 1) * MLA_V] = ykv[:, LANES:].astype(BF16)


def mla_proj(hcat, q_norm, kv_norm, wq, wkv, ct, st, *, tm=512):
    m = hcat.shape[0]
    tm = _tile(m, tm)
    hp = MLA_HEADS * MLA_HEAD_PAD
    row = lambda i: (i, 0)
    const = lambda i: (0, 0)
    return pl.pallas_call(
        _mla_proj_kernel,
        grid=(m // tm,),
        in_specs=[pl.BlockSpec((tm, MLA_Q_LORA), lambda i: (i, HC_CQ // MLA_Q_LORA)),
                  pl.BlockSpec((tm, MLA_KV_LORA), lambda i: (i, HC_CKV // MLA_KV_LORA)),
                  pl.BlockSpec((tm, LANES), lambda i: (i, HC_KR // LANES)),
                  pl.BlockSpec((1, MLA_Q_LORA), const),
                  pl.BlockSpec((1, MLA_KV_LORA), const),
                  pl.BlockSpec((MLA_Q_LORA, hp), const),
                  pl.BlockSpec((MLA_KV_LORA, hp), const),
                  pl.BlockSpec((tm, LANES), row),
                  pl.BlockSpec((tm, LANES), row)],
        out_specs=[pl.BlockSpec((tm, hp), row),
                   pl.BlockSpec((tm, hp), row),
                   pl.BlockSpec((tm, MLA_HEADS * MLA_V), row)],
        out_shape=[jax.ShapeDtypeStruct((m, hp), BF16),
                   jax.ShapeDtypeStruct((m, hp), BF16),
                   jax.ShapeDtypeStruct((m, MLA_HEADS * MLA_V), BF16)],
        compiler_params=_cparams(("parallel",)),
        name="mla_proj",
    )(hcat, hcat, hcat, q_norm, kv_norm, wq, wkv, ct, st)


def _attn_kernel(q_ref, k_ref, v_ref, o_ref, *, tq, tk, hpg):
    qi = pl.program_id(2)
    hs = range(hpg)
    q = [q_ref[:, h * MLA_HEAD_PAD:(h + 1) * MLA_HEAD_PAD] for h in hs]

    def step(ki, carry, masked):
        start = pl.multiple_of(ki * tk, tk)
        s = [_dot_nt(q[h], k_ref[pl.ds(start, tk), h * MLA_HEAD_PAD:(h + 1) * MLA_HEAD_PAD]) for h in hs]
        if masked is not None:
            row = lax.broadcasted_iota(jnp.int32, (tq, tk), 0)
            col = lax.broadcasted_iota(jnp.int32, (tq, tk), 1) + masked * tk
            s = [jnp.where(col <= row, x, -1e30) for x in s]
        m_new = [jnp.maximum(carry[h][0], jnp.max(s[h], axis=-1, keepdims=True)) for h in hs]
        alpha = [jnp.exp2(carry[h][0] - m_new[h]) for h in hs]
        p = [jnp.exp2(s[h] - m_new[h]) for h in hs]
        l_new = [alpha[h] * carry[h][1] + jnp.sum(p[h], axis=-1, keepdims=True) for h in hs]
        pv = [_dot(p[h].astype(BF16), v_ref[pl.ds(start, tk), h * MLA_V:(h + 1) * MLA_V]) for h in hs]
        return tuple((m_new[h], l_new[h], alpha[h] * carry[h][2] + pv[h]) for h in hs)

    init = tuple((jnp.full((tq, 1), -1e30, F32), jnp.zeros((tq, 1), F32), jnp.zeros((tq, MLA_V), F32))
                 for _ in hs)
    per_q = tq // tk
    carry = lax.fori_loop(0, qi * per_q, lambda ki, c: step(ki, c, None), init)
    for j in range(per_q):
        carry = step(qi * per_q + j, carry, j)
    for h in hs:
        o_ref[:, h * MLA_V:(h + 1) * MLA_V] = (carry[h][2] / carry[h][1]).astype(o_ref.dtype)


def mla_attention(q, k, v, batch, seq, *, tq=1024, tk=1024, hpg=2):
    tq = _tile(seq, tq)
    tk = _tile(tq, tk)
    nq = seq // tq
    return pl.pallas_call(
        functools.partial(_attn_kernel, tq=tq, tk=tk, hpg=hpg),
        grid=(batch, MLA_HEADS // hpg, nq),
        in_specs=[pl.BlockSpec((tq, hpg * MLA_HEAD_PAD), lambda b, h, i: (b * nq + i, h)),
                  pl.BlockSpec((seq, hpg * MLA_HEAD_PAD), lambda b, h, i: (b, h)),
                  pl.BlockSpec((seq, hpg * MLA_V), lambda b, h, i: (b, h))],
        out_specs=pl.BlockSpec((tq, hpg * MLA_V), lambda b, h, i: (b * nq + i, h)),
        out_shape=jax.ShapeDtypeStruct((batch * seq, MLA_HEADS * MLA_V), BF16),
        compiler_params=_cparams(("parallel", "parallel", "arbitrary")),
        name="mla_attention",
    )(q, k, v)


def _split2(x):
    hi = x.astype(BF16)
    return hi, (x - hi.astype(F32)).astype(BF16)


def _dot_x3(a, b):
    ah, al = _split2(a)
    bh, bl = _split2(b)
    return _dot(jnp.concatenate([ah, al, ah], axis=1), jnp.concatenate([bh, bh, bl], axis=0))


def _dot_nt_x3(a, b):
    ah, al = _split2(a)
    bh, bl = _split2(b)
    return _dot_nt(jnp.concatenate([ah, al, ah], axis=1), jnp.concatenate([bh, bh, bl], axis=1))


def _dot_bf(a, b):
    return _dot(a.astype(BF16), b.astype(BF16))


def _unit_lower_inverse(a_list, levels, dot):
    n = a_list[0].shape[0]
    eye = (lax.broadcasted_iota(jnp.int32, (n, n), 0) == lax.broadcasted_iota(jnp.int32, (n, n), 1)).astype(F32)
    p = [-a for a in a_list]
    r = [eye + x for x in p]
    p = [dot(x, x) for x in p]
    for _ in range(levels - 2):
        s = [dot(jnp.concatenate([ri, pi], axis=0), pi) for ri, pi in zip(r, p)]
        r = [ri + si[:n] for ri, si in zip(r, s)]
        p = [si[n:] for si in s]
    return [ri + dot(ri, pi) for ri, pi in zip(r, p)]


def _cumsum_rows(x):
    c = x.shape[0]
    row = lax.broadcasted_iota(jnp.int32, x.shape, 0)
    s = 1
    while s < c:
        x = x + jnp.where(row >= s, pltpu.roll(x, s, 0), 0.0)
        s *= 2
    return x


def _col_of_row(row_vec, n):
    eye = lax.broadcasted_iota(jnp.int32, (n, n), 0) == lax.broadcasted_iota(jnp.int32, (n, n), 1)
    return jnp.sum(jnp.where(eye, jnp.broadcast_to(row_vec, (n, n)), 0.0), axis=1, keepdims=True)


def _gdn_kernel(gq_ref, gk_ref, gv_ref, gz_ref, sm_ref, cw_ref, alog_ref, dtb_ref, on_ref, o_ref,
                xq_scr, xk_scr, xv_scr, state_scr, *, chunk):
    c = chunk
    heads = range(GDN_HEADS)
    hs = [slice(h * LANES, (h + 1) * LANES) for h in heads]

    @pl.when(pl.program_id(1) == 0)
    def _():
        for scr in (xq_scr, xk_scr, xv_scr):
            scr[0:8, :] = jnp.zeros((8, scr.shape[1]), F32)
        state_scr[...] = jnp.zeros_like(state_scr)

    def conv_silu(x_ref, scr, widx):
        scr[8:8 + c, :] = x_ref[...].astype(F32)
        acc = cw_ref[widx, GDN_CONV - 1:GDN_CONV, :] * scr[8:8 + c, :]
        for j in range(GDN_CONV - 1):
            s = GDN_CONV - 1 - j
            acc = acc + cw_ref[widx, j:j + 1, :] * scr[8 - s:8 - s + c, :]
        scr[0:8, :] = scr[c:c + 8, :]
        return _silu(acc)

    def l2n(x):
        return x * lax.rsqrt(jnp.sum(x * x, axis=-1, keepdims=True) + 1e-6)

    def bf(x):
        return x.astype(BF16)

    qa = conv_silu(gq_ref, xq_scr, 0)
    ka = conv_silu(gk_ref, xk_scr, 1)
    va = conv_silu(gv_ref, xv_scr, 2)
    q = [l2n(qa[:, s]) * (GDN_DK ** -0.5) for s in hs]
    k = [l2n(ka[:, s]) for s in hs]
    v = [va[:, s] for s in hs]

    sm = sm_ref[...].astype(F32)
    lane = lax.broadcasted_iota(jnp.int32, (c, LANES), 1)
    beta = [_sigmoid(jnp.sum(jnp.where(lane == h, sm, 0.0), axis=-1, keepdims=True)) for h in heads]
    ga = [jnp.sum(jnp.where(lane == GDN_HEADS + h, sm, 0.0), axis=-1, keepdims=True) for h in heads]
    g = [-jnp.exp(alog_ref[:, hs[h]]) * _softplus(jnp.broadcast_to(ga[h], (c, LANES)) + dtb_ref[:, hs[h]])
         for h in heads]

    ri = lax.broadcasted_iota(jnp.int32, (c, c), 0)
    cj = lax.broadcasted_iota(jnp.int32, (c, c), 1)
    gam = [_cumsum_rows(x) for x in g]
    decay = [jnp.exp(jnp.where(cj <= ri, x - x.T, -1e30)) for x in gam]
    eg = [jnp.exp(x) for x in gam]
    gl = [x[c - 1:c, :] for x in gam]
    kb = [k[h] * beta[h] for h in heads]
    vb = [v[h] * beta[h] for h in heads]

    a_mat = [jnp.where(cj < ri, _dot_nt_x3(kb[h], k[h]) * decay[h], 0.0) for h in heads]
    tinv = _unit_lower_inverse(a_mat, 7, _dot_x3)
    rhs = [jnp.concatenate([kb[h] * eg[h], vb[h]], axis=1) for h in heads]
    wu = [bf(_dot_x3(tinv[h], rhs[h])) for h in heads]
    attn = [_dot_nt(bf(q[h]), bf(k[h])) * decay[h] for h in heads]
    k_dec = [k[h] * jnp.exp(gl[h] - gam[h]) for h in heads]
    lhs = [bf(jnp.concatenate([attn[h], k_dec[h].T], axis=0)) for h in heads]
    z = [_dot(lhs[h], wu[h]) for h in heads]
    state = [state_scr[h] for h in heads]
    l2 = [bf(jnp.concatenate([q[h] * eg[h] - z[h][:c, :LANES], -z[h][c:, :LANES]], axis=0)) for h in heads]
    rr = [_dot(l2[h], bf(state[h])) for h in heads]
    for h in heads:
        state_scr[h] = state[h] * jnp.exp(gl[h]) + rr[h][c:] + z[h][c:, LANES:]
        o = rr[h][:c] + z[h][:c, LANES:]
        o_ref[:, hs[h]] = (_rms(o, on_ref[...]) * _silu(gz_ref[:, hs[h]].astype(F32))).astype(o_ref.dtype)


def gdn_mixer(hcat, conv_w, a_log, dt_bias, out_norm, batch, seq):
    c = GDN_CHUNK
    assert c == LANES and seq % c == 0
    nc = seq // c
    width = GDN_HEADS * LANES
    blk = lambda off: pl.BlockSpec((c, width), lambda b, i, off=off: (b * nc + i, off // width))
    const2 = lambda b, i: (0, 0)
    cw = conv_w.reshape(GDN_CONV, 3, width).transpose(1, 0, 2)
    alog = jnp.repeat(a_log.astype(F32), LANES).reshape(1, width)
    dtb = jnp.repeat(dt_bias.astype(F32), LANES).reshape(1, width)
    return pl.pallas_call(
        functools.partial(_gdn_kernel, chunk=c),
        grid=(batch, nc),
        in_specs=[blk(HC_GQ), blk(HC_GK), blk(HC_GV), blk(HC_GZ),
                  pl.BlockSpec((c, LANES), lambda b, i: (b * nc + i, HC_SM // LANES)),
                  pl.BlockSpec((3, GDN_CONV, width), lambda b, i: (0, 0, 0)),
                  pl.BlockSpec((1, width), const2),
                  pl.BlockSpec((1, width), const2),
                  pl.BlockSpec((1, GDN_DV), const2)],
        out_specs=pl.BlockSpec((c, width), lambda b, i: (b * nc + i, 0)),
        out_shape=jax.ShapeDtypeStruct((batch * seq, width), BF16),
        scratch_shapes=[pltpu.VMEM((c + 8, width), F32)] * 3 + [pltpu.VMEM((GDN_HEADS, GDN_DK, GDN_DV), F32)],
        compiler_params=_cparams(("parallel", "arbitrary")),
        name="gdn_mixer",
    )(hcat, hcat, hcat, hcat, hcat, cw, alog, dtb, out_norm)


LZ_W, LZ_A, LZ_G, LZ_V, LZ_COLS = 0, 128, 256, 512, 640


def _rwkv_mix_kernel(h_ref, hp_ref, g_ref, mix_ref, a1_ref, o_ref, z_ref, *, tm, seq, has_v):
    i = pl.program_id(0)
    g = g_ref[...]
    xn = _rms(h_ref[...], g)
    prev_last = _rms(hp_ref[...], g)[7:8, :]
    prev_last = jnp.where((i * tm) % seq == 0, 0.0, prev_last)
    rows = lax.broadcasted_iota(jnp.int32, xn.shape, 0)
    xprev = jnp.where(rows == 0, prev_last, pltpu.roll(xn, 1, 0))
    xx = xprev - xn

    def mixed(j):
        return (xn + xx * mix_ref[j:j + 1, :]).astype(BF16)

    o_ref[0] = mixed(0)
    o_ref[1] = mixed(2)
    xv = mixed(3)
    o_ref[2] = xv
    z_ref[:, LZ_W:LZ_A] = jnp.tanh(_dot(mixed(1), a1_ref[:, LZ_W:LZ_A])).astype(BF16)
    z_ref[:, LZ_A:LZ_G] = _dot(mixed(4), a1_ref[:, LZ_A:LZ_G]).astype(BF16)
    z_ref[:, LZ_G:LZ_V] = _sigmoid(_dot(mixed(5), a1_ref[:, LZ_G:LZ_V])).astype(BF16)
    if has_v:
        z_ref[:, LZ_V:] = _dot(xv, a1_ref[:, LZ_V:]).astype(BF16)
    else:
        z_ref[:, LZ_V:] = jnp.zeros((tm, LZ_COLS - LZ_V), BF16)


def rwkv_mix(h, g, mix, a1_cat, seq, *, has_v, tm=256):
    m, d = h.shape
    tm = _tile(seq, tm)
    return pl.pallas_call(
        functools.partial(_rwkv_mix_kernel, tm=tm, seq=seq, has_v=has_v),
        grid=(m // tm,),
        in_specs=[pl.BlockSpec((tm, d), lambda i: (i, 0)),
                  pl.BlockSpec((8, d), lambda i: (jnp.maximum(i * (tm // 8) - 1, 0), 0)),
                  pl.BlockSpec((1, d), lambda i: (0, 0)),
                  pl.BlockSpec((6, d), lambda i: (0, 0)),
                  pl.BlockSpec((d, LZ_COLS), lambda i: (0, 0))],
        out_specs=[pl.BlockSpec((3, tm, d), lambda i: (0, i, 0)),
                   pl.BlockSpec((tm, LZ_COLS), lambda i: (i, 0))],
        out_shape=[jax.ShapeDtypeStruct((3, m, d), BF16),
                   jax.ShapeDtypeStruct((m, LZ_COLS), BF16)],
        compiler_params=_cparams(("parallel",)),
        name="rwkv_mix",
    )(h, h, g, mix, a1_cat)


def _lora_up_kernel(z_ref, b_ref, bias_ref, *rest, has_v):
    if has_v:
        v_ref, vf_ref, lw_ref, a_ref, gt_ref, vo_ref = rest
    else:
        lw_ref, a_ref, gt_ref = rest
    y = _dot(z_ref[:, LZ_W:LZ_A], b_ref[LZ_W:LZ_A, :]) + bias_ref[0:1, :]
    lw_ref[...] = -jnp.exp(-_softplus(-y) - 0.5)
    y = _dot(z_ref[:, LZ_A:LZ_G], b_ref[LZ_A:LZ_G, :]) + bias_ref[1:2, :]
    a_ref[...] = _sigmoid(y).astype(a_ref.dtype)
    gt_ref[...] = _dot(z_ref[:, LZ_G:LZ_V], b_ref[LZ_G:LZ_V, :]).astype(gt_ref.dtype)
    if has_v:
        y = _dot(z_ref[:, LZ_V:], b_ref[LZ_V:, :]) + bias_ref[2:3, :]
        v = v_ref[...].astype(F32)
        vo_ref[...] = (v + (vf_ref[...].astype(F32) - v) * _sigmoid(y)).astype(vo_ref.dtype)


def lora_up(z, b_cat, bias, v=None, v_first=None, *, tm=512):
    m = z.shape[0]
    d = b_cat.shape[1]
    has_v = v is not None
    tm = _tile(m, tm)
    row = lambda i: (i, 0)
    const = lambda i: (0, 0)
    big = pl.BlockSpec((tm, d), row)
    n_bf = 3 if has_v else 2
    return pl.pallas_call(
        functools.partial(_lora_up_kernel, has_v=has_v),
        grid=(m // tm,),
        in_specs=[pl.BlockSpec((tm, LZ_COLS), row), pl.BlockSpec((LZ_COLS, d), const),
                  pl.BlockSpec((3, d), const)] + ([big, big] if has_v else []),
        out_specs=[big] * (1 + n_bf),
        out_shape=[jax.ShapeDtypeStruct((m, d), F32)] + [jax.ShapeDtypeStruct((m, d), BF16)] * n_bf,
        compiler_params=_cparams(("parallel",)),
        name="lora_up",
    )(z, b_cat, bias, *((v, v_first) if has_v else ()))


def _rwkv_kernel(r_ref, k_ref, v_ref, a_ref, lw_ref, gt_ref, kk_ref, ka_ref, rk_ref, lnw_ref, lnb_ref,
                 o_ref, state_scr, *, chunk, pairs):
    c = chunk
    ps = range(pairs)
    sls = [slice(p * LANES, (p + 1) * LANES) for p in ps]

    @pl.when(pl.program_id(2) == 0)
    def _():
        state_scr[...] = jnp.zeros_like(state_scr)

    lane = lax.broadcasted_iota(jnp.int32, (c, LANES), 1)
    m0 = lane < RWKV_HEAD
    r2 =lax.broadcasted_iota(jnp.int32, (2 * c, 2 * c), 0)
    c2 = lax.broadcasted_iota(jnp.int32, (2 * c, 2 * c), 1)
    strict2 = c2 < r2
    rw = lax.broadcasted_iota(jnp.int32, (c, 2 * c), 0)
    cw = lax.broadcasted_iota(jnp.int32, (c, 2 * c), 1)
    incl_w = jnp.where(cw >= c, cw - c, cw) <= rw
    zero_blk = jnp.zeros((2 * c, LANES), BF16)

    def seg_sum(x):
        s0 = jnp.sum(jnp.where(m0, x, 0.0), axis=-1, keepdims=True)
        s1 = jnp.sum(jnp.where(m0, 0.0, x), axis=-1, keepdims=True)
        return jnp.where(m0, s0, s1)

    def stack(x):
        return jnp.concatenate([jnp.where(m0, x, 0.0), jnp.where(m0, 0.0, x)], axis=0)

    def bf(x):
        return x.astype(BF16)

    r = [r_ref[:, s].astype(F32) for s in sls]
    k = [k_ref[:, s].astype(F32) for s in sls]
    v = [v_ref[:, s].astype(F32) for s in sls]
    a = [a_ref[:, s].astype(F32) for s in sls]
    lw = [lw_ref[:, s] for s in sls]
    kk = [k[p] * kk_ref[:, sls[p]] for p in ps]
    kk = [x * lax.rsqrt(seg_sum(x * x) + 1e-12) for x in kk]
    kh = [k[p] * (1.0 + (a[p] - 1.0) * ka_ref[:, sls[p]]) for p in ps]
    bv = [kk[p] * a[p] for p in ps]

    cum = [_cumsum_rows(x) for x in lw]
    cl = [x[c - 1:c, :] for x in cum]
    e_neg = [jnp.exp(-x) for x in cum]
    e_end = [jnp.exp(cl[p] - cum[p]) for p in ps]
    rg = [r[p] * jnp.exp(cum[p]) for p in ps]
    rgc = [bf(x) for x in rg]
    xs = [bf(stack(kk[p] * jnp.exp(cum[p] - lw[p]))) for p in ps]
    bs = [bf(stack(bv[p] * e_neg[p])) for p in ps]
    ks = [bf(stack(kh[p] * e_neg[p])) for p in ps]
    vs = [bf(stack(v[p])) for p in ps]
    kht = [stack(kh[p] * e_end[p]).T for p in ps]
    bht = [stack(bv[p] * e_end[p]).T for p in ps]

    ab = [_dot_nt(xs[p], jnp.concatenate([bs[p], ks[p]], axis=0)) for p in ps]
    akk = [jnp.where(strict2, x[:, :2 * c], 0.0) for x in ab]
    akv = [bf(jnp.where(strict2, x[:, 2 * c:], 0.0)) for x in ab]
    tinv = [bf(x) for x in _unit_lower_inverse(akk, 6, _dot_bf)]
    w1 = [bf(_dot(akv[p], vs[p])) for p in ps]
    m12 = [bf(_dot(tinv[p], jnp.concatenate([xs[p], w1[p]], axis=1))) for p in ps]
    bb = [_dot_nt(rgc[p], jnp.concatenate([ks[p], bs[p]], axis=0)) for p in ps]
    lhs = [bf(jnp.concatenate([
        jnp.concatenate([jnp.where(incl_w, x[:, :2 * c], 0.0), -jnp.where(incl_w, x[:, 2 * c:], 0.0)], axis=1),
        jnp.concatenate([kht[p], -bht[p]], axis=1)], axis=0)) for p, x in enumerate(bb)]
    rhs = [jnp.concatenate([jnp.concatenate([vs[p], zero_blk], axis=1),
                            jnp.concatenate([m12[p][:, LANES:], m12[p][:, :LANES]], axis=1)], axis=0)
           for p in ps]
    z = [_dot(lhs[p], rhs[p]) for p in ps]
    state = [state_scr[p] for p in ps]
    l2 = [bf(jnp.concatenate([rg[p] + z[p][:c, LANES:], z[p][c:, LANES:]], axis=0)) for p in ps]
    rr = [_dot(l2[p], bf(state[p])) for p in ps]
    for p in ps:
        sl = sls[p]
        state_scr[p] = state[p] * _col_of_row(jnp.exp(cl[p]), LANES) + rr[p][c:] + z[p][c:, :LANES]
        y = rr[p][:c] + z[p][:c, :LANES]
        mu = seg_sum(y) * (1.0 / RWKV_HEAD)
        yc = y - mu
        var = seg_sum(yc * yc) * (1.0 / RWKV_HEAD)
        yn = yc * lax.rsqrt(var + RWKV_LN_EPS) * lnw_ref[:, sl] + lnb_ref[:, sl]
        bonus = seg_sum(r[p] * kh[p] * rk_ref[:, sl]) * v[p]
        o_ref[:, sl] = ((yn + bonus) * gt_ref[:, sl].astype(F32)).astype(o_ref.dtype)


def rwkv_recurrence(r, k, v, a, lw, gate, k_k, k_a, r_k, ln_w, ln_b, batch, seq, *, pairs=16):
    c = RWKV_CHUNK
    m, d = r.shape
    nc = seq // c
    width = pairs * LANES
    blk = pl.BlockSpec((c, width), lambda b, hp, i: (b * nc + i, hp))
    par = pl.BlockSpec((1, width), lambda b, hp, i: (0, hp))
    return pl.pallas_call(
        functools.partial(_rwkv_kernel, chunk=c, pairs=pairs),
        grid=(batch, d // width, nc),
        in_specs=[blk] * 6 + [par] * 5,
        out_specs=blk,
        out_shape=jax.ShapeDtypeStruct((m, d), BF16),
        scratch_shapes=[pltpu.VMEM((pairs, LANES, LANES), F32)],
        compiler_params=_cparams(("parallel", "parallel", "arbitrary")),
        name="rwkv_recurrence",
    )(r, k, v, a, lw, gate, k_k, k_a, r_k, ln_w, ln_b)


def _pad_cols(w, n):
    return jnp.pad(w, ((0, 0), (0, n - w.shape[1])))


def _rot_cols(w):
    half = MLA_ROPE // 2
    return jnp.concatenate([-w[..., half:], w[..., :half]], axis=-1)


def _hybrid_layer(h, ct, st, batch, seq, g_pre, g_post, w_in, q_norm, w_uq, kv_norm, w_ukv, conv_w, a_log,
                  dt_bias, out_norm, w_out):
    d = D_MODEL
    o_cq, o_ckv, o_kr = 0, MLA_Q_LORA, MLA_Q_LORA + MLA_KV_LORA
    o_g = o_kr + MLA_ROPE
    o_sm = o_g + 4 * GDN_HEADS * GDN_DK
    w_kr = w_in[:, o_kr:o_g]
    w_cat = jnp.concatenate([w_in[:, :o_kr], w_in[:, o_g:o_sm], w_kr, _rot_cols(w_kr),
                             _pad_cols(w_in[:, o_sm:], LANES)], axis=1).astype(BF16)
    hcat = norm_matmul(h, g_pre.reshape(1, d), w_cat, tn=1792, out_dtype=BF16)

    wq = w_uq.reshape(MLA_Q_LORA, MLA_HEADS, MLA_QK)
    wq_rope = wq[..., MLA_NOPE:]
    wq = jnp.concatenate([wq[..., :MLA_NOPE], wq_rope, _rot_cols(wq_rope)], axis=-1)
    wq = wq.reshape(MLA_Q_LORA, MLA_HEADS * MLA_HEAD_PAD).astype(BF16)
    q, k, v = mla_proj(hcat, q_norm.reshape(1, -1), kv_norm.reshape(1, -1), wq, w_ukv.astype(BF16), ct, st)
    mla_out = mla_attention(q, k, v, batch, seq)
    gdn_out = gdn_mixer(hcat, conv_w, a_log, dt_bias, out_norm.reshape(1, -1), batch, seq)
    return matmul_norm_residual([mla_out, gdn_out], w_out.astype(BF16), g_post.reshape(1, d), h, tn=d)


def _rwkv_layer(h, v_first, batch, seq, g_pre, g_post, mix, w_r, w_k, w_v, w_o, w0, w1, w2, a0, a1, a2,
                g1, g2, k_k, k_a, r_k, ln_w, ln_b, vres):
    d = D_MODEL
    row = lambda t: t.reshape(1, d).astype(F32)
    padc = lambda w, n: _pad_cols(w, n)
    padr = lambda w, n: jnp.pad(w, ((0, n - w.shape[0]), (0, 0)))
    has_v = vres is not None
    v0, v1, v2 = vres if has_v else (jnp.zeros((d,), F32), jnp.zeros((d, 0), F32), jnp.zeros((0, d), F32))
    a1_cat = jnp.concatenate([padc(w1, LZ_A - LZ_W), padc(a1, LZ_G - LZ_A), padc(g1, LZ_V - LZ_G),
                              padc(v1, LZ_COLS - LZ_V)], axis=1).astype(BF16)
    b_cat = jnp.concatenate([padr(w2, LZ_A - LZ_W), padr(a2, LZ_G - LZ_A), padr(g2, LZ_V - LZ_G),
                             padr(v2, LZ_COLS - LZ_V)], axis=0).astype(BF16)
    bias = jnp.stack([w0, a0, v0]).astype(F32)
    xs, z = rwkv_mix(h, g_pre.reshape(1, d), mix, a1_cat, seq, has_v=has_v)
    r = matmul(xs, 0, w_r.astype(BF16), out_dtype=BF16)
    k = matmul(xs, 1, w_k.astype(BF16), out_dtype=BF16)
    v = matmul(xs, 2, w_v.astype(BF16), out_dtype=BF16)
    if has_v:
        lw, a, gate, v = lora_up(z, b_cat, bias, v, v_first)
    else:
        lw, a, gate = lora_up(z, b_cat, bias)
        v_first = v
    y = rwkv_recurrence(r, k, v, a, lw, gate, row(k_k), row(k_a), row(r_k), row(ln_w), row(ln_b), batch, seq)
    return matmul_norm_residual([y], w_o.astype(BF16), g_post.reshape(1, d), h, tn=d), v_first


def _ffn_layer(h, g_pre, g_post, w_gate, w_up, w_down):
    d = D_MODEL
    act = ffn_up(h, g_pre.reshape(1, d), w_gate.astype(BF16), w_up.astype(BF16))
    return matmul_norm_residual([act], w_down.astype(BF16), g_post.reshape(1, d), h)


def _rope_tables(positions):
    inv_freq = 1.0 / (ROPE_THETA ** (jnp.arange(0, MLA_ROPE, 2, dtype=F32) / MLA_ROPE))
    ang = positions.astype(F32).reshape(-1, 1) * inv_freq
    zeros = jnp.zeros((ang.shape[0], LANES - MLA_ROPE), F32)
    cos, sin = jnp.cos(ang), jnp.sin(ang)
    return jnp.concatenate([cos, cos, zeros], axis=1), jnp.concatenate([sin, sin, zeros], axis=1)


def kernel(x, positions, norm_mix_pre, norm_mix_post, norm_ffn_pre, norm_ffn_post, hyb_w_in, mla_q_norm, mla_w_uq, mla_kv_norm, mla_w_ukv, gdn_conv_w, gdn_a_log, gdn_dt_bias, gdn_out_norm, hyb_w_out, rwkv_mix, rwkv_w_r, rwkv_w_k, rwkv_w_v, rwkv_w_o, rwkv_w0, rwkv_w1, rwkv_w2, rwkv_a0, rwkv_a1, rwkv_a2, rwkv_g1, rwkv_g2, rwkv_k_k, rwkv_k_a, rwkv_r_k, rwkv_ln_w, rwkv_ln_b, rwkv_v0, rwkv_v1, rwkv_v2, ffn_w_gate, ffn_w_up, ffn_w_down):
    batch, seq, d = x.shape
    h = x.reshape(batch * seq, d)
    ct, st = _rope_tables(positions)
    v_first = None
    for layer in range(DEPTH):
        if layer % 2 == 0:
            e = layer // 2
            h = _hybrid_layer(h, ct, st, batch, seq, norm_mix_pre[layer], norm_mix_post[layer], hyb_w_in[e],
                              mla_q_norm[e], mla_w_uq[e], mla_kv_norm[e], mla_w_ukv[e], gdn_conv_w[e],
                              gdn_a_log[e], gdn_dt_bias[e], gdn_out_norm[e], hyb_w_out[e])
        else:
            o = layer // 2
            vres = None if o == 0 else (rwkv_v0[o - 1], rwkv_v1[o - 1], rwkv_v2[o - 1])
            h, v_first = _rwkv_layer(h, v_first, batch, seq, norm_mix_pre[layer], norm_mix_post[layer],
                                     rwkv_mix[o], rwkv_w_r[o], rwkv_w_k[o], rwkv_w_v[o], rwkv_w_o[o],
                                     rwkv_w0[o], rwkv_w1[o], rwkv_w2[o], rwkv_a0[o], rwkv_a1[o], rwkv_a2[o],
                                     rwkv_g1[o], rwkv_g2[o], rwkv_k_k[o], rwkv_k_a[o], rwkv_r_k[o],
                                     rwkv_ln_w[o], rwkv_ln_b[o], vres)
        h = _ffn_layer(h, norm_ffn_pre[layer], norm_ffn_post[layer], ffn_w_gate[layer], ffn_w_up[layer],
                       ffn_w_down[layer])
    return h.reshape(batch, seq, d)
```

```python
import functools

import jax
import jax.numpy as jnp
import numpy as np
from jax import lax
from jax.experimental import pallas as pl
from jax.experimental.pallas import tpu as pltpu

F32 = jnp.float32
BF16 = jnp.bfloat16
HI = lax.Precision.HIGHEST

D_MODEL = 2048
DEPTH = 4
LANES = 128
VMEM_LIMIT = 56 * 1024 * 1024

MLA_HEADS = 8
MLA_Q_LORA = 512
MLA_KV_LORA = 512
MLA_NOPE = 128
MLA_ROPE = 64
MLA_V = 128
MLA_QK = MLA_NOPE + MLA_ROPE
MLA_HEAD_PAD = 256
ROPE_THETA = 10000.0

GDN_HEADS = 8
GDN_DK = 128
GDN_DV = 128
GDN_CONV = 4
GDN_CHUNK = 128

RWKV_HEAD = 64
RWKV_HEADS = D_MODEL // RWKV_HEAD
RWKV_CHUNK = 64
RWKV_LN_EPS = 64e-5
RMS_EPS = 1e-6

HC_CQ, HC_CKV, HC_GQ, HC_GK, HC_GV, HC_GZ, HC_KR, HC_SM = 0, 512, 1024, 2048, 3072, 4096, 5120, 5248
HC_COLS = 5376


def _cparams(sem):
    return pltpu.CompilerParams(dimension_semantics=sem, vmem_limit_bytes=VMEM_LIMIT)


def _tile(n, pref):
    t = min(n, pref)
    while n % t:
        t -= 8
    return t


def _dot(a, b, prec=None):
    return jnp.dot(a, b, preferred_element_type=F32, precision=prec)


def _dot_nt(a, b, prec=None):
    return lax.dot_general(a, b, (((1,), (1,)), ((), ())), preferred_element_type=F32, precision=prec)


def _rms(x, g, eps=RMS_EPS):
    return x * lax.rsqrt(jnp.mean(x * x, axis=-1, keepdims=True) + eps) * g


def _sigmoid(x):
    return 1.0 / (1.0 + jnp.exp(-x))


def _silu(x):
    return x * _sigmoid(x)


def _softplus(x):
    return jnp.maximum(x, 0.0) + jnp.log(1.0 + jnp.exp(-jnp.abs(x)))


def _norm_mm_kernel(x_ref, g_ref, w_ref, o_ref, xn_ref):
    @pl.when(pl.program_id(1) == 0)
    def _():
        xn_ref[...] = _rms(x_ref[...], g_ref[...]).astype(BF16)

    o_ref[...] = _dot(xn_ref[...], w_ref[...]).astype(o_ref.dtype)


def norm_matmul(x, g, w, *, tn, out_dtype=F32, tm=512):
    m, k = x.shape
    n = w.shape[1]
    tm = _tile(m, tm)
    return pl.pallas_call(
        _norm_mm_kernel,
        grid=(m // tm, n // tn),
        in_specs=[pl.BlockSpec((tm, k), lambda i, j: (i, 0)),
                  pl.BlockSpec((1, k), lambda i, j: (0, 0)),
                  pl.BlockSpec((k, tn), lambda i, j: (0, j))],
        out_specs=pl.BlockSpec((tm, tn), lambda i, j: (i, j)),
        out_shape=jax.ShapeDtypeStruct((m, n), out_dtype),
        scratch_shapes=[pltpu.VMEM((tm, k), BF16)],
        compiler_params=_cparams(("parallel", "arbitrary")),
        name="norm_matmul",
    )(x, g, w)


def _ffn_up_kernel(x_ref, g_ref, wg_ref, wu_ref, o_ref, xn_ref):
    @pl.when(pl.program_id(1) == 0)
    def _():
        xn_ref[...] = _rms(x_ref[...], g_ref[...]).astype(BF16)

    xn = xn_ref[...]
    gate = _dot(xn, wg_ref[...])
    up = _dot(xn, wu_ref[...])
    o_ref[...] = (_silu(gate) * up).astype(o_ref.dtype)


def ffn_up(h, g, wg, wu, *, tm=512, tn=1408):
    m, k = h.shape
    n = wg.shape[1]
    tm = _tile(m, tm)
    tn = _tile(n, tn)
    return pl.pallas_call(
        _ffn_up_kernel,
        grid=(m // tm, n // tn),
        in_specs=[pl.BlockSpec((tm, k), lambda i, j: (i, 0)),
                  pl.BlockSpec((1, k), lambda i, j: (0, 0)),
                  pl.BlockSpec((k, tn), lambda i, j: (0, j)),
                  pl.BlockSpec((k, tn), lambda i, j: (0, j))],
        out_specs=pl.BlockSpec((tm, tn), lambda i, j: (i, j)),
        out_shape=jax.ShapeDtypeStruct((m, n), BF16),
        scratch_shapes=[pltpu.VMEM((tm, k), BF16)],
        compiler_params=_cparams(("parallel", "arbitrary")),
        name="ffn_up",
    )(h, g, wg, wu)


def _mm_norm_res_kernel(*refs, tn, nb, n_a):
    a_refs = refs[:n_a]
    w_ref, g_ref, h_ref, o_ref, f_ref = refs[n_a:]
    j = pl.program_id(1)
    k0 = 0
    f = None
    for a_ref in a_refs:
        part = _dot(a_ref[...], w_ref[k0:k0 + a_ref.shape[1], :])
        f = part if f is None else f + part
        k0 += a_ref.shape[1]
    f_ref[j] = f

    @pl.when(j == nb - 1)
    def _():
        ssq = jnp.sum(f_ref[0] * f_ref[0], axis=-1, keepdims=True)
        for b in range(1, nb):
            ssq = ssq + jnp.sum(f_ref[b] * f_ref[b], axis=-1, keepdims=True)
        inv = lax.rsqrt(ssq * (1.0 / (nb * tn)) + RMS_EPS)
        for b in range(nb):
            cs = slice(b * tn, (b + 1) * tn)
            o_ref[:, cs] = h_ref[:, cs] + f_ref[b] * inv * g_ref[:, cs]


def matmul_norm_residual(a_parts, w, g, h, *, tm=512, tn=512):
    m = a_parts[0].shape[0]
    k, n = w.shape
    tm = _tile(m, tm)
    tn = _tile(n, tn)
    nb = n // tn
    return pl.pallas_call(
        functools.partial(_mm_norm_res_kernel, tn=tn, nb=nb, n_a=len(a_parts)),
        grid=(m // tm, nb),
        in_specs=[pl.BlockSpec((tm, a.shape[1]), lambda i, j: (i, 0)) for a in a_parts] + [
                  pl.BlockSpec((k, tn), lambda i, j: (0, j)),
                  pl.BlockSpec((1, n), lambda i, j: (0, 0)),
                  pl.BlockSpec((tm, n), lambda i, j: (i, 0))],
        out_specs=pl.BlockSpec((tm, n), lambda i, j: (i, 0)),
        out_shape=jax.ShapeDtypeStruct((m, n), F32),
        scratch_shapes=[pltpu.VMEM((nb, tm, tn), F32)],
        compiler_params=_cparams(("parallel", "arbitrary")),
        name="matmul_norm_residual",
    )(*a_parts, w, g, h)


def _mm_kernel(a_ref, w_ref, o_ref):
    o_ref[...] = _dot(a_ref[...], w_ref[...]).astype(o_ref.dtype)


def matmul(a, idx, w, *, tm=1024, tn=1024, out_dtype=F32):
    _, m, k = a.shape
    n = w.shape[1]
    tm = _tile(m, tm)
    tn = _tile(n, tn)
    return pl.pallas_call(
        _mm_kernel,
        grid=(m // tm, n // tn),
        in_specs=[pl.BlockSpec((None, tm, k), lambda i, j: (idx, i, 0)),
                  pl.BlockSpec((k, tn), lambda i, j: (0, j))],
        out_specs=pl.BlockSpec((tm, tn), lambda i, j: (i, j)),
        out_shape=jax.ShapeDtypeStruct((m, n), out_dtype),
        compiler_params=_cparams(("parallel", "arbitrary")),
        name="matmul",
    )(a, w)


def _rope_group(grp, ct, st):
    return grp * ct + pltpu.roll(grp, 64, 1) * st


def _mla_proj_kernel(cq_ref, ckv_ref, kr_ref, qn_ref, kvn_ref, wq_ref, wkv_ref, ct_ref, st_ref,
                     q_ref, k_ref, v_ref):
    scale = MLA_QK ** -0.5 * np.log2(np.e)
    ct = ct_ref[...]
    st = st_ref[...]
    xq = _rms(cq_ref[...].astype(F32), qn_ref[...]).astype(BF16)
    xkv = _rms(ckv_ref[...].astype(F32), kvn_ref[...]).astype(BF16)
    kr = _rope_group(kr_ref[...].astype(F32), ct, st).astype(BF16)
    for h in range(MLA_HEADS):
        c0 = h * MLA_HEAD_PAD
        yq = _dot(xq, wq_ref[:, c0:c0 + MLA_HEAD_PAD])
        q_ref[:, c0:c0 + LANES] = (yq[:, :LANES] * scale).astype(BF16)
        q_ref[:, c0 + LANES:c0 + 2 * LANES] = (_rope_group(yq[:, LANES:], ct, st) * scale).astype(BF16)
        ykv = _dot(xkv, wkv_ref[:, c0:c0 + MLA_HEAD_PAD])
        k_ref[:, c0:c0 + LANES] = ykv[:, :LANES].astype(BF16)
        k_ref[:, c0 + LANES:c0 + 2 * LANES] = kr
        v_ref[:, h * MLA_V:(h + 1) * MLA_V] = ykv[:, LANES:].astype(BF16)


def mla_proj(hcat, q_norm, kv_norm, wq, wkv, ct, st, *, tm=1024):
    m = hcat.shape[0]
    tm = _tile(m, tm)
    hp = MLA_HEADS * MLA_HEAD_PAD
    row = lambda i: (i, 0)
    const = lambda i: (0, 0)
    return pl.pallas_call(
        _mla_proj_kernel,
        grid=(m // tm,),
        in_specs=[pl.BlockSpec((tm, MLA_Q_LORA), lambda i: (i, HC_CQ // MLA_Q_LORA)),
                  pl.BlockSpec((tm, MLA_KV_LORA), lambda i: (i, HC_CKV // MLA_KV_LORA)),
                  pl.BlockSpec((tm, LANES), lambda i: (i, HC_KR // LANES)),
                  pl.BlockSpec((1, MLA_Q_LORA), const),
                  pl.BlockSpec((1, MLA_KV_LORA), const),
                  pl.BlockSpec((MLA_Q_LORA, hp), const),
                  pl.BlockSpec((MLA_KV_LORA, hp), const),
                  pl.BlockSpec((tm, LANES), row),
                  pl.BlockSpec((tm, LANES), row)],
        out_specs=[pl.BlockSpec((tm, hp), row),
                   pl.BlockSpec((tm, hp), row),
                   pl.BlockSpec((tm, MLA_HEADS * MLA_V), row)],
        out_shape=[jax.ShapeDtypeStruct((m, hp), BF16),
                   jax.ShapeDtypeStruct((m, hp), BF16),
                   jax.ShapeDtypeStruct((m, MLA_HEADS * MLA_V), BF16)],
        compiler_params=_cparams(("parallel",)),
        name="mla_proj",
    )(hcat, hcat, hcat, q_norm, kv_norm, wq, wkv, ct, st)


def _attn_kernel(q_ref, k_ref, v_ref, o_ref, *, tq, tk, hpg):
    qi = pl.program_id(2)
    hs = range(hpg)
    q = [q_ref[:, h * MLA_HEAD_PAD:(h + 1) * MLA_HEAD_PAD] for h in hs]

    def step(ki, carry, masked):
        start = pl.multiple_of(ki * tk, tk)
        s = [_dot_nt(q[h], k_ref[pl.ds(start, tk), h * MLA_HEAD_PAD:(h + 1) * MLA_HEAD_PAD]) for h in hs]
        if masked is not None:
            row = lax.broadcasted_iota(jnp.int32, (tq, tk), 0)
            col = lax.broadcasted_iota(jnp.int32, (tq, tk), 1) + masked * tk
            s = [jnp.where(col <= row, x, -1e30) for x in s]
        m_new = [jnp.maximum(carry[h][0], jnp.max(s[h], axis=-1, keepdims=True)) for h in hs]
        alpha = [jnp.exp2(carry[h][0] - m_new[h]) for h in hs]
        p = [jnp.exp2(s[h] - m_new[h]) for h in hs]
        l_new = [alpha[h] * carry[h][1] + jnp.sum(p[h], axis=-1, keepdims=True) for h in hs]
        pv = [_dot(p[h].astype(BF16), v_ref[pl.ds(start, tk), h * MLA_V:(h + 1) * MLA_V]) for h in hs]
        return tuple((m_new[h], l_new[h], alpha[h] * carry[h][2] + pv[h]) for h in hs)

    init = tuple((jnp.full((tq, 1), -1e30, F32), jnp.zeros((tq, 1), F32), jnp.zeros((tq, MLA_V), F32))
                 for _ in hs)
    per_q = tq // tk
    carry = lax.fori_loop(0, qi * per_q, lambda ki, c: step(ki, c, None), init)
    for j in range(per_q):
        carry = step(qi * per_q + j, carry, j)
    for h in hs:
        o_ref[:, h * MLA_V:(h + 1) * MLA_V] = (carry[h][2] / carry[h][1]).astype(o_ref.dtype)


def mla_attention(q, k, v, batch, seq, *, tq=1024, tk=1024, hpg=2):
    tq = _tile(seq, tq)
    tk = _tile(tq, tk)
    nq = seq // tq
    return pl.pallas_call(
        functools.partial(_attn_kernel, tq=tq, tk=tk, hpg=hpg),
        grid=(batch, MLA_HEADS // hpg, nq),
        in_specs=[pl.BlockSpec((tq, hpg * MLA_HEAD_PAD), lambda b, h, i: (b * nq + i, h)),
                  pl.BlockSpec((seq, hpg * MLA_HEAD_PAD), lambda b, h, i: (b, h)),
                  pl.BlockSpec((seq, hpg * MLA_V), lambda b, h, i: (b, h))],
        out_specs=pl.BlockSpec((tq, hpg * MLA_V), lambda b, h, i: (b * nq + i, h)),
        out_shape=jax.ShapeDtypeStruct((batch * seq, MLA_HEADS * MLA_V), BF16),
        compiler_params=_cparams(("parallel", "parallel", "arbitrary")),
        name="mla_attention",
    )(q, k, v)


def _split2(x):
    hi = x.astype(BF16)
    return hi, (x - hi.astype(F32)).astype(BF16)


def _dot_x3(a, b):
    ah, al = _split2(a)
    bh, bl = _split2(b)
    return _dot(jnp.concatenate([ah, al, ah], axis=1), jnp.concatenate([bh, bh, bl], axis=0))


def _dot_nt_x3(a, b):
    ah, al = _split2(a)
    bh, bl = _split2(b)
    return _dot_nt(jnp.concatenate([ah, al, ah], axis=1), jnp.concatenate([bh, bh, bl], axis=1))


def _dot_bf(a, b):
    return _dot(a.astype(BF16), b.astype(BF16))


def _unit_lower_inverse(a_list, levels, dot):
    n = a_list[0].shape[0]
    eye = (lax.broadcasted_iota(jnp.int32, (n, n), 0) == lax.broadcasted_iota(jnp.int32, (n, n), 1)).astype(F32)
    p = [-a for a in a_list]
    r = [eye + x for x in p]
    p = [dot(x, x) for x in p]
    for _ in range(levels - 2):
        s = [dot(jnp.concatenate([ri, pi], axis=0), pi) for ri, pi in zip(r, p)]
        r = [ri + si[:n] for ri, si in zip(r, s)]
        p = [si[n:] for si in s]
    return [ri + dot(ri, pi) for ri, pi in zip(r, p)]


def _cumsum_rows(x):
    c = x.shape[0]
    row = lax.broadcasted_iota(jnp.int32, x.shape, 0)
    s = 1
    while s < c:
        x = x + jnp.where(row >= s, pltpu.roll(x, s, 0), 0.0)
        s *= 2
    return x


def _col_of_row(row_vec, n):
    eye = lax.broadcasted_iota(jnp.int32, (n, n), 0) == lax.broadcasted_iota(jnp.int32, (n, n), 1)
    return jnp.sum(jnp.where(eye, jnp.broadcast_to(row_vec, (n, n)), 0.0), axis=1, keepdims=True)


def _gdn_kernel(gq_ref, gk_ref, gv_ref, gz_ref, sm_ref, cw_ref, alog_ref, dtb_ref, on_ref, o_ref,
                xq_scr, xk_scr, xv_scr, state_scr, *, chunk):
    c = chunk
    heads = range(GDN_HEADS)
    hs = [slice(h * LANES, (h + 1) * LANES) for h in heads]

    @pl.when(pl.program_id(1) == 0)
    def _():
        for scr in (xq_scr, xk_scr, xv_scr):
            scr[0:8, :] = jnp.zeros((8, scr.shape[1]), F32)
        state_scr[...] = jnp.zeros_like(state_scr)

    def conv_silu(x_ref, scr, widx):
        scr[8:8 + c, :] = x_ref[...].astype(F32)
        acc = cw_ref[widx, GDN_CONV - 1:GDN_CONV, :] * scr[8:8 + c, :]
        for j in range(GDN_CONV - 1):
            s = GDN_CONV - 1 - j
            acc = acc + cw_ref[widx, j:j + 1, :] * scr[8 - s:8 - s + c, :]
        scr[0:8, :] = scr[c:c + 8, :]
        return _silu(acc)

    def l2n(x):
        return x * lax.rsqrt(jnp.sum(x * x, axis=-1, keepdims=True) + 1e-6)

    def bf(x):
        return x.astype(BF16)

    qa = conv_silu(gq_ref, xq_scr, 0)
    ka = conv_silu(gk_ref, xk_scr, 1)
    va = conv_silu(gv_ref, xv_scr, 2)
    q = [l2n(qa[:, s]) * (GDN_DK ** -0.5) for s in hs]
    k = [l2n(ka[:, s]) for s in hs]
    v = [va[:, s] for s in hs]

    sm = sm_ref[...].astype(F32)
    lane = lax.broadcasted_iota(jnp.int32, (c, LANES), 1)
    beta = [_sigmoid(jnp.sum(jnp.where(lane == h, sm, 0.0), axis=-1, keepdims=True)) for h in heads]
    ga = [jnp.sum(jnp.where(lane == GDN_HEADS + h, sm, 0.0), axis=-1, keepdims=True) for h in heads]
    g = [-jnp.exp(alog_ref[:, hs[h]]) * _softplus(jnp.broadcast_to(ga[h], (c, LANES)) + dtb_ref[:, hs[h]])
         for h in heads]

    ri = lax.broadcasted_iota(jnp.int32, (c, c), 0)
    cj = lax.broadcasted_iota(jnp.int32, (c, c), 1)
    gam = [_cumsum_rows(x) for x in g]
    decay = [jnp.exp(jnp.where(cj <= ri, x - x.T, -1e30)) for x in gam]
    eg = [jnp.exp(x) for x in gam]
    gl = [x[c - 1:c, :] for x in gam]
    kb = [k[h] * beta[h] for h in heads]
    vb = [v[h] * beta[h] for h in heads]

    a_mat = [jnp.where(cj < ri, _dot_nt_x3(kb[h], k[h]) * decay[h], 0.0) for h in heads]
    tinv = _unit_lower_inverse(a_mat, 7, _dot_x3)
    rhs = [jnp.concatenate([kb[h] * eg[h], vb[h]], axis=1) for h in heads]
    wu = [bf(_dot_x3(tinv[h], rhs[h])) for h in heads]
    attn = [_dot_nt(bf(q[h]), bf(k[h])) * decay[h] for h in heads]
    k_dec = [k[h] * jnp.exp(gl[h] - gam[h]) for h in heads]
    lhs = [bf(jnp.concatenate([attn[h], k_dec[h].T], axis=0)) for h in heads]
    z = [_dot(lhs[h], wu[h]) for h in heads]
    state = [state_scr[h] for h in heads]
    l2 = [bf(jnp.concatenate([q[h] * eg[h] - z[h][:c, :LANES], -z[h][c:, :LANES]], axis=0)) for h in heads]
    rr = [_dot(l2[h], bf(state[h])) for h in heads]
    for h in heads:
        state_scr[h] = state[h] * jnp.exp(gl[h]) + rr[h][c:] + z[h][c:, LANES:]
        o = rr[h][:c] + z[h][:c, LANES:]
        o_ref[:, hs[h]] = (_rms(o, on_ref[...]) * _silu(gz_ref[:, hs[h]].astype(F32))).astype(o_ref.dtype)


def gdn_mixer(hcat, conv_w, a_log, dt_bias, out_norm, batch, seq):
    c = GDN_CHUNK
    assert c == LANES and seq % c == 0
    nc = seq // c
    width = GDN_HEADS * LANES
    blk = lambda off: pl.BlockSpec((c, width), lambda b, i, off=off: (b * nc + i, off // width))
    const2 = lambda b, i: (0, 0)
    cw = conv_w.reshape(GDN_CONV, 3, width).transpose(1, 0, 2)
    alog = jnp.repeat(a_log.astype(F32), LANES).reshape(1, width)
    dtb = jnp.repeat(dt_bias.astype(F32), LANES).reshape(1, width)
    return pl.pallas_call(
        functools.partial(_gdn_kernel, chunk=c),
        grid=(batch, nc),
        in_specs=[blk(HC_GQ), blk(HC_GK), blk(HC_GV), blk(HC_GZ),
                  pl.BlockSpec((c, LANES), lambda b, i: (b * nc + i, HC_SM // LANES)),
                  pl.BlockSpec((3, GDN_CONV, width), lambda b, i: (0, 0, 0)),
                  pl.BlockSpec((1, width), const2),
                  pl.BlockSpec((1, width), const2),
                  pl.BlockSpec((1, GDN_DV), const2)],
        out_specs=pl.BlockSpec((c, width), lambda b, i: (b * nc + i, 0)),
        out_shape=jax.ShapeDtypeStruct((batch * seq, width), BF16),
        scratch_shapes=[pltpu.VMEM((c + 8, width), F32)] * 3 + [pltpu.VMEM((GDN_HEADS, GDN_DK, GDN_DV), F32)],
        compiler_params=_cparams(("parallel", "arbitrary")),
        name="gdn_mixer",
    )(hcat, hcat, hcat, hcat, hcat, cw, alog, dtb, out_norm)


LZ_W, LZ_A, LZ_G, LZ_V, LZ_COLS = 0, 128, 256, 512, 640


def _rwkv_mix_kernel(h_ref, hp_ref, g_ref, mix_ref, a1_ref, o_ref, z_ref, *, tm, seq, has_v):
    i = pl.program_id(0)
    g = g_ref[...]
    xn = _rms(h_ref[...], g)
    prev_last = _rms(hp_ref[...], g)[7:8, :]
    prev_last = jnp.where((i * tm) % seq == 0, 0.0, prev_last)
    rows = lax.broadcasted_iota(jnp.int32, xn.shape, 0)
    xprev = jnp.where(rows == 0, prev_last, pltpu.roll(xn, 1, 0))
    xx = xprev - xn

    def mixed(j):
        return (xn + xx * mix_ref[j:j + 1, :]).astype(BF16)

    o_ref[0] = mixed(0)
    o_ref[1] = mixed(2)
    xv = mixed(3)
    o_ref[2] = xv
    z_ref[:, LZ_W:LZ_A] = jnp.tanh(_dot(mixed(1), a1_ref[:, LZ_W:LZ_A])).astype(BF16)
    z_ref[:, LZ_A:LZ_G] = _dot(mixed(4), a1_ref[:, LZ_A:LZ_G]).astype(BF16)
    z_ref[:, LZ_G:LZ_V] = _sigmoid(_dot(mixed(5), a1_ref[:, LZ_G:LZ_V])).astype(BF16)
    if has_v:
        z_ref[:, LZ_V:] = _dot(xv, a1_ref[:, LZ_V:]).astype(BF16)
    else:
        z_ref[:, LZ_V:] = jnp.zeros((tm, LZ_COLS - LZ_V), BF16)


def rwkv_mix(h, g, mix, a1_cat, seq, *, has_v, tm=512):
    m, d = h.shape
    tm = _tile(seq, tm)
    return pl.pallas_call(
        functools.partial(_rwkv_mix_kernel, tm=tm, seq=seq, has_v=has_v),
        grid=(m // tm,),
        in_specs=[pl.BlockSpec((tm, d), lambda i: (i, 0)),
                  pl.BlockSpec((8, d), lambda i: (jnp.maximum(i * (tm // 8) - 1, 0), 0)),
                  pl.BlockSpec((1, d), lambda i: (0, 0)),
                  pl.BlockSpec((6, d), lambda i: (0, 0)),
                  pl.BlockSpec((d, LZ_COLS), lambda i: (0, 0))],
        out_specs=[pl.BlockSpec((3, tm, d), lambda i: (0, i, 0)),
                   pl.BlockSpec((tm, LZ_COLS), lambda i: (i, 0))],
        out_shape=[jax.ShapeDtypeStruct((3, m, d), BF16),
                   jax.ShapeDtypeStruct((m, LZ_COLS), BF16)],
        compiler_params=_cparams(("parallel",)),
        name="rwkv_mix",
    )(h, h, g, mix, a1_cat)


def _lora_up_kernel(z_ref, b_ref, bias_ref, *rest, has_v):
    if has_v:
        v_ref, vf_ref, lw_ref, a_ref, gt_ref, vo_ref = rest
    else:
        lw_ref, a_ref, gt_ref = rest
    y = _dot(z_ref[:, LZ_W:LZ_A], b_ref[LZ_W:LZ_A, :]) + bias_ref[0:1, :]
    lw_ref[...] = -jnp.exp(-_softplus(-y) - 0.5)
    y = _dot(z_ref[:, LZ_A:LZ_G], b_ref[LZ_A:LZ_G, :]) + bias_ref[1:2, :]
    a_ref[...] = _sigmoid(y).astype(a_ref.dtype)
    gt_ref[...] = _dot(z_ref[:, LZ_G:LZ_V], b_ref[LZ_G:LZ_V, :]).astype(gt_ref.dtype)
    if has_v:
        y = _dot(z_ref[:, LZ_V:], b_ref[LZ_V:, :]) + bias_ref[2:3, :]
        v = v_ref[...].astype(F32)
        vo_ref[...] = (v + (vf_ref[...].astype(F32) - v) * _sigmoid(y)).astype(vo_ref.dtype)


def lora_up(z, b_cat, bias, v=None, v_first=None, *, tm=512):
    m = z.shape[0]
    d = b_cat.shape[1]
    has_v = v is not None
    tm = _tile(m, tm)
    row = lambda i: (i, 0)
    const = lambda i: (0, 0)
    big = pl.BlockSpec((tm, d), row)
    n_bf = 3 if has_v else 2
    return pl.pallas_call(
        functools.partial(_lora_up_kernel, has_v=has_v),
        grid=(m // tm,),
        in_specs=[pl.BlockSpec((tm, LZ_COLS), row), pl.BlockSpec((LZ_COLS, d), const),
                  pl.BlockSpec((3, d), const)] + ([big, big] if has_v else []),
        out_specs=[big] * (1 + n_bf),
        out_shape=[jax.ShapeDtypeStruct((m, d), F32)] + [jax.ShapeDtypeStruct((m, d), BF16)] * n_bf,
        compiler_params=_cparams(("parallel",)),
        name="lora_up",
    )(z, b_cat, bias, *((v, v_first) if has_v else ()))


def _rwkv_kernel(r_ref, k_ref, v_ref, a_ref, lw_ref, gt_ref, kk_ref, ka_ref, rk_ref, lnw_ref, lnb_ref,
                 o_ref, state_scr, *, chunk, pairs):
    c = chunk
    ps = range(pairs)
    sls = [slice(p * LANES, (p + 1) * LANES) for p in ps]

    @pl.when(pl.program_id(2) == 0)
    def _():
        state_scr[...] = jnp.zeros_like(state_scr)

    lane = lax.broadcasted_iota(jnp.int32, (c, LANES), 1)
    m0 = lane < RWKV_HEAD
    r2 =lax.broadcasted_iota(jnp.int32, (2 * c, 2 * c), 0)
    c2 = lax.broadcasted_iota(jnp.int32, (2 * c, 2 * c), 1)
    strict2 = c2 < r2
    rw = lax.broadcasted_iota(jnp.int32, (c, 2 * c), 0)
    cw = lax.broadcasted_iota(jnp.int32, (c, 2 * c), 1)
    incl_w = jnp.where(cw >= c, cw - c, cw) <= rw
    zero_blk = jnp.zeros((2 * c, LANES), BF16)

    def seg_sum(x):
        s0 = jnp.sum(jnp.where(m0, x, 0.0), axis=-1, keepdims=True)
        s1 = jnp.sum(jnp.where(m0, 0.0, x), axis=-1, keepdims=True)
        return jnp.where(m0, s0, s1)

    def stack(x):
        return jnp.concatenate([jnp.where(m0, x, 0.0), jnp.where(m0, 0.0, x)], axis=0)

    def bf(x):
        return x.astype(BF16)

    r = [r_ref[:, s].astype(F32) for s in sls]
    k = [k_ref[:, s].astype(F32) for s in sls]
    v = [v_ref[:, s].astype(F32) for s in sls]
    a = [a_ref[:, s].astype(F32) for s in sls]
    lw = [lw_ref[:, s] for s in sls]
    kk = [k[p] * kk_ref[:, sls[p]] for p in ps]
    kk = [x * lax.rsqrt(seg_sum(x * x) + 1e-12) for x in kk]
    kh = [k[p] * (1.0 + (a[p] - 1.0) * ka_ref[:, sls[p]]) for p in ps]
    bv = [kk[p] * a[p] for p in ps]

    cum = [_cumsum_rows(x) for x in lw]
    cl = [x[c - 1:c, :] for x in cum]
    e_neg = [jnp.exp(-x) for x in cum]
    e_end = [jnp.exp(cl[p] - cum[p]) for p in ps]
    rg = [r[p] * jnp.exp(cum[p]) for p in ps]
    rgc = [bf(x) for x in rg]
    xs = [bf(stack(kk[p] * jnp.exp(cum[p] - lw[p]))) for p in ps]
    bs = [bf(stack(bv[p] * e_neg[p])) for p in ps]
    ks = [bf(stack(kh[p] * e_neg[p])) for p in ps]
    vs = [bf(stack(v[p])) for p in ps]
    kht = [stack(kh[p] * e_end[p]).T for p in ps]
    bht = [stack(bv[p] * e_end[p]).T for p in ps]

    ab = [_dot_nt(xs[p], jnp.concatenate([bs[p], ks[p]], axis=0)) for p in ps]
    akk = [jnp.where(strict2, x[:, :2 * c], 0.0) for x in ab]
    akv = [bf(jnp.where(strict2, x[:, 2 * c:], 0.0)) for x in ab]
    tinv = [bf(x) for x in _unit_lower_inverse(akk, 6, _dot_bf)]
    w1 = [bf(_dot(akv[p], vs[p])) for p in ps]
    m12 = [bf(_dot(tinv[p], jnp.concatenate([xs[p], w1[p]], axis=1))) for p in ps]
    bb = [_dot_nt(rgc[p], jnp.concatenate([ks[p], bs[p]], axis=0)) for p in ps]
    lhs = [bf(jnp.concatenate([
        jnp.concatenate([jnp.where(incl_w, x[:, :2 * c], 0.0), -jnp.where(incl_w, x[:, 2 * c:], 0.0)], axis=1),
        jnp.concatenate([kht[p], -bht[p]], axis=1)], axis=0)) for p, x in enumerate(bb)]
    rhs = [jnp.concatenate([jnp.concatenate([vs[p], zero_blk], axis=1),
                            jnp.concatenate([m12[p][:, LANES:], m12[p][:, :LANES]], axis=1)], axis=0)
           for p in ps]
    z = [_dot(lhs[p], rhs[p]) for p in ps]
    state = [state_scr[p] for p in ps]
    l2 = [bf(jnp.concatenate([rg[p] + z[p][:c, LANES:], z[p][c:, LANES:]], axis=0)) for p in ps]
    rr = [_dot(l2[p], bf(state[p])) for p in ps]
    for p in ps:
        sl = sls[p]
        state_scr[p] = state[p] * _col_of_row(jnp.exp(cl[p]), LANES) + rr[p][c:] + z[p][c:, :LANES]
        y = rr[p][:c] + z[p][:c, :LANES]
        mu = seg_sum(y) * (1.0 / RWKV_HEAD)
        yc = y - mu
        var = seg_sum(yc * yc) * (1.0 / RWKV_HEAD)
        yn = yc * lax.rsqrt(var + RWKV_LN_EPS) * lnw_ref[:, sl] + lnb_ref[:, sl]
        bonus = seg_sum(r[p] * kh[p] * rk_ref[:, sl]) * v[p]
        o_ref[:, sl] = ((yn + bonus) * gt_ref[:, sl].astype(F32)).astype(o_ref.dtype)


def rwkv_recurrence(r, k, v, a, lw, gate, k_k, k_a, r_k, ln_w, ln_b, batch, seq, *, pairs=16):
    c = RWKV_CHUNK
    m, d = r.shape
    nc = seq // c
    width = pairs * LANES
    blk = pl.BlockSpec((c, width), lambda b, hp, i: (b * nc + i, hp))
    par = pl.BlockSpec((1, width), lambda b, hp, i: (0, hp))
    return pl.pallas_call(
        functools.partial(_rwkv_kernel, chunk=c, pairs=pairs),
        grid=(batch, d // width, nc),
        in_specs=[blk] * 6 + [par] * 5,
        out_specs=blk,
        out_shape=jax.ShapeDtypeStruct((m, d), BF16),
        scratch_shapes=[pltpu.VMEM((pairs, LANES, LANES), F32)],
        compiler_params=_cparams(("parallel", "parallel", "arbitrary")),
        name="rwkv_recurrence",
    )(r, k, v, a, lw, gate, k_k, k_a, r_k, ln_w, ln_b)


def _pad_cols(w, n):
    return jnp.pad(w, ((0, 0), (0, n - w.shape[1])))


def _rot_cols(w):
    half = MLA_ROPE // 2
    return jnp.concatenate([-w[..., half:], w[..., :half]], axis=-1)


def _hybrid_layer(h, ct, st, batch, seq, g_pre, g_post, w_in, q_norm, w_uq, kv_norm, w_ukv, conv_w, a_log,
                  dt_bias, out_norm, w_out):
    d = D_MODEL
    o_cq, o_ckv, o_kr = 0, MLA_Q_LORA, MLA_Q_LORA + MLA_KV_LORA
    o_g = o_kr + MLA_ROPE
    o_sm = o_g + 4 * GDN_HEADS * GDN_DK
    w_in = w_in.astype(BF16)
    w_kr = w_in[:, o_kr:o_g]
    w_cat = jnp.concatenate([w_in[:, :o_kr], w_in[:, o_g:o_sm], w_kr, _rot_cols(w_kr),
                             _pad_cols(w_in[:, o_sm:], LANES)], axis=1)
    hcat = norm_matmul(h, g_pre.reshape(1, d), w_cat, tn=1792, out_dtype=BF16)

    wq = w_uq.reshape(MLA_Q_LORA, MLA_HEADS, MLA_QK)
    wq_rope = wq[..., MLA_NOPE:]
    wq = jnp.concatenate([wq[..., :MLA_NOPE], wq_rope, _rot_cols(wq_rope)], axis=-1)
    wq = wq.reshape(MLA_Q_LORA, MLA_HEADS * MLA_HEAD_PAD).astype(BF16)
    q, k, v = mla_proj(hcat, q_norm.reshape(1, -1), kv_norm.reshape(1, -1), wq, w_ukv.astype(BF16), ct, st)
    mla_out = mla_attention(q, k, v, batch, seq)
    gdn_out = gdn_mixer(hcat, conv_w, a_log, dt_bias, out_norm.reshape(1, -1), batch, seq)
    return matmul_norm_residual([mla_out, gdn_out], w_out.astype(BF16), g_post.reshape(1, d), h, tn=d)


def _rwkv_layer(h, v_first, batch, seq, g_pre, g_post, mix, w_r, w_k, w_v, w_o, w0, w1, w2, a0, a1, a2,
                g1, g2, k_k, k_a, r_k, ln_w, ln_b, vres):
    d = D_MODEL
    row = lambda t: t.reshape(1, d).astype(F32)
    padc = lambda w, n: _pad_cols(w, n)
    padr = lambda w, n: jnp.pad(w, ((0, n - w.shape[0]), (0, 0)))
    has_v = vres is not None
    v0, v1, v2 = vres if has_v else (jnp.zeros((d,), F32), jnp.zeros((d, 0), F32), jnp.zeros((0, d), F32))
    a1_cat = jnp.concatenate([padc(w1, LZ_A - LZ_W), padc(a1, LZ_G - LZ_A), padc(g1, LZ_V - LZ_G),
                              padc(v1, LZ_COLS - LZ_V)], axis=1).astype(BF16)
    b_cat = jnp.concatenate([padr(w2, LZ_A - LZ_W), padr(a2, LZ_G - LZ_A), padr(g2, LZ_V - LZ_G),
                             padr(v2, LZ_COLS - LZ_V)], axis=0).astype(BF16)
    bias = jnp.stack([w0, a0, v0]).astype(F32)
    xs, z = rwkv_mix(h, g_pre.reshape(1, d), mix, a1_cat, seq, has_v=has_v)
    r = matmul(xs, 0, w_r.astype(BF16), out_dtype=BF16)
    k = matmul(xs, 1, w_k.astype(BF16), out_dtype=BF16)
    v = matmul(xs, 2, w_v.astype(BF16), out_dtype=BF16)
    if has_v:
        lw, a, gate, v = lora_up(z, b_cat, bias, v, v_first)
    else:
        lw, a, gate = lora_up(z, b_cat, bias)
        v_first = v
    y = rwkv_recurrence(r, k, v, a, lw, gate, row(k_k), row(k_a), row(r_k), row(ln_w), row(ln_b), batch, seq)
    return matmul_norm_residual([y], w_o.astype(BF16), g_post.reshape(1, d), h, tn=d), v_first


def _ffn_layer(h, g_pre, g_post, w_gate, w_up, w_down):
    d = D_MODEL
    act = ffn_up(h, g_pre.reshape(1, d), w_gate.astype(BF16), w_up.astype(BF16))
    return matmul_norm_residual([act], w_down.astype(BF16), g_post.reshape(1, d), h)


def _rope_tables(positions):
    inv_freq = 1.0 / (ROPE_THETA ** (jnp.arange(0, MLA_ROPE, 2, dtype=F32) / MLA_ROPE))
    ang = positions.astype(F32).reshape(-1, 1) * inv_freq
    zeros = jnp.zeros((ang.shape[0], LANES - MLA_ROPE), F32)
    cos, sin = jnp.cos(ang), jnp.sin(ang)
    return jnp.concatenate([cos, cos, zeros], axis=1), jnp.concatenate([sin, sin, zeros], axis=1)


def kernel(x, positions, norm_mix_pre, norm_mix_post, norm_ffn_pre, norm_ffn_post, hyb_w_in, mla_q_norm, mla_w_uq, mla_kv_norm, mla_w_ukv, gdn_conv_w, gdn_a_log, gdn_dt_bias, gdn_out_norm, hyb_w_out, rwkv_mix, rwkv_w_r, rwkv_w_k, rwkv_w_v, rwkv_w_o, rwkv_w0, rwkv_w1, rwkv_w2, rwkv_a0, rwkv_a1, rwkv_a2, rwkv_g1, rwkv_g2, rwkv_k_k, rwkv_k_a, rwkv_r_k, rwkv_ln_w, rwkv_ln_b, rwkv_v0, rwkv_v1, rwkv_v2, ffn_w_gate, ffn_w_up, ffn_w_down):
    batch, seq, d = x.shape
    h = x.reshape(batch * seq, d)
    ct, st = _rope_tables(positions)
    v_first = None
    for layer in range(DEPTH):
        if layer % 2 == 0:
            e = layer // 2
            h = _hybrid_layer(h, ct, st, batch, seq, norm_mix_pre[layer], norm_mix_post[layer], hyb_w_in[e],
                              mla_q_norm[e], mla_w_uq[e], mla_kv_norm[e], mla_w_ukv[e], gdn_conv_w[e],
                              gdn_a_log[e], gdn_dt_bias[e], gdn_out_norm[e], hyb_w_out[e])
        else:
            o = layer // 2
            vres = None if o == 0 else (rwkv_v0[o - 1], rwkv_v1[o - 1], rwkv_v2[o - 1])
            h, v_first = _rwkv_layer(h, v_first, batch, seq, norm_mix_pre[layer], norm_mix_post[layer],
                                     rwkv_mix[o], rwkv_w_r[o], rwkv_w_k[o], rwkv_w_v[o], rwkv_w_o[o],
                                     rwkv_w0[o], rwkv_w1[o], rwkv_w2[o], rwkv_a0[o], rwkv_a1[o], rwkv_a2[o],
                                     rwkv_g1[o], rwkv_g2[o], rwkv_k_k[o], rwkv_k_a[o], rwkv_r_k[o],
                                     rwkv_ln_w[o], rwkv_ln_b[o], vres)
        h = _ffn_layer(h, norm_ffn_pre[layer], norm_ffn_post[layer], ffn_w_gate[layer], ffn_w_up[layer],
                       ffn_w_down[layer])
    return h.reshape(batch, seq, d)
```

```python
import functools

import jax
import jax.numpy as jnp
import numpy as np
from jax import lax
from jax.experimental import pallas as pl
from jax.experimental.pallas import tpu as pltpu

F32 = jnp.float32
BF16 = jnp.bfloat16
HI = lax.Precision.HIGHEST

D_MODEL = 2048
DEPTH = 4
LANES = 128
VMEM_LIMIT = 56 * 1024 * 1024

MLA_HEADS = 8
MLA_Q_LORA = 512
MLA_KV_LORA = 512
MLA_NOPE = 128
MLA_ROPE = 64
MLA_V = 128
MLA_QK = MLA_NOPE + MLA_ROPE
MLA_HEAD_PAD = 256
ROPE_THETA = 10000.0

GDN_HEADS = 8
GDN_DK = 128
GDN_DV = 128
GDN_CONV = 4
GDN_CHUNK = 128

RWKV_HEAD = 64
RWKV_HEADS = D_MODEL // RWKV_HEAD
RWKV_CHUNK = 64
RWKV_LN_EPS = 64e-5
RMS_EPS = 1e-6

HC_CQ, HC_CKV, HC_GQ, HC_GK, HC_GV, HC_GZ, HC_KR, HC_SM = 0, 512, 1024, 2048, 3072, 4096, 5120, 5248
HC_COLS = 5376


def _cparams(sem):
    return pltpu.CompilerParams(dimension_semantics=sem, vmem_limit_bytes=VMEM_LIMIT)


def _tile(n, pref):
    t = min(n, pref)
    while n % t:
        t -= 8
    return t


def _layer_spec(w, layer, blk, imap):
    return pl.BlockSpec((None,) + blk, lambda *g: (layer,) + imap(*g))


def _dot(a, b, prec=None):
    return jnp.dot(a, b, preferred_element_type=F32, precision=prec)


def _dot_nt(a, b, prec=None):
    return lax.dot_general(a, b, (((1,), (1,)), ((), ())), preferred_element_type=F32, precision=prec)


def _rms(x, g, eps=RMS_EPS):
    return x * lax.rsqrt(jnp.mean(x * x, axis=-1, keepdims=True) + eps) * g


def _sigmoid(x):
    return 1.0 / (1.0 + jnp.exp(-x))


def _silu(x):
    return x * _sigmoid(x)


def _softplus(x):
    return jnp.maximum(x, 0.0) + jnp.log(1.0 + jnp.exp(-jnp.abs(x)))


def _norm_mm_kernel(x_ref, g_ref, w_ref, o_ref, xn_ref):
    @pl.when(pl.program_id(1) == 0)
    def _():
        xn_ref[...] = _rms(x_ref[...], g_ref[...]).astype(BF16)

    o_ref[...] = _dot(xn_ref[...], w_ref[...]).astype(o_ref.dtype)


def norm_matmul(x, g, w, *, tn, out_dtype=F32, tm=512):
    m, k = x.shape
    n = w.shape[1]
    tm = _tile(m, tm)
    return pl.pallas_call(
        _norm_mm_kernel,
        grid=(m // tm, n // tn),
        in_specs=[pl.BlockSpec((tm, k), lambda i, j: (i, 0)),
                  pl.BlockSpec((1, k), lambda i, j: (0, 0)),
                  pl.BlockSpec((k, tn), lambda i, j: (0, j))],
        out_specs=pl.BlockSpec((tm, tn), lambda i, j: (i, j)),
        out_shape=jax.ShapeDtypeStruct((m, n), out_dtype),
        scratch_shapes=[pltpu.VMEM((tm, k), BF16)],
        compiler_params=_cparams(("parallel", "arbitrary")),
        name="norm_matmul",
    )(x, g, w)


def _ffn_up_kernel(x_ref, g_ref, wg_ref, wu_ref, o_ref, xn_ref):
    @pl.when(pl.program_id(1) == 0)
    def _():
        xn_ref[...] = _rms(x_ref[...], g_ref[...]).astype(BF16)

    xn = xn_ref[...]
    gate = _dot(xn, wg_ref[...])
    up = _dot(xn, wu_ref[...])
    o_ref[...] = (_silu(gate) * up).astype(o_ref.dtype)


def ffn_up(h, g, wg, wu, layer, *, tm=512, tn=1408):
    m, k = h.shape
    n = wg.shape[2]
    tm = _tile(m, tm)
    tn = _tile(n, tn)
    return pl.pallas_call(
        _ffn_up_kernel,
        grid=(m // tm, n // tn),
        in_specs=[pl.BlockSpec((tm, k), lambda i, j: (i, 0)),
                  pl.BlockSpec((1, k), lambda i, j: (0, 0)),
                  _layer_spec(wg, layer, (k, tn), lambda i, j: (0, j)),
                  _layer_spec(wu, layer, (k, tn), lambda i, j: (0, j))],
        out_specs=pl.BlockSpec((tm, tn), lambda i, j: (i, j)),
        out_shape=jax.ShapeDtypeStruct((m, n), BF16),
        scratch_shapes=[pltpu.VMEM((tm, k), BF16)],
        compiler_params=_cparams(("parallel", "arbitrary")),
        name="ffn_up",
    )(h, g, wg, wu)


def _mm_norm_res_kernel(*refs, tn, nb, n_a):
    a_refs = refs[:n_a]
    w_ref, g_ref, h_ref, o_ref, f_ref = refs[n_a:]
    j = pl.program_id(1)
    k0 = 0
    f = None
    for a_ref in a_refs:
        part = _dot(a_ref[...], w_ref[k0:k0 + a_ref.shape[1], :])
        f = part if f is None else f + part
        k0 += a_ref.shape[1]
    f_ref[j] = f

    @pl.when(j == nb - 1)
    def _():
        ssq = jnp.sum(f_ref[0] * f_ref[0], axis=-1, keepdims=True)
        for b in range(1, nb):
            ssq = ssq + jnp.sum(f_ref[b] * f_ref[b], axis=-1, keepdims=True)
        inv = lax.rsqrt(ssq * (1.0 / (nb * tn)) + RMS_EPS)
        for b in range(nb):
            cs = slice(b * tn, (b + 1) * tn)
            o_ref[:, cs] = h_ref[:, cs] + f_ref[b] * inv * g_ref[:, cs]


def matmul_norm_residual(a_parts, w, layer, g, h, *, tm=512, tn=512):
    m = a_parts[0].shape[0]
    _, k, n = w.shape
    tm = _tile(m, tm)
    tn = _tile(n, tn)
    nb = n // tn
    return pl.pallas_call(
        functools.partial(_mm_norm_res_kernel, tn=tn, nb=nb, n_a=len(a_parts)),
        grid=(m // tm, nb),
        in_specs=[pl.BlockSpec((tm, a.shape[1]), lambda i, j: (i, 0)) for a in a_parts] + [
                  _layer_spec(w, layer, (k, tn), lambda i, j: (0, j)),
                  pl.BlockSpec((1, n), lambda i, j: (0, 0)),
                  pl.BlockSpec((tm, n), lambda i, j: (i, 0))],
        out_specs=pl.BlockSpec((tm, n), lambda i, j: (i, 0)),
        out_shape=jax.ShapeDtypeStruct((m, n), F32),
        scratch_shapes=[pltpu.VMEM((nb, tm, tn), F32)],
        compiler_params=_cparams(("parallel", "arbitrary")),
        name="matmul_norm_residual",
    )(*a_parts, w, g, h)


def _mm_kernel(a_ref, w_ref, o_ref):
    o_ref[...] = _dot(a_ref[...], w_ref[...]).astype(o_ref.dtype)


def matmul(a, idx, w, layer, *, tm=1024, tn=1024, out_dtype=F32):
    _, m, k = a.shape
    n = w.shape[2]
    tm = _tile(m, tm)
    tn = _tile(n, tn)
    return pl.pallas_call(
        _mm_kernel,
        grid=(m // tm, n // tn),
        in_specs=[pl.BlockSpec((None, tm, k), lambda i, j: (idx, i, 0)),
                  _layer_spec(w, layer, (k, tn), lambda i, j: (0, j))],
        out_specs=pl.BlockSpec((tm, tn), lambda i, j: (i, j)),
        out_shape=jax.ShapeDtypeStruct((m, n), out_dtype),
        compiler_params=_cparams(("parallel", "arbitrary")),
        name="matmul",
    )(a, w)


def _rope_group(grp, ct, st):
    return grp * ct + pltpu.roll(grp, 64, 1) * st


def _mla_proj_kernel(cq_ref, ckv_ref, kr_ref, qn_ref, kvn_ref, wq_ref, wkv_ref, ct_ref, st_ref,
                     q_ref, k_ref, v_ref):
    scale = MLA_QK ** -0.5 * np.log2(np.e)
    ct = ct_ref[...]
    st = st_ref[...]
    xq = _rms(cq_ref[...].astype(F32), qn_ref[...]).astype(BF16)
    xkv = _rms(ckv_ref[...].astype(F32), kvn_ref[...]).astype(BF16)
    kr = _rope_group(kr_ref[...].astype(F32), ct, st).astype(BF16)
    for h in range(MLA_HEADS):
        c0 = h * MLA_HEAD_PAD
        yq = _dot(xq, wq_ref[:, c0:c0 + MLA_HEAD_PAD])
        q_ref[:, c0:c0 + LANES] = (yq[:, :LANES] * scale).astype(BF16)
        q_ref[:, c0 + LANES:c0 + 2 * LANES] = (_rope_group(yq[:, LANES:], ct, st) * scale).astype(BF16)
        ykv = _dot(xkv, wkv_ref[:, c0:c0 + MLA_HEAD_PAD])
        k_ref[:, c0:c0 + LANES] = ykv[:, :LANES].astype(BF16)
        k_ref[:, c0 + LANES:c0 + 2 * LANES] = kr
        v_ref[:, h * MLA_V:(h + 1) * MLA_V] = ykv[:, LANES:].astype(BF16)


def mla_proj(hcat, q_norm, kv_norm, wq, wkv, ct, st, *, tm=1024):
    m = hcat.shape[0]
    tm = _tile(m, tm)
    hp = MLA_HEADS * MLA_HEAD_PAD
    row = lambda i: (i, 0)
    const = lambda i: (0, 0)
    return pl.pallas_call(
        _mla_proj_kernel,
        grid=(m // tm,),
        in_specs=[pl.BlockSpec((tm, MLA_Q_LORA), lambda i: (i, HC_CQ // MLA_Q_LORA)),
                  pl.BlockSpec((tm, MLA_KV_LORA), lambda i: (i, HC_CKV // MLA_KV_LORA)),
                  pl.BlockSpec((tm, LANES), lambda i: (i, HC_KR // LANES)),
                  pl.BlockSpec((1, MLA_Q_LORA), const),
                  pl.BlockSpec((1, MLA_KV_LORA), const),
                  pl.BlockSpec((MLA_Q_LORA, hp), const),
                  pl.BlockSpec((MLA_KV_LORA, hp), const),
                  pl.BlockSpec((tm, LANES), row),
                  pl.BlockSpec((tm, LANES), row)],
        out_specs=[pl.BlockSpec((tm, hp), row),
                   pl.BlockSpec((tm, hp), row),
                   pl.BlockSpec((tm, MLA_HEADS * MLA_V), row)],
        out_shape=[jax.ShapeDtypeStruct((m, hp), BF16),
                   jax.ShapeDtypeStruct((m, hp), BF16),
                   jax.ShapeDtypeStruct((m, MLA_HEADS * MLA_V), BF16)],
        compiler_params=_cparams(("parallel",)),
        name="mla_proj",
    )(hcat, hcat, hcat, q_norm, kv_norm, wq, wkv, ct, st)


def _attn_kernel(q_ref, k_ref, v_ref, o_ref, *, tq, tk, hpg):
    qi = pl.program_id(2)
    hs = range(hpg)
    q = [q_ref[:, h * MLA_HEAD_PAD:(h + 1) * MLA_HEAD_PAD] for h in hs]

    def step(ki, carry, masked):
        start = pl.multiple_of(ki * tk, tk)
        s = [_dot_nt(q[h], k_ref[pl.ds(start, tk), h * MLA_HEAD_PAD:(h + 1) * MLA_HEAD_PAD]) for h in hs]
        if masked is not None:
            row = lax.broadcasted_iota(jnp.int32, (tq, tk), 0)
            col = lax.broadcasted_iota(jnp.int32, (tq, tk), 1) + masked * tk
            s = [jnp.where(col <= row, x, -1e30) for x in s]
        m_new = [jnp.maximum(carry[h][0], jnp.max(s[h], axis=-1, keepdims=True)) for h in hs]
        alpha = [jnp.exp2(carry[h][0] - m_new[h]) for h in hs]
        p = [jnp.exp2(s[h] - m_new[h]) for h in hs]
        l_new = [alpha[h] * carry[h][1] + jnp.sum(p[h], axis=-1, keepdims=True) for h in hs]
        pv = [_dot(p[h].astype(BF16), v_ref[pl.ds(start, tk), h * MLA_V:(h + 1) * MLA_V]) for h in hs]
        return tuple((m_new[h], l_new[h], alpha[h] * carry[h][2] + pv[h]) for h in hs)

    init = tuple((jnp.full((tq, 1), -1e30, F32), jnp.zeros((tq, 1), F32), jnp.zeros((tq, MLA_V), F32))
                 for _ in hs)
    per_q = tq // tk
    carry = lax.fori_loop(0, qi * per_q, lambda ki, c: step(ki, c, None), init)
    for j in range(per_q):
        carry = step(qi * per_q + j, carry, j)
    for h in hs:
        o_ref[:, h * MLA_V:(h + 1) * MLA_V] = (carry[h][2] / carry[h][1]).astype(o_ref.dtype)


def mla_attention(q, k, v, batch, seq, *, tq=1024, tk=1024, hpg=2):
    tq = _tile(seq, tq)
    tk = _tile(tq, tk)
    nq = seq // tq
    return pl.pallas_call(
        functools.partial(_attn_kernel, tq=tq, tk=tk, hpg=hpg),
        grid=(batch, MLA_HEADS // hpg, nq),
        in_specs=[pl.BlockSpec((tq, hpg * MLA_HEAD_PAD), lambda b, h, i: (b * nq + i, h)),
                  pl.BlockSpec((seq, hpg * MLA_HEAD_PAD), lambda b, h, i: (b, h)),
                  pl.BlockSpec((seq, hpg * MLA_V), lambda b, h, i: (b, h))],
        out_specs=pl.BlockSpec((tq, hpg * MLA_V), lambda b, h, i: (b * nq + i, h)),
        out_shape=jax.ShapeDtypeStruct((batch * seq, MLA_HEADS * MLA_V), BF16),
        compiler_params=_cparams(("parallel", "parallel", "arbitrary")),
        name="mla_attention",
    )(q, k, v)


def _split2(x):
    hi = x.astype(BF16)
    return hi, (x - hi.astype(F32)).astype(BF16)


def _dot_x3(a, b):
    ah, al = _split2(a)
    bh, bl = _split2(b)
    return _dot(jnp.concatenate([ah, al, ah], axis=1), jnp.concatenate([bh, bh, bl], axis=0))


def _dot_nt_x3(a, b):
    ah, al = _split2(a)
    bh, bl = _split2(b)
    return _dot_nt(jnp.concatenate([ah, al, ah], axis=1), jnp.concatenate([bh, bh, bl], axis=1))


def _dot_bf(a, b):
    return _dot(a.astype(BF16), b.astype(BF16))


def _unit_lower_inverse(a_list, levels, dot):
    n = a_list[0].shape[0]
    eye = (lax.broadcasted_iota(jnp.int32, (n, n), 0) == lax.broadcasted_iota(jnp.int32, (n, n), 1)).astype(F32)
    p = [-a for a in a_list]
    r = [eye + x for x in p]
    p = [dot(x, x) for x in p]
    for _ in range(levels - 2):
        s = [dot(jnp.concatenate([ri, pi], axis=0), pi) for ri, pi in zip(r, p)]
        r = [ri + si[:n] for ri, si in zip(r, s)]
        p = [si[n:] for si in s]
    return [ri + dot(ri, pi) for ri, pi in zip(r, p)]


def _cumsum_rows(x):
    c = x.shape[0]
    row = lax.broadcasted_iota(jnp.int32, x.shape, 0)
    s = 1
    while s < c:
        x = x + jnp.where(row >= s, pltpu.roll(x, s, 0), 0.0)
        s *= 2
    return x


def _col_of_row(row_vec, n):
    eye = lax.broadcasted_iota(jnp.int32, (n, n), 0) == lax.broadcasted_iota(jnp.int32, (n, n), 1)
    return jnp.sum(jnp.where(eye, jnp.broadcast_to(row_vec, (n, n)), 0.0), axis=1, keepdims=True)


def _gdn_kernel(gq_ref, gk_ref, gv_ref, gz_ref, sm_ref, cw_ref, alog_ref, dtb_ref, on_ref, o_ref,
                xq_scr, xk_scr, xv_scr, state_scr, *, chunk):
    c = chunk
    heads = range(GDN_HEADS)
    hs = [slice(h * LANES, (h + 1) * LANES) for h in heads]

    @pl.when(pl.program_id(1) == 0)
    def _():
        for scr in (xq_scr, xk_scr, xv_scr):
            scr[0:8, :] = jnp.zeros((8, scr.shape[1]), F32)
        state_scr[...] = jnp.zeros_like(state_scr)

    def conv_silu(x_ref, scr, widx):
        scr[8:8 + c, :] = x_ref[...].astype(F32)
        acc = cw_ref[widx, GDN_CONV - 1:GDN_CONV, :] * scr[8:8 + c, :]
        for j in range(GDN_CONV - 1):
            s = GDN_CONV - 1 - j
            acc = acc + cw_ref[widx, j:j + 1, :] * scr[8 - s:8 - s + c, :]
        scr[0:8, :] = scr[c:c + 8, :]
        return _silu(acc)

    def l2n(x):
        return x * lax.rsqrt(jnp.sum(x * x, axis=-1, keepdims=True) + 1e-6)

    def bf(x):
        return x.astype(BF16)

    qa = conv_silu(gq_ref, xq_scr, 0)
    ka = conv_silu(gk_ref, xk_scr, 1)
    va = conv_silu(gv_ref, xv_scr, 2)
    q = [l2n(qa[:, s]) * (GDN_DK ** -0.5) for s in hs]
    k = [l2n(ka[:, s]) for s in hs]
    v = [va[:, s] for s in hs]

    sm = sm_ref[...].astype(F32)
    lane = lax.broadcasted_iota(jnp.int32, (c, LANES), 1)
    beta = [_sigmoid(jnp.sum(jnp.where(lane == h, sm, 0.0), axis=-1, keepdims=True)) for h in heads]
    ga = [jnp.sum(jnp.where(lane == GDN_HEADS + h, sm, 0.0), axis=-1, keepdims=True) for h in heads]
    g = [-jnp.exp(alog_ref[:, hs[h]]) * _softplus(jnp.broadcast_to(ga[h], (c, LANES)) + dtb_ref[:, hs[h]])
         for h in heads]

    ri = lax.broadcasted_iota(jnp.int32, (c, c), 0)
    cj = lax.broadcasted_iota(jnp.int32, (c, c), 1)
    gam = [_cumsum_rows(x) for x in g]
    decay = [jnp.exp(jnp.where(cj <= ri, x - x.T, -1e30)) for x in gam]
    eg = [jnp.exp(x) for x in gam]
    gl = [x[c - 1:c, :] for x in gam]
    kb = [k[h] * beta[h] for h in heads]
    vb = [v[h] * beta[h] for h in heads]

    a_mat = [jnp.where(cj < ri, _dot_nt_x3(kb[h], k[h]) * decay[h], 0.0) for h in heads]
    tinv = _unit_lower_inverse(a_mat, 7, _dot_x3)
    rhs = [jnp.concatenate([kb[h] * eg[h], vb[h]], axis=1) for h in heads]
    wu = [bf(_dot_x3(tinv[h], rhs[h])) for h in heads]
    attn = [_dot_nt(bf(q[h]), bf(k[h])) * decay[h] for h in heads]
    k_dec = [k[h] * jnp.exp(gl[h] - gam[h]) for h in heads]
    lhs = [bf(jnp.concatenate([attn[h], k_dec[h].T], axis=0)) for h in heads]
    z = [_dot(lhs[h], wu[h]) for h in heads]
    state = [state_scr[h] for h in heads]
    l2 = [bf(jnp.concatenate([q[h] * eg[h] - z[h][:c, :LANES], -z[h][c:, :LANES]], axis=0)) for h in heads]
    rr = [_dot(l2[h], bf(state[h])) for h in heads]
    for h in heads:
        state_scr[h] = state[h] * jnp.exp(gl[h]) + rr[h][c:] + z[h][c:, LANES:]
        o = rr[h][:c] + z[h][:c, LANES:]
        o_ref[:, hs[h]] = (_rms(o, on_ref[...]) * _silu(gz_ref[:, hs[h]].astype(F32))).astype(o_ref.dtype)


def gdn_mixer(hcat, conv_w, a_log, dt_bias, out_norm, batch, seq):
    c = GDN_CHUNK
    assert c == LANES and seq % c == 0
    nc = seq // c
    width = GDN_HEADS * LANES
    blk = lambda off: pl.BlockSpec((c, width), lambda b, i, off=off: (b * nc + i, off // width))
    const2 = lambda b, i: (0, 0)
    cw = conv_w.reshape(GDN_CONV, 3, width).transpose(1, 0, 2)
    alog = jnp.repeat(a_log.astype(F32), LANES).reshape(1, width)
    dtb = jnp.repeat(dt_bias.astype(F32), LANES).reshape(1, width)
    return pl.pallas_call(
        functools.partial(_gdn_kernel, chunk=c),
        grid=(batch, nc),
        in_specs=[blk(HC_GQ), blk(HC_GK), blk(HC_GV), blk(HC_GZ),
                  pl.BlockSpec((c, LANES), lambda b, i: (b * nc + i, HC_SM // LANES)),
                  pl.BlockSpec((3, GDN_CONV, width), lambda b, i: (0, 0, 0)),
                  pl.BlockSpec((1, width), const2),
                  pl.BlockSpec((1, width), const2),
                  pl.BlockSpec((1, GDN_DV), const2)],
        out_specs=pl.BlockSpec((c, width), lambda b, i: (b * nc + i, 0)),
        out_shape=jax.ShapeDtypeStruct((batch * seq, width), BF16),
        scratch_shapes=[pltpu.VMEM((c + 8, width), F32)] * 3 + [pltpu.VMEM((GDN_HEADS, GDN_DK, GDN_DV), F32)],
        compiler_params=_cparams(("parallel", "arbitrary")),
        name="gdn_mixer",
    )(hcat, hcat, hcat, hcat, hcat, cw, alog, dtb, out_norm)


LZ_W, LZ_A, LZ_G, LZ_V, LZ_COLS = 0, 128, 256, 512, 640


def _rwkv_mix_kernel(h_ref, hp_ref, g_ref, mix_ref, a1_ref, o_ref, z_ref, *, tm, seq, has_v):
    i = pl.program_id(0)
    g = g_ref[...]
    xn = _rms(h_ref[...], g)
    prev_last = _rms(hp_ref[...], g)[7:8, :]
    prev_last = jnp.where((i * tm) % seq == 0, 0.0, prev_last)
    rows = lax.broadcasted_iota(jnp.int32, xn.shape, 0)
    xprev = jnp.where(rows == 0, prev_last, pltpu.roll(xn, 1, 0))
    xx = xprev - xn

    def mixed(j):
        return (xn + xx * mix_ref[j:j + 1, :]).astype(BF16)

    o_ref[0] = mixed(0)
    o_ref[1] = mixed(2)
    xv = mixed(3)
    o_ref[2] = xv
    z_ref[:, LZ_W:LZ_A] = jnp.tanh(_dot(mixed(1), a1_ref[:, LZ_W:LZ_A])).astype(BF16)
    z_ref[:, LZ_A:LZ_G] = _dot(mixed(4), a1_ref[:, LZ_A:LZ_G]).astype(BF16)
    z_ref[:, LZ_G:LZ_V] = _sigmoid(_dot(mixed(5), a1_ref[:, LZ_G:LZ_V])).astype(BF16)
    if has_v:
        z_ref[:, LZ_V:] = _dot(xv, a1_ref[:, LZ_V:]).astype(BF16)
    else:
        z_ref[:, LZ_V:] = jnp.zeros((tm, LZ_COLS - LZ_V), BF16)


def rwkv_mix(h, g, mix, a1_cat, seq, *, has_v, tm=512):
    m, d = h.shape
    tm = _tile(seq, tm)
    return pl.pallas_call(
        functools.partial(_rwkv_mix_kernel, tm=tm, seq=seq, has_v=has_v),
        grid=(m // tm,),
        in_specs=[pl.BlockSpec((tm, d), lambda i: (i, 0)),
                  pl.BlockSpec((8, d), lambda i: (jnp.maximum(i * (tm // 8) - 1, 0), 0)),
                  pl.BlockSpec((1, d), lambda i: (0, 0)),
                  pl.BlockSpec((6, d), lambda i: (0, 0)),
                  pl.BlockSpec((d, LZ_COLS), lambda i: (0, 0))],
        out_specs=[pl.BlockSpec((3, tm, d), lambda i: (0, i, 0)),
                   pl.BlockSpec((tm, LZ_COLS), lambda i: (i, 0))],
        out_shape=[jax.ShapeDtypeStruct((3, m, d), BF16),
                   jax.ShapeDtypeStruct((m, LZ_COLS), BF16)],
        compiler_params=_cparams(("parallel",)),
        name="rwkv_mix",
    )(h, h, g, mix, a1_cat)


def _lora_up_kernel(z_ref, b_ref, bias_ref, *rest, has_v):
    if has_v:
        v_ref, vf_ref, lw_ref, a_ref, gt_ref, vo_ref = rest
    else:
        lw_ref, a_ref, gt_ref = rest
    y = _dot(z_ref[:, LZ_W:LZ_A], b_ref[LZ_W:LZ_A, :]) + bias_ref[0:1, :]
    lw_ref[...] = -jnp.exp(-_softplus(-y) - 0.5)
    y = _dot(z_ref[:, LZ_A:LZ_G], b_ref[LZ_A:LZ_G, :]) + bias_ref[1:2, :]
    a_ref[...] = _sigmoid(y).astype(a_ref.dtype)
    gt_ref[...] = _dot(z_ref[:, LZ_G:LZ_V], b_ref[LZ_G:LZ_V, :]).astype(gt_ref.dtype)
    if has_v:
        y = _dot(z_ref[:, LZ_V:], b_ref[LZ_V:, :]) + bias_ref[2:3, :]
        v = v_ref[...].astype(F32)
        vo_ref[...] = (v + (vf_ref[...].astype(F32) - v) * _sigmoid(y)).astype(vo_ref.dtype)


def lora_up(z, b_cat, bias, v=None, v_first=None, *, tm=512):
    m = z.shape[0]
    d = b_cat.shape[1]
    has_v = v is not None
    tm = _tile(m, tm)
    row = lambda i: (i, 0)
    const = lambda i: (0, 0)
    big = pl.BlockSpec((tm, d), row)
    n_bf = 3 if has_v else 2
    return pl.pallas_call(
        functools.partial(_lora_up_kernel, has_v=has_v),
        grid=(m // tm,),
        in_specs=[pl.BlockSpec((tm, LZ_COLS), row), pl.BlockSpec((LZ_COLS, d), const),
                  pl.BlockSpec((3, d), const)] + ([big, big] if has_v else []),
        out_specs=[big] * (1 + n_bf),
        out_shape=[jax.ShapeDtypeStruct((m, d), F32)] + [jax.ShapeDtypeStruct((m, d), BF16)] * n_bf,
        compiler_params=_cparams(("parallel",)),
        name="lora_up",
    )(z, b_cat, bias, *((v, v_first) if has_v else ()))


def _rwkv_kernel(r_ref, k_ref, v_ref, a_ref, lw_ref, gt_ref, kk_ref, ka_ref, rk_ref, lnw_ref, lnb_ref,
                 o_ref, state_scr, *, chunk, pairs):
    c = chunk
    ps = range(pairs)
    sls = [slice(p * LANES, (p + 1) * LANES) for p in ps]

    @pl.when(pl.program_id(2) == 0)
    def _():
        state_scr[...] = jnp.zeros_like(state_scr)

    lane = lax.broadcasted_iota(jnp.int32, (c, LANES), 1)
    m0 = lane < RWKV_HEAD
    r2 =lax.broadcasted_iota(jnp.int32, (2 * c, 2 * c), 0)
    c2 = lax.broadcasted_iota(jnp.int32, (2 * c, 2 * c), 1)
    strict2 = c2 < r2
    rw = lax.broadcasted_iota(jnp.int32, (c, 2 * c), 0)
    cw = lax.broadcasted_iota(jnp.int32, (c, 2 * c), 1)
    incl_w = jnp.where(cw >= c, cw - c, cw) <= rw
    zero_blk = jnp.zeros((2 * c, LANES), BF16)

    def seg_sum(x):
        s0 = jnp.sum(jnp.where(m0, x, 0.0), axis=-1, keepdims=True)
        s1 = jnp.sum(jnp.where(m0, 0.0, x), axis=-1, keepdims=True)
        return jnp.where(m0, s0, s1)

    def stack(x):
        return jnp.concatenate([jnp.where(m0, x, 0.0), jnp.where(m0, 0.0, x)], axis=0)

    def bf(x):
        return x.astype(BF16)

    r = [r_ref[:, s].astype(F32) for s in sls]
    k = [k_ref[:, s].astype(F32) for s in sls]
    v = [v_ref[:, s].astype(F32) for s in sls]
    a = [a_ref[:, s].astype(F32) for s in sls]
    lw = [lw_ref[:, s] for s in sls]
    kk = [k[p] * kk_ref[:, sls[p]] for p in ps]
    kk = [x * lax.rsqrt(seg_sum(x * x) + 1e-12) for x in kk]
    kh = [k[p] * (1.0 + (a[p] - 1.0) * ka_ref[:, sls[p]]) for p in ps]
    bv = [kk[p] * a[p] for p in ps]

    cum = [_cumsum_rows(x) for x in lw]
    cl = [x[c - 1:c, :] for x in cum]
    e_neg = [jnp.exp(-x) for x in cum]
    e_end = [jnp.exp(cl[p] - cum[p]) for p in ps]
    rg = [r[p] * jnp.exp(cum[p]) for p in ps]
    rgc = [bf(x) for x in rg]
    xs = [bf(stack(kk[p] * jnp.exp(cum[p] - lw[p]))) for p in ps]
    bs = [bf(stack(bv[p] * e_neg[p])) for p in ps]
    ks = [bf(stack(kh[p] * e_neg[p])) for p in ps]
    vs = [bf(stack(v[p])) for p in ps]
    kht = [stack(kh[p] * e_end[p]).T for p in ps]
    bht = [stack(bv[p] * e_end[p]).T for p in ps]

    ab = [_dot_nt(xs[p], jnp.concatenate([bs[p], ks[p]], axis=0)) for p in ps]
    akk = [jnp.where(strict2, x[:, :2 * c], 0.0) for x in ab]
    akv = [bf(jnp.where(strict2, x[:, 2 * c:], 0.0)) for x in ab]
    tinv = [bf(x) for x in _unit_lower_inverse(akk, 6, _dot_bf)]
    w1 = [bf(_dot(akv[p], vs[p])) for p in ps]
    m12 = [bf(_dot(tinv[p], jnp.concatenate([xs[p], w1[p]], axis=1))) for p in ps]
    bb = [_dot_nt(rgc[p], jnp.concatenate([ks[p], bs[p]], axis=0)) for p in ps]
    lhs = [bf(jnp.concatenate([
        jnp.concatenate([jnp.where(incl_w, x[:, :2 * c], 0.0), -jnp.where(incl_w, x[:, 2 * c:], 0.0)], axis=1),
        jnp.concatenate([kht[p], -bht[p]], axis=1)], axis=0)) for p, x in enumerate(bb)]
    rhs = [jnp.concatenate([jnp.concatenate([vs[p], zero_blk], axis=1),
                            jnp.concatenate([m12[p][:, LANES:], m12[p][:, :LANES]], axis=1)], axis=0)
           for p in ps]
    z = [_dot(lhs[p], rhs[p]) for p in ps]
    state = [state_scr[p] for p in ps]
    l2 = [bf(jnp.concatenate([rg[p] + z[p][:c, LANES:], z[p][c:, LANES:]], axis=0)) for p in ps]
    rr = [_dot(l2[p], bf(state[p])) for p in ps]
    for p in ps:
        sl = sls[p]
        state_scr[p] = state[p] * _col_of_row(jnp.exp(cl[p]), LANES) + rr[p][c:] + z[p][c:, :LANES]
        y = rr[p][:c] + z[p][:c, :LANES]
        mu = seg_sum(y) * (1.0 / RWKV_HEAD)
        yc = y - mu
        var = seg_sum(yc * yc) * (1.0 / RWKV_HEAD)
        yn = yc * lax.rsqrt(var + RWKV_LN_EPS) * lnw_ref[:, sl] + lnb_ref[:, sl]
        bonus = seg_sum(r[p] * kh[p] * rk_ref[:, sl]) * v[p]
        o_ref[:, sl] = ((yn + bonus) * gt_ref[:, sl].astype(F32)).astype(o_ref.dtype)


def rwkv_recurrence(r, k, v, a, lw, gate, k_k, k_a, r_k, ln_w, ln_b, batch, seq, *, pairs=16):
    c = RWKV_CHUNK
    m, d = r.shape
    nc = seq // c
    width = pairs * LANES
    blk = pl.BlockSpec((c, width), lambda b, hp, i: (b * nc + i, hp))
    par = pl.BlockSpec((1, width), lambda b, hp, i: (0, hp))
    return pl.pallas_call(
        functools.partial(_rwkv_kernel, chunk=c, pairs=pairs),
        grid=(batch, d // width, nc),
        in_specs=[blk] * 6 + [par] * 5,
        out_specs=blk,
        out_shape=jax.ShapeDtypeStruct((m, d), BF16),
        scratch_shapes=[pltpu.VMEM((pairs, LANES, LANES), F32)],
        compiler_params=_cparams(("parallel", "parallel", "arbitrary")),
        name="rwkv_recurrence",
    )(r, k, v, a, lw, gate, k_k, k_a, r_k, ln_w, ln_b)


def _pad_cols(w, n):
    return jnp.pad(w, ((0, 0), (0, n - w.shape[1])))


def _rot_cols(w):
    half = MLA_ROPE // 2
    return jnp.concatenate([-w[..., half:], w[..., :half]], axis=-1)


def _hybrid_layer(h, ct, st, batch, seq, g_pre, g_post, w_in, q_norm, w_uq, kv_norm, w_ukv, conv_w, a_log,
                  dt_bias, out_norm, w_out, e):
    d = D_MODEL
    o_cq, o_ckv, o_kr = 0, MLA_Q_LORA, MLA_Q_LORA + MLA_KV_LORA
    o_g = o_kr + MLA_ROPE
    o_sm = o_g + 4 * GDN_HEADS * GDN_DK
    w_in = w_in.astype(BF16)
    w_kr = w_in[:, o_kr:o_g]
    w_cat = jnp.concatenate([w_in[:, :o_kr], w_in[:, o_g:o_sm], w_kr, _rot_cols(w_kr),
                             _pad_cols(w_in[:, o_sm:], LANES)], axis=1)
    hcat = norm_matmul(h, g_pre.reshape(1, d), w_cat, tn=1792, out_dtype=BF16)

    wq = w_uq.reshape(MLA_Q_LORA, MLA_HEADS, MLA_QK)
    wq_rope = wq[..., MLA_NOPE:]
    wq = jnp.concatenate([wq[..., :MLA_NOPE], wq_rope, _rot_cols(wq_rope)], axis=-1)
    wq = wq.reshape(MLA_Q_LORA, MLA_HEADS * MLA_HEAD_PAD).astype(BF16)
    q, k, v = mla_proj(hcat, q_norm.reshape(1, -1), kv_norm.reshape(1, -1), wq, w_ukv.astype(BF16), ct, st)
    mla_out = mla_attention(q, k, v, batch, seq)
    gdn_out = gdn_mixer(hcat, conv_w, a_log, dt_bias, out_norm.reshape(1, -1), batch, seq)
    return matmul_norm_residual([mla_out, gdn_out], w_out, e, g_post.reshape(1, d), h, tn=d)


def _rwkv_layer(h, v_first, batch, seq, g_pre, g_post, mix, w_r, w_k, w_v, w_o, o, w0, w1, w2, a0, a1, a2,
                g1, g2, k_k, k_a, r_k, ln_w, ln_b, vres):
    d = D_MODEL
    row = lambda t: t.reshape(1, d).astype(F32)
    padc = lambda w, n: _pad_cols(w, n)
    padr = lambda w, n: jnp.pad(w, ((0, n - w.shape[0]), (0, 0)))
    has_v = vres is not None
    v0, v1, v2 = vres if has_v else (jnp.zeros((d,), F32), jnp.zeros((d, 0), F32), jnp.zeros((0, d), F32))
    a1_cat = jnp.concatenate([padc(w1, LZ_A - LZ_W), padc(a1, LZ_G - LZ_A), padc(g1, LZ_V - LZ_G),
                              padc(v1, LZ_COLS - LZ_V)], axis=1).astype(BF16)
    b_cat = jnp.concatenate([padr(w2, LZ_A - LZ_W), padr(a2, LZ_G - LZ_A), padr(g2, LZ_V - LZ_G),
                             padr(v2, LZ_COLS - LZ_V)], axis=0).astype(BF16)
    bias = jnp.stack([w0, a0, v0]).astype(F32)
    xs, z = rwkv_mix(h, g_pre.reshape(1, d), mix, a1_cat, seq, has_v=has_v)
    r = matmul(xs, 0, w_r, o, out_dtype=BF16)
    k = matmul(xs, 1, w_k, o, out_dtype=BF16)
    v = matmul(xs, 2, w_v, o, out_dtype=BF16)
    if has_v:
        lw, a, gate, v = lora_up(z, b_cat, bias, v, v_first)
    else:
        lw, a, gate = lora_up(z, b_cat, bias)
        v_first = v
    y = rwkv_recurrence(r, k, v, a, lw, gate, row(k_k), row(k_a), row(r_k), row(ln_w), row(ln_b), batch, seq)
    return matmul_norm_residual([y], w_o, o, g_post.reshape(1, d), h, tn=d), v_first


def _ffn_layer(h, g_pre, g_post, w_gate, w_up, w_down, layer):
    d = D_MODEL
    act = ffn_up(h, g_pre.reshape(1, d), w_gate, w_up, layer)
    return matmul_norm_residual([act], w_down, layer, g_post.reshape(1, d), h)


def _rope_tables(positions):
    inv_freq = 1.0 / (ROPE_THETA ** (jnp.arange(0, MLA_ROPE, 2, dtype=F32) / MLA_ROPE))
    ang = positions.astype(F32).reshape(-1, 1) * inv_freq
    zeros = jnp.zeros((ang.shape[0], LANES - MLA_ROPE), F32)
    cos, sin = jnp.cos(ang), jnp.sin(ang)
    return jnp.concatenate([cos, cos, zeros], axis=1), jnp.concatenate([sin, sin, zeros], axis=1)


def kernel(x, positions, norm_mix_pre, norm_mix_post, norm_ffn_pre, norm_ffn_post, hyb_w_in, mla_q_norm, mla_w_uq, mla_kv_norm, mla_w_ukv, gdn_conv_w, gdn_a_log, gdn_dt_bias, gdn_out_norm, hyb_w_out, rwkv_mix, rwkv_w_r, rwkv_w_k, rwkv_w_v, rwkv_w_o, rwkv_w0, rwkv_w1, rwkv_w2, rwkv_a0, rwkv_a1, rwkv_a2, rwkv_g1, rwkv_g2, rwkv_k_k, rwkv_k_a, rwkv_r_k, rwkv_ln_w, rwkv_ln_b, rwkv_v0, rwkv_v1, rwkv_v2, ffn_w_gate, ffn_w_up, ffn_w_down):
    batch, seq, d = x.shape
    h = x.reshape(batch * seq, d)
    ct, st = _rope_tables(positions)
    bf = lambda w: w.astype(BF16)
    ffn_w_gate, ffn_w_up, ffn_w_down, hyb_w_out = bf(ffn_w_gate), bf(ffn_w_up), bf(ffn_w_down), bf(hyb_w_out)
    rwkv_w_r, rwkv_w_k, rwkv_w_v, rwkv_w_o = bf(rwkv_w_r), bf(rwkv_w_k), bf(rwkv_w_v), bf(rwkv_w_o)
    v_first = None
    for layer in range(DEPTH):
        if layer % 2 == 0:
            e = layer // 2
            h = _hybrid_layer(h, ct, st, batch, seq, norm_mix_pre[layer], norm_mix_post[layer], hyb_w_in[e],
                              mla_q_norm[e], mla_w_uq[e], mla_kv_norm[e], mla_w_ukv[e], gdn_conv_w[e],
                              gdn_a_log[e], gdn_dt_bias[e], gdn_out_norm[e], hyb_w_out, e)
        else:
            o = layer // 2
            vres = None if o == 0 else (rwkv_v0[o - 1], rwkv_v1[o - 1], rwkv_v2[o - 1])
            h, v_first = _rwkv_layer(h, v_first, batch, seq, norm_mix_pre[layer], norm_mix_post[layer],
                                     rwkv_mix[o], rwkv_w_r, rwkv_w_k, rwkv_w_v, rwkv_w_o, o,
                                     rwkv_w0[o], rwkv_w1[o], rwkv_w2[o], rwkv_a0[o], rwkv_a1[o], rwkv_a2[o],
                                     rwkv_g1[o], rwkv_g2[o], rwkv_k_k[o], rwkv_k_a[o], rwkv_r_k[o],
                                     rwkv_ln_w[o], rwkv_ln_b[o], vres)
        h = _ffn_layer(h, norm_ffn_pre[layer], norm_ffn_post[layer], ffn_w_gate, ffn_w_up, ffn_w_down, layer)
    return h.reshape(batch, seq, d)
```

```python
import functools

import jax
import jax.numpy as jnp
import numpy as np
from jax import lax
from jax.experimental import pallas as pl
from jax.experimental.pallas import tpu as pltpu

F32 = jnp.float32
BF16 = jnp.bfloat16
HI = lax.Precision.HIGHEST

D_MODEL = 2048
DEPTH = 4
LANES = 128
VMEM_LIMIT = 56 * 1024 * 1024

MLA_HEADS = 8
MLA_Q_LORA = 512
MLA_KV_LORA = 512
MLA_NOPE = 128
MLA_ROPE = 64
MLA_V = 128
MLA_QK = MLA_NOPE + MLA_ROPE
MLA_HEAD_PAD = 256
ROPE_THETA = 10000.0

GDN_HEADS = 8
GDN_DK = 128
GDN_DV = 128
GDN_CONV = 4
GDN_CHUNK = 128

RWKV_HEAD = 64
RWKV_HEADS = D_MODEL // RWKV_HEAD
RWKV_CHUNK = 64
RWKV_LN_EPS = 64e-5
RMS_EPS = 1e-6

HC_CQ, HC_CKV, HC_GQ, HC_GK, HC_GV, HC_GZ, HC_KR, HC_SM = 0, 512, 1024, 2048, 3072, 4096, 5120, 5248
HC_COLS = 5376


def _cparams(sem):
    return pltpu.CompilerParams(dimension_semantics=sem, vmem_limit_bytes=VMEM_LIMIT)


def _tile(n, pref):
    t = min(n, pref)
    while n % t:
        t -= 8
    return t


def _layer_spec(w, layer, blk, imap):
    return pl.BlockSpec((None,) + blk, lambda *g: (layer,) + imap(*g))


def _dot(a, b, prec=None):
    return jnp.dot(a, b, preferred_element_type=F32, precision=prec)


def _dot_nt(a, b, prec=None):
    return lax.dot_general(a, b, (((1,), (1,)), ((), ())), preferred_element_type=F32, precision=prec)


def _rms(x, g, eps=RMS_EPS):
    return x * lax.rsqrt(jnp.mean(x * x, axis=-1, keepdims=True) + eps) * g


def _sigmoid(x):
    return 1.0 / (1.0 + jnp.exp(-x))


def _silu(x):
    return x * _sigmoid(x)


def _softplus(x):
    return jnp.maximum(x, 0.0) + jnp.log(1.0 + jnp.exp(-jnp.abs(x)))


def _norm_mm_kernel(x_ref, g_ref, w_ref, o_ref, xn_ref):
    @pl.when(pl.program_id(1) == 0)
    def _():
        xn_ref[...] = _rms(x_ref[...], g_ref[...]).astype(BF16)

    o_ref[...] = _dot(xn_ref[...], w_ref[...]).astype(o_ref.dtype)


def norm_matmul(x, g, w, layer, *, tn, out_dtype=F32, tm=1024):
    m, k = x.shape
    n = w.shape[2]
    tm = _tile(m, tm)
    return pl.pallas_call(
        _norm_mm_kernel,
        grid=(m // tm, n // tn),
        in_specs=[pl.BlockSpec((tm, k), lambda i, j: (i, 0)),
                  pl.BlockSpec((1, k), lambda i, j: (0, 0)),
                  _layer_spec(w, layer, (k, tn), lambda i, j: (0, j))],
        out_specs=pl.BlockSpec((tm, tn), lambda i, j: (i, j)),
        out_shape=jax.ShapeDtypeStruct((m, n), out_dtype),
        scratch_shapes=[pltpu.VMEM((tm, k), BF16)],
        compiler_params=_cparams(("parallel", "arbitrary")),
        name="norm_matmul",
    )(x, g, w)


def _ffn_up_kernel(x_ref, g_ref, wg_ref, wu_ref, o_ref, xn_ref):
    @pl.when(pl.program_id(1) == 0)
    def _():
        xn_ref[...] = _rms(x_ref[...], g_ref[...]).astype(BF16)

    xn = xn_ref[...]
    gate = _dot(xn, wg_ref[...])
    up = _dot(xn, wu_ref[...])
    o_ref[...] = (_silu(gate) * up).astype(o_ref.dtype)


def ffn_up(h, g, wg, wu, layer, *, tm=512, tn=1408):
    m, k = h.shape
    n = wg.shape[2]
    tm = _tile(m, tm)
    tn = _tile(n, tn)
    return pl.pallas_call(
        _ffn_up_kernel,
        grid=(m // tm, n // tn),
        in_specs=[pl.BlockSpec((tm, k), lambda i, j: (i, 0)),
                  pl.BlockSpec((1, k), lambda i, j: (0, 0)),
                  _layer_spec(wg, layer, (k, tn), lambda i, j: (0, j)),
                  _layer_spec(wu, layer, (k, tn), lambda i, j: (0, j))],
        out_specs=pl.BlockSpec((tm, tn), lambda i, j: (i, j)),
        out_shape=jax.ShapeDtypeStruct((m, n), BF16),
        scratch_shapes=[pltpu.VMEM((tm, k), BF16)],
        compiler_params=_cparams(("parallel", "arbitrary")),
        name="ffn_up",
    )(h, g, wg, wu)


def _mm_norm_res_kernel(*refs, tn, nb, n_a):
    a_refs = refs[:n_a]
    w_ref, g_ref, h_ref, o_ref, f_ref = refs[n_a:]
    j = pl.program_id(1)
    k0 = 0
    f = None
    for a_ref in a_refs:
        part = _dot(a_ref[...], w_ref[k0:k0 + a_ref.shape[1], :])
        f = part if f is None else f + part
        k0 += a_ref.shape[1]
    f_ref[j] = f

    @pl.when(j == nb - 1)
    def _():
        ssq = jnp.sum(f_ref[0] * f_ref[0], axis=-1, keepdims=True)
        for b in range(1, nb):
            ssq = ssq + jnp.sum(f_ref[b] * f_ref[b], axis=-1, keepdims=True)
        inv = lax.rsqrt(ssq * (1.0 / (nb * tn)) + RMS_EPS)
        for b in range(nb):
            cs = slice(b * tn, (b + 1) * tn)
            o_ref[:, cs] = h_ref[:, cs] + f_ref[b] * inv * g_ref[:, cs]


def matmul_norm_residual(a_parts, w, layer, g, h, *, tm=512, tn=512):
    m = a_parts[0].shape[0]
    _, k, n = w.shape
    tm = _tile(m, tm)
    tn = _tile(n, tn)
    nb = n // tn
    return pl.pallas_call(
        functools.partial(_mm_norm_res_kernel, tn=tn, nb=nb, n_a=len(a_parts)),
        grid=(m // tm, nb),
        in_specs=[pl.BlockSpec((tm, a.shape[1]), lambda i, j: (i, 0)) for a in a_parts] + [
                  _layer_spec(w, layer, (k, tn), lambda i, j: (0, j)),
                  pl.BlockSpec((1, n), lambda i, j: (0, 0)),
                  pl.BlockSpec((tm, n), lambda i, j: (i, 0))],
        out_specs=pl.BlockSpec((tm, n), lambda i, j: (i, 0)),
        out_shape=jax.ShapeDtypeStruct((m, n), F32),
        scratch_shapes=[pltpu.VMEM((nb, tm, tn), F32)],
        compiler_params=_cparams(("parallel", "arbitrary")),
        name="matmul_norm_residual",
    )(*a_parts, w, g, h)


def _mm_kernel(a_ref, w_ref, o_ref):
    o_ref[...] = _dot(a_ref[...], w_ref[...]).astype(o_ref.dtype)


def matmul(a, idx, w, layer, *, tm=1024, tn=1024, out_dtype=F32):
    _, m, k = a.shape
    n = w.shape[2]
    tm = _tile(m, tm)
    tn = _tile(n, tn)
    return pl.pallas_call(
        _mm_kernel,
        grid=(m // tm, n // tn),
        in_specs=[pl.BlockSpec((None, tm, k), lambda i, j: (idx, i, 0)),
                  _layer_spec(w, layer, (k, tn), lambda i, j: (0, j))],
        out_specs=pl.BlockSpec((tm, tn), lambda i, j: (i, j)),
        out_shape=jax.ShapeDtypeStruct((m, n), out_dtype),
        compiler_params=_cparams(("parallel", "arbitrary")),
        name="matmul",
    )(a, w)


def _rope_group(grp, ct, st):
    return grp * ct + pltpu.roll(grp, 64, 1) * st


def _mla_proj_kernel(cq_ref, ckv_ref, kr_ref, qn_ref, kvn_ref, wq_ref, wkv_ref, ct_ref, st_ref,
                     q_ref, k_ref, v_ref):
    scale = MLA_QK ** -0.5 * np.log2(np.e)
    ct = ct_ref[...]
    st = st_ref[...]
    xq = _rms(cq_ref[...].astype(F32), qn_ref[...]).astype(BF16)
    xkv = _rms(ckv_ref[...].astype(F32), kvn_ref[...]).astype(BF16)
    kr = _rope_group(kr_ref[...].astype(F32), ct, st).astype(BF16)
    for h in range(MLA_HEADS):
        c0 = h * MLA_HEAD_PAD
        yq = _dot(xq, wq_ref[:, c0:c0 + MLA_HEAD_PAD])
        q_ref[:, c0:c0 + LANES] = (yq[:, :LANES] * scale).astype(BF16)
        q_ref[:, c0 + LANES:c0 + 2 * LANES] = (_rope_group(yq[:, LANES:], ct, st) * scale).astype(BF16)
        ykv = _dot(xkv, wkv_ref[:, c0:c0 + MLA_HEAD_PAD])
        k_ref[:, c0:c0 + LANES] = ykv[:, :LANES].astype(BF16)
        k_ref[:, c0 + LANES:c0 + 2 * LANES] = kr
        v_ref[:, h * MLA_V:(h + 1) * MLA_V] = ykv[:, LANES:].astype(BF16)


def mla_proj(hcat, q_norm, kv_norm, wq, wkv, ct, st, *, tm=1024):
    m = hcat.shape[0]
    tm = _tile(m, tm)
    hp = MLA_HEADS * MLA_HEAD_PAD
    row = lambda i: (i, 0)
    const = lambda i: (0, 0)
    return pl.pallas_call(
        _mla_proj_kernel,
        grid=(m // tm,),
        in_specs=[pl.BlockSpec((tm, MLA_Q_LORA), lambda i: (i, HC_CQ // MLA_Q_LORA)),
                  pl.BlockSpec((tm, MLA_KV_LORA), lambda i: (i, HC_CKV // MLA_KV_LORA)),
                  pl.BlockSpec((tm, LANES), lambda i: (i, HC_KR // LANES)),
                  pl.BlockSpec((1, MLA_Q_LORA), const),
                  pl.BlockSpec((1, MLA_KV_LORA), const),
                  pl.BlockSpec((MLA_Q_LORA, hp), const),
                  pl.BlockSpec((MLA_KV_LORA, hp), const),
                  pl.BlockSpec((tm, LANES), row),
                  pl.BlockSpec((tm, LANES), row)],
        out_specs=[pl.BlockSpec((tm, hp), row),
                   pl.BlockSpec((tm, hp), row),
                   pl.BlockSpec((tm, MLA_HEADS * MLA_V), row)],
        out_shape=[jax.ShapeDtypeStruct((m, hp), BF16),
                   jax.ShapeDtypeStruct((m, hp), BF16),
                   jax.ShapeDtypeStruct((m, MLA_HEADS * MLA_V), BF16)],
        compiler_params=_cparams(("parallel",)),
        name="mla_proj",
    )(hcat, hcat, hcat, q_norm, kv_norm, wq, wkv, ct, st)


def _attn_kernel(q_ref, k_ref, v_ref, o_ref, *, tq, tk, hpg):
    qi = pl.program_id(2)
    hs = range(hpg)
    q = [q_ref[:, h * MLA_HEAD_PAD:(h + 1) * MLA_HEAD_PAD] for h in hs]

    def step(ki, carry, masked):
        start = pl.multiple_of(ki * tk, tk)
        s = [_dot_nt(q[h], k_ref[pl.ds(start, tk), h * MLA_HEAD_PAD:(h + 1) * MLA_HEAD_PAD]) for h in hs]
        if masked is not None:
            row = lax.broadcasted_iota(jnp.int32, (tq, tk), 0)
            col = lax.broadcasted_iota(jnp.int32, (tq, tk), 1) + masked * tk
            s = [jnp.where(col <= row, x, -1e30) for x in s]
        m_new = [jnp.maximum(carry[h][0], jnp.max(s[h], axis=-1, keepdims=True)) for h in hs]
        alpha = [jnp.exp2(carry[h][0] - m_new[h]) for h in hs]
        p = [jnp.exp2(s[h] - m_new[h]) for h in hs]
        l_new = [alpha[h] * carry[h][1] + jnp.sum(p[h], axis=-1, keepdims=True) for h in hs]
        pv = [_dot(p[h].astype(BF16), v_ref[pl.ds(start, tk), h * MLA_V:(h + 1) * MLA_V]) for h in hs]
        return tuple((m_new[h], l_new[h], alpha[h] * carry[h][2] + pv[h]) for h in hs)

    init = tuple((jnp.full((tq, 1), -1e30, F32), jnp.zeros((tq, 1), F32), jnp.zeros((tq, MLA_V), F32))
                 for _ in hs)
    per_q = tq // tk
    carry = lax.fori_loop(0, qi * per_q, lambda ki, c: step(ki, c, None), init)
    for j in range(per_q):
        carry = step(qi * per_q + j, carry, j)
    for h in hs:
        o_ref[:, h * MLA_V:(h + 1) * MLA_V] = (carry[h][2] / carry[h][1]).astype(o_ref.dtype)


def mla_attention(q, k, v, batch, seq, *, tq=1024, tk=1024, hpg=2):
    tq = _tile(seq, tq)
    tk = _tile(tq, tk)
    nq = seq // tq
    return pl.pallas_call(
        functools.partial(_attn_kernel, tq=tq, tk=tk, hpg=hpg),
        grid=(batch, MLA_HEADS // hpg, nq),
        in_specs=[pl.BlockSpec((tq, hpg * MLA_HEAD_PAD), lambda b, h, i: (b * nq + i, h)),
                  pl.BlockSpec((seq, hpg * MLA_HEAD_PAD), lambda b, h, i: (b, h)),
                  pl.BlockSpec((seq, hpg * MLA_V), lambda b, h, i: (b, h))],
        out_specs=pl.BlockSpec((tq, hpg * MLA_V), lambda b, h, i: (b * nq + i, h)),
        out_shape=jax.ShapeDtypeStruct((batch * seq, MLA_HEADS * MLA_V), BF16),
        compiler_params=_cparams(("parallel", "parallel", "arbitrary")),
        name="mla_attention",
    )(q, k, v)


def _split2(x):
    hi = x.astype(BF16)
    return hi, (x - hi.astype(F32)).astype(BF16)


def _dot_x3(a, b):
    ah, al = _split2(a)
    bh, bl = _split2(b)
    return _dot(jnp.concatenate([ah, al, ah], axis=1), jnp.concatenate([bh, bh, bl], axis=0))


def _dot_nt_x3(a, b):
    ah, al = _split2(a)
    bh, bl = _split2(b)
    return _dot_nt(jnp.concatenate([ah, al, ah], axis=1), jnp.concatenate([bh, bh, bl], axis=1))


def _dot_bf(a, b):
    return _dot(a.astype(BF16), b.astype(BF16))


def _unit_lower_inverse(a_list, levels, dot):
    n = a_list[0].shape[0]
    eye = (lax.broadcasted_iota(jnp.int32, (n, n), 0) == lax.broadcasted_iota(jnp.int32, (n, n), 1)).astype(F32)
    p = [-a for a in a_list]
    r = [eye + x for x in p]
    p = [dot(x, x) for x in p]
    for _ in range(levels - 2):
        s = [dot(jnp.concatenate([ri, pi], axis=0), pi) for ri, pi in zip(r, p)]
        r = [ri + si[:n] for ri, si in zip(r, s)]
        p = [si[n:] for si in s]
    return [ri + dot(ri, pi) for ri, pi in zip(r, p)]


def _cumsum_rows(x):
    c = x.shape[0]
    row = lax.broadcasted_iota(jnp.int32, x.shape, 0)
    s = 1
    while s < c:
        x = x + jnp.where(row >= s, pltpu.roll(x, s, 0), 0.0)
        s *= 2
    return x


def _col_of_row(row_vec, n):
    eye = lax.broadcasted_iota(jnp.int32, (n, n), 0) == lax.broadcasted_iota(jnp.int32, (n, n), 1)
    return jnp.sum(jnp.where(eye, jnp.broadcast_to(row_vec, (n, n)), 0.0), axis=1, keepdims=True)


def _gdn_kernel(gq_ref, gk_ref, gv_ref, gz_ref, sm_ref, cw_ref, alog_ref, dtb_ref, on_ref, o_ref,
                xq_scr, xk_scr, xv_scr, state_scr, *, chunk):
    c = chunk
    heads = range(GDN_HEADS)
    hs = [slice(h * LANES, (h + 1) * LANES) for h in heads]

    @pl.when(pl.program_id(1) == 0)
    def _():
        for scr in (xq_scr, xk_scr, xv_scr):
            scr[0:8, :] = jnp.zeros((8, scr.shape[1]), F32)
        state_scr[...] = jnp.zeros_like(state_scr)

    def conv_silu(x_ref, scr, widx):
        scr[8:8 + c, :] = x_ref[...].astype(F32)
        acc = cw_ref[widx, GDN_CONV - 1:GDN_CONV, :] * scr[8:8 + c, :]
        for j in range(GDN_CONV - 1):
            s = GDN_CONV - 1 - j
            acc = acc + cw_ref[widx, j:j + 1, :] * scr[8 - s:8 - s + c, :]
        scr[0:8, :] = scr[c:c + 8, :]
        return _silu(acc)

    def l2n(x):
        return x * lax.rsqrt(jnp.sum(x * x, axis=-1, keepdims=True) + 1e-6)

    def bf(x):
        return x.astype(BF16)

    qa = conv_silu(gq_ref, xq_scr, 0)
    ka = conv_silu(gk_ref, xk_scr, 1)
    va = conv_silu(gv_ref, xv_scr, 2)
    q = [l2n(qa[:, s]) * (GDN_DK ** -0.5) for s in hs]
    k = [l2n(ka[:, s]) for s in hs]
    v = [va[:, s] for s in hs]

    sm = sm_ref[...].astype(F32)
    lane = lax.broadcasted_iota(jnp.int32, (c, LANES), 1)
    beta = [_sigmoid(jnp.sum(jnp.where(lane == h, sm, 0.0), axis=-1, keepdims=True)) for h in heads]
    ga = [jnp.sum(jnp.where(lane == GDN_HEADS + h, sm, 0.0), axis=-1, keepdims=True) for h in heads]
    g = [-jnp.exp(alog_ref[:, hs[h]]) * _softplus(jnp.broadcast_to(ga[h], (c, LANES)) + dtb_ref[:, hs[h]])
         for h in heads]

    ri = lax.broadcasted_iota(jnp.int32, (c, c), 0)
    cj = lax.broadcasted_iota(jnp.int32, (c, c), 1)
    gam = [_cumsum_rows(x) for x in g]
    decay = [jnp.exp(jnp.where(cj <= ri, x - x.T, -1e30)) for x in gam]
    eg = [jnp.exp(x) for x in gam]
    gl = [x[c - 1:c, :] for x in gam]
    kb = [k[h] * beta[h] for h in heads]
    vb = [v[h] * beta[h] for h in heads]

    a_mat = [jnp.where(cj < ri, _dot_nt_x3(kb[h], k[h]) * decay[h], 0.0) for h in heads]
    tinv = _unit_lower_inverse(a_mat, 7, _dot_x3)
    rhs = [jnp.concatenate([kb[h] * eg[h], vb[h]], axis=1) for h in heads]
    wu = [bf(_dot_x3(tinv[h], rhs[h])) for h in heads]
    attn = [_dot_nt(bf(q[h]), bf(k[h])) * decay[h] for h in heads]
    k_dec = [k[h] * jnp.exp(gl[h] - gam[h]) for h in heads]
    lhs = [bf(jnp.concatenate([attn[h], k_dec[h].T], axis=0)) for h in heads]
    z = [_dot(lhs[h], wu[h]) for h in heads]
    state = [state_scr[h] for h in heads]
    l2 = [bf(jnp.concatenate([q[h] * eg[h] - z[h][:c, :LANES], -z[h][c:, :LANES]], axis=0)) for h in heads]
    rr = [_dot(l2[h], bf(state[h])) for h in heads]
    for h in heads:
        state_scr[h] = state[h] * jnp.exp(gl[h]) + rr[h][c:] + z[h][c:, LANES:]
        o = rr[h][:c] + z[h][:c, LANES:]
        o_ref[:, hs[h]] = (_rms(o, on_ref[...]) * _silu(gz_ref[:, hs[h]].astype(F32))).astype(o_ref.dtype)


def gdn_mixer(hcat, conv_w, a_log, dt_bias, out_norm, batch, seq):
    c = GDN_CHUNK
    assert c == LANES and seq % c == 0
    nc = seq // c
    width = GDN_HEADS * LANES
    blk = lambda off: pl.BlockSpec((c, width), lambda b, i, off=off: (b * nc + i, off // width))
    const2 = lambda b, i: (0, 0)
    cw = conv_w.reshape(GDN_CONV, 3, width).transpose(1, 0, 2)
    alog = jnp.repeat(a_log.astype(F32), LANES).reshape(1, width)
    dtb = jnp.repeat(dt_bias.astype(F32), LANES).reshape(1, width)
    return pl.pallas_call(
        functools.partial(_gdn_kernel, chunk=c),
        grid=(batch, nc),
        in_specs=[blk(HC_GQ), blk(HC_GK), blk(HC_GV), blk(HC_GZ),
                  pl.BlockSpec((c, LANES), lambda b, i: (b * nc + i, HC_SM // LANES)),
                  pl.BlockSpec((3, GDN_CONV, width), lambda b, i: (0, 0, 0)),
                  pl.BlockSpec((1, width), const2),
                  pl.BlockSpec((1, width), const2),
                  pl.BlockSpec((1, GDN_DV), const2)],
        out_specs=pl.BlockSpec((c, width), lambda b, i: (b * nc + i, 0)),
        out_shape=jax.ShapeDtypeStruct((batch * seq, width), BF16),
        scratch_shapes=[pltpu.VMEM((c + 8, width), F32)] * 3 + [pltpu.VMEM((GDN_HEADS, GDN_DK, GDN_DV), F32)],
        compiler_params=_cparams(("parallel", "arbitrary")),
        name="gdn_mixer",
    )(hcat, hcat, hcat, hcat, hcat, cw, alog, dtb, out_norm)


LZ_W, LZ_A, LZ_G, LZ_V, LZ_COLS = 0, 128, 256, 512, 640


def _rwkv_mix_kernel(h_ref, hp_ref, g_ref, mix_ref, a1_ref, o_ref, z_ref, *, tm, seq, has_v):
    i = pl.program_id(0)
    g = g_ref[...]
    xn = _rms(h_ref[...], g)
    prev_last = _rms(hp_ref[...], g)[7:8, :]
    prev_last = jnp.where((i * tm) % seq == 0, 0.0, prev_last)
    rows = lax.broadcasted_iota(jnp.int32, xn.shape, 0)
    xprev = jnp.where(rows == 0, prev_last, pltpu.roll(xn, 1, 0))
    xx = xprev - xn

    def mixed(j):
        return (xn + xx * mix_ref[j:j + 1, :]).astype(BF16)

    o_ref[0] = mixed(0)
    o_ref[1] = mixed(2)
    xv = mixed(3)
    o_ref[2] = xv
    z_ref[:, LZ_W:LZ_A] = jnp.tanh(_dot(mixed(1), a1_ref[:, LZ_W:LZ_A])).astype(BF16)
    z_ref[:, LZ_A:LZ_G] = _dot(mixed(4), a1_ref[:, LZ_A:LZ_G]).astype(BF16)
    z_ref[:, LZ_G:LZ_V] = _sigmoid(_dot(mixed(5), a1_ref[:, LZ_G:LZ_V])).astype(BF16)
    if has_v:
        z_ref[:, LZ_V:] = _dot(xv, a1_ref[:, LZ_V:]).astype(BF16)
    else:
        z_ref[:, LZ_V:] = jnp.zeros((tm, LZ_COLS - LZ_V), BF16)


def rwkv_mix(h, g, mix, a1_cat, seq, *, has_v, tm=512):
    m, d = h.shape
    tm = _tile(seq, tm)
    return pl.pallas_call(
        functools.partial(_rwkv_mix_kernel, tm=tm, seq=seq, has_v=has_v),
        grid=(m // tm,),
        in_specs=[pl.BlockSpec((tm, d), lambda i: (i, 0)),
                  pl.BlockSpec((8, d), lambda i: (jnp.maximum(i * (tm // 8) - 1, 0), 0)),
                  pl.BlockSpec((1, d), lambda i: (0, 0)),
                  pl.BlockSpec((6, d), lambda i: (0, 0)),
                  pl.BlockSpec((d, LZ_COLS), lambda i: (0, 0))],
        out_specs=[pl.BlockSpec((3, tm, d), lambda i: (0, i, 0)),
                   pl.BlockSpec((tm, LZ_COLS), lambda i: (i, 0))],
        out_shape=[jax.ShapeDtypeStruct((3, m, d), BF16),
                   jax.ShapeDtypeStruct((m, LZ_COLS), BF16)],
        compiler_params=_cparams(("parallel",)),
        name="rwkv_mix",
    )(h, h, g, mix, a1_cat)


def _lora_up_kernel(z_ref, b_ref, bias_ref, *rest, has_v):
    if has_v:
        v_ref, vf_ref, lw_ref, a_ref, gt_ref, vo_ref = rest
    else:
        lw_ref, a_ref, gt_ref = rest
    y = _dot(z_ref[:, LZ_W:LZ_A], b_ref[LZ_W:LZ_A, :]) + bias_ref[0:1, :]
    lw_ref[...] = -jnp.exp(-_softplus(-y) - 0.5)
    y = _dot(z_ref[:, LZ_A:LZ_G], b_ref[LZ_A:LZ_G, :]) + bias_ref[1:2, :]
    a_ref[...] = _sigmoid(y).astype(a_ref.dtype)
    gt_ref[...] = _dot(z_ref[:, LZ_G:LZ_V], b_ref[LZ_G:LZ_V, :]).astype(gt_ref.dtype)
    if has_v:
        y = _dot(z_ref[:, LZ_V:], b_ref[LZ_V:, :]) + bias_ref[2:3, :]
        v = v_ref[...].astype(F32)
        vo_ref[...] = (v + (vf_ref[...].astype(F32) - v) * _sigmoid(y)).astype(vo_ref.dtype)


def lora_up(z, b_cat, bias, v=None, v_first=None, *, tm=512):
    m = z.shape[0]
    d = b_cat.shape[1]
    has_v = v is not None
    tm = _tile(m, tm)
    row = lambda i: (i, 0)
    const = lambda i: (0, 0)
    big = pl.BlockSpec((tm, d), row)
    n_bf = 3 if has_v else 2
    return pl.pallas_call(
        functools.partial(_lora_up_kernel, has_v=has_v),
        grid=(m // tm,),
        in_specs=[pl.BlockSpec((tm, LZ_COLS), row), pl.BlockSpec((LZ_COLS, d), const),
                  pl.BlockSpec((3, d), const)] + ([big, big] if has_v else []),
        out_specs=[big] * (1 + n_bf),
        out_shape=[jax.ShapeDtypeStruct((m, d), F32)] + [jax.ShapeDtypeStruct((m, d), BF16)] * n_bf,
        compiler_params=_cparams(("parallel",)),
        name="lora_up",
    )(z, b_cat, bias, *((v, v_first) if has_v else ()))


def _rwkv_kernel(r_ref, k_ref, v_ref, a_ref, lw_ref, gt_ref, kk_ref, ka_ref, rk_ref, lnw_ref, lnb_ref,
                 o_ref, state_scr, *, chunk, pairs):
    c = chunk
    ps = range(pairs)
    sls = [slice(p * LANES, (p + 1) * LANES) for p in ps]

    @pl.when(pl.program_id(2) == 0)
    def _():
        state_scr[...] = jnp.zeros_like(state_scr)

    lane = lax.broadcasted_iota(jnp.int32, (c, LANES), 1)
    m0 = lane < RWKV_HEAD
    r2 =lax.broadcasted_iota(jnp.int32, (2 * c, 2 * c), 0)
    c2 = lax.broadcasted_iota(jnp.int32, (2 * c, 2 * c), 1)
    strict2 = c2 < r2
    rw = lax.broadcasted_iota(jnp.int32, (c, 2 * c), 0)
    cw = lax.broadcasted_iota(jnp.int32, (c, 2 * c), 1)
    incl_w = jnp.where(cw >= c, cw - c, cw) <= rw
    zero_blk = jnp.zeros((2 * c, LANES), BF16)

    def seg_sum(x):
        s0 = jnp.sum(jnp.where(m0, x, 0.0), axis=-1, keepdims=True)
        s1 = jnp.sum(jnp.where(m0, 0.0, x), axis=-1, keepdims=True)
        return jnp.where(m0, s0, s1)

    def stack(x):
        return jnp.concatenate([jnp.where(m0, x, 0.0), jnp.where(m0, 0.0, x)], axis=0)

    def bf(x):
        return x.astype(BF16)

    r = [r_ref[:, s].astype(F32) for s in sls]
    k = [k_ref[:, s].astype(F32) for s in sls]
    v = [v_ref[:, s].astype(F32) for s in sls]
    a = [a_ref[:, s].astype(F32) for s in sls]
    lw = [lw_ref[:, s] for s in sls]
    kk = [k[p] * kk_ref[:, sls[p]] for p in ps]
    kk = [x * lax.rsqrt(seg_sum(x * x) + 1e-12) for x in kk]
    kh = [k[p] * (1.0 + (a[p] - 1.0) * ka_ref[:, sls[p]]) for p in ps]
    bv = [kk[p] * a[p] for p in ps]

    cum = [_cumsum_rows(x) for x in lw]
    cl = [x[c - 1:c, :] for x in cum]
    e_neg = [jnp.exp(-x) for x in cum]
    e_end = [jnp.exp(cl[p] - cum[p]) for p in ps]
    rg = [r[p] * jnp.exp(cum[p]) for p in ps]
    rgc = [bf(x) for x in rg]
    xs = [bf(stack(kk[p] * jnp.exp(cum[p] - lw[p]))) for p in ps]
    bs = [bf(stack(bv[p] * e_neg[p])) for p in ps]
    ks = [bf(stack(kh[p] * e_neg[p])) for p in ps]
    vs = [bf(stack(v[p])) for p in ps]
    kht = [stack(kh[p] * e_end[p]).T for p in ps]
    bht = [stack(bv[p] * e_end[p]).T for p in ps]

    ab = [_dot_nt(xs[p], jnp.concatenate([bs[p], ks[p]], axis=0)) for p in ps]
    akk = [jnp.where(strict2, x[:, :2 * c], 0.0) for x in ab]
    akv = [bf(jnp.where(strict2, x[:, 2 * c:], 0.0)) for x in ab]
    tinv = [bf(x) for x in _unit_lower_inverse(akk, 6, _dot_bf)]
    w1 = [bf(_dot(akv[p], vs[p])) for p in ps]
    m12 = [bf(_dot(tinv[p], jnp.concatenate([xs[p], w1[p]], axis=1))) for p in ps]
    bb = [_dot_nt(rgc[p], jnp.concatenate([ks[p], bs[p]], axis=0)) for p in ps]
    lhs = [bf(jnp.concatenate([
        jnp.concatenate([jnp.where(incl_w, x[:, :2 * c], 0.0), -jnp.where(incl_w, x[:, 2 * c:], 0.0)], axis=1),
        jnp.concatenate([kht[p], -bht[p]], axis=1)], axis=0)) for p, x in enumerate(bb)]
    rhs = [jnp.concatenate([jnp.concatenate([vs[p], zero_blk], axis=1),
                            jnp.concatenate([m12[p][:, LANES:], m12[p][:, :LANES]], axis=1)], axis=0)
           for p in ps]
    z = [_dot(lhs[p], rhs[p]) for p in ps]
    state = [state_scr[p] for p in ps]
    l2 = [bf(jnp.concatenate([rg[p] + z[p][:c, LANES:], z[p][c:, LANES:]], axis=0)) for p in ps]
    rr = [_dot(l2[p], bf(state[p])) for p in ps]
    for p in ps:
        sl = sls[p]
        state_scr[p] = state[p] * _col_of_row(jnp.exp(cl[p]), LANES) + rr[p][c:] + z[p][c:, :LANES]
        y = rr[p][:c] + z[p][:c, :LANES]
        mu = seg_sum(y) * (1.0 / RWKV_HEAD)
        yc = y - mu
        var = seg_sum(yc * yc) * (1.0 / RWKV_HEAD)
        yn = yc * lax.rsqrt(var + RWKV_LN_EPS) * lnw_ref[:, sl] + lnb_ref[:, sl]
        bonus = seg_sum(r[p] * kh[p] * rk_ref[:, sl]) * v[p]
        o_ref[:, sl] = ((yn + bonus) * gt_ref[:, sl].astype(F32)).astype(o_ref.dtype)


def rwkv_recurrence(r, k, v, a, lw, gate, k_k, k_a, r_k, ln_w, ln_b, batch, seq, *, pairs=16):
    c = RWKV_CHUNK
    m, d = r.shape
    nc = seq // c
    width = pairs * LANES
    blk = pl.BlockSpec((c, width), lambda b, hp, i: (b * nc + i, hp))
    par = pl.BlockSpec((1, width), lambda b, hp, i: (0, hp))
    return pl.pallas_call(
        functools.partial(_rwkv_kernel, chunk=c, pairs=pairs),
        grid=(batch, d // width, nc),
        in_specs=[blk] * 6 + [par] * 5,
        out_specs=blk,
        out_shape=jax.ShapeDtypeStruct((m, d), BF16),
        scratch_shapes=[pltpu.VMEM((pairs, LANES, LANES), F32)],
        compiler_params=_cparams(("parallel", "parallel", "arbitrary")),
        name="rwkv_recurrence",
    )(r, k, v, a, lw, gate, k_k, k_a, r_k, ln_w, ln_b)


def _pad_cols(w, n):
    return jnp.pad(w, ((0, 0), (0, n - w.shape[1])))


def _rot_cols(w):
    half = MLA_ROPE // 2
    return jnp.concatenate([-w[..., half:], w[..., :half]], axis=-1)


def _input_projection_weight(w_in):
    o_kr = MLA_Q_LORA + MLA_KV_LORA
    o_g = o_kr + MLA_ROPE
    o_sm = o_g + 4 * GDN_HEADS * GDN_DK
    w_in = w_in.astype(BF16)
    w_kr = w_in[..., o_kr:o_g]
    small = w_in[..., o_sm:]
    small = jnp.pad(small, ((0, 0), (0, 0), (0, LANES - small.shape[-1])))
    return jnp.concatenate([w_in[..., :o_kr], w_in[..., o_g:o_sm], w_kr, _rot_cols(w_kr), small], axis=-1)


def _hybrid_layer(h, ct, st, batch, seq, g_pre, g_post, w_cat, q_norm, w_uq, kv_norm, w_ukv, conv_w, a_log,
                  dt_bias, out_norm, w_out, e):
    d = D_MODEL
    hcat = norm_matmul(h, g_pre.reshape(1, d), w_cat, e, tn=1792, out_dtype=BF16)

    wq = w_uq.reshape(MLA_Q_LORA, MLA_HEADS, MLA_QK)
    wq_rope = wq[..., MLA_NOPE:]
    wq = jnp.concatenate([wq[..., :MLA_NOPE], wq_rope, _rot_cols(wq_rope)], axis=-1)
    wq = wq.reshape(MLA_Q_LORA, MLA_HEADS * MLA_HEAD_PAD).astype(BF16)
    q, k, v = mla_proj(hcat, q_norm.reshape(1, -1), kv_norm.reshape(1, -1), wq, w_ukv.astype(BF16), ct, st)
    mla_out = mla_attention(q, k, v, batch, seq)
    gdn_out = gdn_mixer(hcat, conv_w, a_log, dt_bias, out_norm.reshape(1, -1), batch, seq)
    return matmul_norm_residual([mla_out, gdn_out], w_out, e, g_post.reshape(1, d), h, tn=d)


def _rwkv_layer(h, v_first, batch, seq, g_pre, g_post, mix, w_r, w_k, w_v, w_o, o, w0, w1, w2, a0, a1, a2,
                g1, g2, k_k, k_a, r_k, ln_w, ln_b, vres):
    d = D_MODEL
    row = lambda t: t.reshape(1, d).astype(F32)
    padc = lambda w, n: _pad_cols(w, n)
    padr = lambda w, n: jnp.pad(w, ((0, n - w.shape[0]), (0, 0)))
    has_v = vres is not None
    v0, v1, v2 = vres if has_v else (jnp.zeros((d,), F32), jnp.zeros((d, 0), F32), jnp.zeros((0, d), F32))
    a1_cat = jnp.concatenate([padc(w1, LZ_A - LZ_W), padc(a1, LZ_G - LZ_A), padc(g1, LZ_V - LZ_G),
                              padc(v1, LZ_COLS - LZ_V)], axis=1).astype(BF16)
    b_cat = jnp.concatenate([padr(w2, LZ_A - LZ_W), padr(a2, LZ_G - LZ_A), padr(g2, LZ_V - LZ_G),
                             padr(v2, LZ_COLS - LZ_V)], axis=0).astype(BF16)
    bias = jnp.stack([w0, a0, v0]).astype(F32)
    xs, z = rwkv_mix(h, g_pre.reshape(1, d), mix, a1_cat, seq, has_v=has_v)
    r = matmul(xs, 0, w_r, o, out_dtype=BF16)
    k = matmul(xs, 1, w_k, o, out_dtype=BF16)
    v = matmul(xs, 2, w_v, o, out_dtype=BF16)
    if has_v:
        lw, a, gate, v = lora_up(z, b_cat, bias, v, v_first)
    else:
        lw, a, gate = lora_up(z, b_cat, bias)
        v_first = v
    y = rwkv_recurrence(r, k, v, a, lw, gate, row(k_k), row(k_a), row(r_k), row(ln_w), row(ln_b), batch, seq)
    return matmul_norm_residual([y], w_o, o, g_post.reshape(1, d), h, tn=d), v_first


def _ffn_layer(h, g_pre, g_post, w_gate, w_up, w_down, layer):
    d = D_MODEL
    act = ffn_up(h, g_pre.reshape(1, d), w_gate, w_up, layer)
    return matmul_norm_residual([act], w_down, layer, g_post.reshape(1, d), h)


def _rope_tables(positions):
    inv_freq = 1.0 / (ROPE_THETA ** (jnp.arange(0, MLA_ROPE, 2, dtype=F32) / MLA_ROPE))
    ang = positions.astype(F32).reshape(-1, 1) * inv_freq
    zeros = jnp.zeros((ang.shape[0], LANES - MLA_ROPE), F32)
    cos, sin = jnp.cos(ang), jnp.sin(ang)
    return jnp.concatenate([cos, cos, zeros], axis=1), jnp.concatenate([sin, sin, zeros], axis=1)


def kernel(x, positions, norm_mix_pre, norm_mix_post, norm_ffn_pre, norm_ffn_post, hyb_w_in, mla_q_norm, mla_w_uq, mla_kv_norm, mla_w_ukv, gdn_conv_w, gdn_a_log, gdn_dt_bias, gdn_out_norm, hyb_w_out, rwkv_mix, rwkv_w_r, rwkv_w_k, rwkv_w_v, rwkv_w_o, rwkv_w0, rwkv_w1, rwkv_w2, rwkv_a0, rwkv_a1, rwkv_a2, rwkv_g1, rwkv_g2, rwkv_k_k, rwkv_k_a, rwkv_r_k, rwkv_ln_w, rwkv_ln_b, rwkv_v0, rwkv_v1, rwkv_v2, ffn_w_gate, ffn_w_up, ffn_w_down):
    batch, seq, d = x.shape
    h = x.reshape(batch * seq, d)
    ct, st = _rope_tables(positions)
    bf = lambda w: w.astype(BF16)
    ffn_w_gate, ffn_w_up, ffn_w_down, hyb_w_out = bf(ffn_w_gate), bf(ffn_w_up), bf(ffn_w_down), bf(hyb_w_out)
    rwkv_w_r, rwkv_w_k, rwkv_w_v, rwkv_w_o = bf(rwkv_w_r), bf(rwkv_w_k), bf(rwkv_w_v), bf(rwkv_w_o)
    hyb_w_cat = _input_projection_weight(hyb_w_in)
    v_first = None
    for layer in range(DEPTH):
        if layer % 2 == 0:
            e = layer // 2
            h = _hybrid_layer(h, ct, st, batch, seq, norm_mix_pre[layer], norm_mix_post[layer], hyb_w_cat,
                              mla_q_norm[e], mla_w_uq[e], mla_kv_norm[e], mla_w_ukv[e], gdn_conv_w[e],
                              gdn_a_log[e], gdn_dt_bias[e], gdn_out_norm[e], hyb_w_out, e)
        else:
            o = layer // 2
            vres = None if o == 0 else (rwkv_v0[o - 1], rwkv_v1[o - 1], rwkv_v2[o - 1])
            h, v_first = _rwkv_layer(h, v_first, batch, seq, norm_mix_pre[layer], norm_mix_post[layer],
                                     rwkv_mix[o], rwkv_w_r, rwkv_w_k, rwkv_w_v, rwkv_w_o, o,
                                     rwkv_w0[o], rwkv_w1[o], rwkv_w2[o], rwkv_a0[o], rwkv_a1[o], rwkv_a2[o],
                                     rwkv_g1[o], rwkv_g2[o], rwkv_k_k[o], rwkv_k_a[o], rwkv_r_k[o],
                                     rwkv_ln_w[o], rwkv_ln_b[o], vres)
        h = _ffn_layer(h, norm_ffn_pre[layer], norm_ffn_post[layer], ffn_w_gate, ffn_w_up, ffn_w_down, layer)
    return h.reshape(batch, seq, d)
```

```python
import functools

import jax
import jax.numpy as jnp
import numpy as np
from jax import lax
from jax.experimental import pallas as pl
from jax.experimental.pallas import tpu as pltpu

F32 = jnp.float32
BF16 = jnp.bfloat16
HI = lax.Precision.HIGHEST

D_MODEL = 2048
DEPTH = 4
LANES = 128
VMEM_LIMIT = 56 * 1024 * 1024

MLA_HEADS = 8
MLA_Q_LORA = 512
MLA_KV_LORA = 512
MLA_NOPE = 128
MLA_ROPE = 64
MLA_V = 128
MLA_QK = MLA_NOPE + MLA_ROPE
MLA_HEAD_PAD = 256
ROPE_THETA = 10000.0

GDN_HEADS = 8
GDN_DK = 128
GDN_DV = 128
GDN_CONV = 4
GDN_CHUNK = 128

RWKV_HEAD = 64
RWKV_HEADS = D_MODEL // RWKV_HEAD
RWKV_CHUNK = 64
RWKV_LN_EPS = 64e-5
RMS_EPS = 1e-6

HC_CQ, HC_CKV, HC_GQ, HC_GK, HC_GV, HC_GZ, HC_KR, HC_SM = 0, 512, 1024, 2048, 3072, 4096, 5120, 5248
HC_COLS = 5376


def _cparams(sem):
    return pltpu.CompilerParams(dimension_semantics=sem, vmem_limit_bytes=VMEM_LIMIT)


def _tile(n, pref):
    t = min(n, pref)
    while n % t:
        t -= 8
    return t


def _layer_spec(w, layer, blk, imap):
    return pl.BlockSpec((None,) + blk, lambda *g: (layer,) + imap(*g))


def _dot(a, b, prec=None):
    return jnp.dot(a, b, preferred_element_type=F32, precision=prec)


def _dot_nt(a, b, prec=None):
    return lax.dot_general(a, b, (((1,), (1,)), ((), ())), preferred_element_type=F32, precision=prec)


def _rms(x, g, eps=RMS_EPS):
    return x * lax.rsqrt(jnp.mean(x * x, axis=-1, keepdims=True) + eps) * g


def _sigmoid(x):
    return 1.0 / (1.0 + jnp.exp(-x))


def _silu(x):
    return x * _sigmoid(x)


def _softplus(x):
    return jnp.maximum(x, 0.0) + jnp.log(1.0 + jnp.exp(-jnp.abs(x)))


def _norm_mm_kernel(x_ref, g_ref, w_ref, o_ref, xn_ref):
    @pl.when(pl.program_id(1) == 0)
    def _():
        xn_ref[...] = _rms(x_ref[...], g_ref[...]).astype(BF16)

    o_ref[...] = _dot(xn_ref[...], w_ref[...]).astype(o_ref.dtype)


def norm_matmul(x, g, w, layer, *, tn, out_dtype=F32, tm=1024):
    m, k = x.shape
    n = w.shape[2]
    tm = _tile(m, tm)
    return pl.pallas_call(
        _norm_mm_kernel,
        grid=(m // tm, n // tn),
        in_specs=[pl.BlockSpec((tm, k), lambda i, j: (i, 0)),
                  pl.BlockSpec((1, k), lambda i, j: (0, 0)),
                  _layer_spec(w, layer, (k, tn), lambda i, j: (0, j))],
        out_specs=pl.BlockSpec((tm, tn), lambda i, j: (i, j)),
        out_shape=jax.ShapeDtypeStruct((m, n), out_dtype),
        scratch_shapes=[pltpu.VMEM((tm, k), BF16)],
        compiler_params=_cparams(("parallel", "arbitrary")),
        name="norm_matmul",
    )(x, g, w)


def _ffn_up_kernel(x_ref, g_ref, wg_ref, wu_ref, o_ref, xn_ref):
    @pl.when(pl.program_id(1) == 0)
    def _():
        xn_ref[...] = _rms(x_ref[...], g_ref[...]).astype(BF16)

    xn = xn_ref[...]
    gate = _dot(xn, wg_ref[...])
    up = _dot(xn, wu_ref[...])
    o_ref[...] = (_silu(gate) * up).astype(o_ref.dtype)


def ffn_up(h, g, wg, wu, layer, *, tm=1024, tn=512):
    m, k = h.shape
    n = wg.shape[2]
    tm = _tile(m, tm)
    tn = _tile(n, tn)
    return pl.pallas_call(
        _ffn_up_kernel,
        grid=(m // tm, n // tn),
        in_specs=[pl.BlockSpec((tm, k), lambda i, j: (i, 0)),
                  pl.BlockSpec((1, k), lambda i, j: (0, 0)),
                  _layer_spec(wg, layer, (k, tn), lambda i, j: (0, j)),
                  _layer_spec(wu, layer, (k, tn), lambda i, j: (0, j))],
        out_specs=pl.BlockSpec((tm, tn), lambda i, j: (i, j)),
        out_shape=jax.ShapeDtypeStruct((m, n), BF16),
        scratch_shapes=[pltpu.VMEM((tm, k), BF16)],
        compiler_params=_cparams(("parallel", "arbitrary")),
        name="ffn_up",
    )(h, g, wg, wu)


def _mm_norm_res_kernel(*refs, tn, nb, n_a):
    a_refs = refs[:n_a]
    w_ref, g_ref, h_ref, o_ref, f_ref = refs[n_a:]
    j = pl.program_id(1)
    k0 = 0
    f = None
    for a_ref in a_refs:
        part = _dot(a_ref[...], w_ref[k0:k0 + a_ref.shape[1], :])
        f = part if f is None else f + part
        k0 += a_ref.shape[1]
    f_ref[j] = f

    @pl.when(j == nb - 1)
    def _():
        ssq = jnp.sum(f_ref[0] * f_ref[0], axis=-1, keepdims=True)
        for b in range(1, nb):
            ssq = ssq + jnp.sum(f_ref[b] * f_ref[b], axis=-1, keepdims=True)
        inv = lax.rsqrt(ssq * (1.0 / (nb * tn)) + RMS_EPS)
        for b in range(nb):
            cs = slice(b * tn, (b + 1) * tn)
            o_ref[:, cs] = h_ref[:, cs] + f_ref[b] * inv * g_ref[:, cs]


def matmul_norm_residual(a_parts, w, layer, g, h, *, tm=512, tn=512):
    m = a_parts[0].shape[0]
    _, k, n = w.shape
    tm = _tile(m, tm)
    tn = _tile(n, tn)
    nb = n // tn
    return pl.pallas_call(
        functools.partial(_mm_norm_res_kernel, tn=tn, nb=nb, n_a=len(a_parts)),
        grid=(m // tm, nb),
        in_specs=[pl.BlockSpec((tm, a.shape[1]), lambda i, j: (i, 0)) for a in a_parts] + [
                  _layer_spec(w, layer, (k, tn), lambda i, j: (0, j)),
                  pl.BlockSpec((1, n), lambda i, j: (0, 0)),
                  pl.BlockSpec((tm, n), lambda i, j: (i, 0))],
        out_specs=pl.BlockSpec((tm, n), lambda i, j: (i, 0)),
        out_shape=jax.ShapeDtypeStruct((m, n), F32),
        scratch_shapes=[pltpu.VMEM((nb, tm, tn), F32)],
        compiler_params=_cparams(("parallel", "arbitrary")),
        name="matmul_norm_residual",
    )(*a_parts, w, g, h)


def _mm_kernel(a_ref, w_ref, o_ref):
    o_ref[...] = _dot(a_ref[...], w_ref[...]).astype(o_ref.dtype)


def matmul(a, idx, w, layer, *, tm=1024, tn=1024, out_dtype=F32):
    _, m, k = a.shape
    n = w.shape[2]
    tm = _tile(m, tm)
    tn = _tile(n, tn)
    return pl.pallas_call(
        _mm_kernel,
        grid=(m // tm, n // tn),
        in_specs=[pl.BlockSpec((None, tm, k), lambda i, j: (idx, i, 0)),
                  _layer_spec(w, layer, (k, tn), lambda i, j: (0, j))],
        out_specs=pl.BlockSpec((tm, tn), lambda i, j: (i, j)),
        out_shape=jax.ShapeDtypeStruct((m, n), out_dtype),
        compiler_params=_cparams(("parallel", "arbitrary")),
        name="matmul",
    )(a, w)


def _rope_group(grp, ct, st):
    return grp * ct + pltpu.roll(grp, 64, 1) * st


def _mla_proj_kernel(cq_ref, ckv_ref, kr_ref, qn_ref, kvn_ref, wq_ref, wkv_ref, ct_ref, st_ref,
                     q_ref, k_ref, v_ref):
    scale = MLA_QK ** -0.5 * np.log2(np.e)
    ct = ct_ref[...]
    st = st_ref[...]
    xq = _rms(cq_ref[...].astype(F32), qn_ref[...]).astype(BF16)
    xkv = _rms(ckv_ref[...].astype(F32), kvn_ref[...]).astype(BF16)
    kr = _rope_group(kr_ref[...].astype(F32), ct, st).astype(BF16)
    for h in range(MLA_HEADS):
        c0 = h * MLA_HEAD_PAD
        yq = _dot(xq, wq_ref[:, c0:c0 + MLA_HEAD_PAD])
        q_ref[:, c0:c0 + LANES] = (yq[:, :LANES] * scale).astype(BF16)
        q_ref[:, c0 + LANES:c0 + 2 * LANES] = (_rope_group(yq[:, LANES:], ct, st) * scale).astype(BF16)
        ykv = _dot(xkv, wkv_ref[:, c0:c0 + MLA_HEAD_PAD])
        k_ref[:, c0:c0 + LANES] = ykv[:, :LANES].astype(BF16)
        k_ref[:, c0 + LANES:c0 + 2 * LANES] = kr
        v_ref[:, h * MLA_V:(h + 1) * MLA_V] = ykv[:, LANES:].astype(BF16)


def mla_proj(hcat, q_norm, kv_norm, wq, wkv, ct, st, *, tm=1024):
    m = hcat.shape[0]
    tm = _tile(m, tm)
    hp = MLA_HEADS * MLA_HEAD_PAD
    row = lambda i: (i, 0)
    const = lambda i: (0, 0)
    return pl.pallas_call(
        _mla_proj_kernel,
        grid=(m // tm,),
        in_specs=[pl.BlockSpec((tm, MLA_Q_LORA), lambda i: (i, HC_CQ // MLA_Q_LORA)),
                  pl.BlockSpec((tm, MLA_KV_LORA), lambda i: (i, HC_CKV // MLA_KV_LORA)),
                  pl.BlockSpec((tm, LANES), lambda i: (i, HC_KR // LANES)),
                  pl.BlockSpec((1, MLA_Q_LORA), const),
                  pl.BlockSpec((1, MLA_KV_LORA), const),
                  pl.BlockSpec((MLA_Q_LORA, hp), const),
                  pl.BlockSpec((MLA_KV_LORA, hp), const),
                  pl.BlockSpec((tm, LANES), row),
                  pl.BlockSpec((tm, LANES), row)],
        out_specs=[pl.BlockSpec((tm, hp), row),
                   pl.BlockSpec((tm, hp), row),
                   pl.BlockSpec((tm, MLA_HEADS * MLA_V), row)],
        out_shape=[jax.ShapeDtypeStruct((m, hp), BF16),
                   jax.ShapeDtypeStruct((m, hp), BF16),
                   jax.ShapeDtypeStruct((m, MLA_HEADS * MLA_V), BF16)],
        compiler_params=_cparams(("parallel",)),
        name="mla_proj",
    )(hcat, hcat, hcat, q_norm, kv_norm, wq, wkv, ct, st)


def _attn_kernel(q_ref, k_ref, v_ref, o_ref, *, tq, tk, hpg):
    qi = pl.program_id(2)
    hs = range(hpg)
    q = [q_ref[:, h * MLA_HEAD_PAD:(h + 1) * MLA_HEAD_PAD] for h in hs]

    def step(ki, carry, masked):
        start = pl.multiple_of(ki * tk, tk)
        s = [_dot_nt(q[h], k_ref[pl.ds(start, tk), h * MLA_HEAD_PAD:(h + 1) * MLA_HEAD_PAD]) for h in hs]
        if masked is not None:
            row = lax.broadcasted_iota(jnp.int32, (tq, tk), 0)
            col = lax.broadcasted_iota(jnp.int32, (tq, tk), 1) + masked * tk
            s = [jnp.where(col <= row, x, -1e30) for x in s]
        m_new = [jnp.maximum(carry[h][0], jnp.max(s[h], axis=-1, keepdims=True)) for h in hs]
        alpha = [jnp.exp2(carry[h][0] - m_new[h]) for h in hs]
        p = [jnp.exp2(s[h] - m_new[h]) for h in hs]
        l_new = [alpha[h] * carry[h][1] + jnp.sum(p[h], axis=-1, keepdims=True) for h in hs]
        pv = [_dot(p[h].astype(BF16), v_ref[pl.ds(start, tk), h * MLA_V:(h + 1) * MLA_V]) for h in hs]
        return tuple((m_new[h], l_new[h], alpha[h] * carry[h][2] + pv[h]) for h in hs)

    init = tuple((jnp.full((tq, 1), -1e30, F32), jnp.zeros((tq, 1), F32), jnp.zeros((tq, MLA_V), F32))
                 for _ in hs)
    per_q = tq // tk
    carry = lax.fori_loop(0, qi * per_q, lambda ki, c: step(ki, c, None), init)
    for j in range(per_q):
        carry = step(qi * per_q + j, carry, j)
    for h in hs:
        o_ref[:, h * MLA_V:(h + 1) * MLA_V] = (carry[h][2] / carry[h][1]).astype(o_ref.dtype)


def mla_attention(q, k, v, batch, seq, *, tq=1024, tk=1024, hpg=2):
    tq = _tile(seq, tq)
    tk = _tile(tq, tk)
    nq = seq // tq
    return pl.pallas_call(
        functools.partial(_attn_kernel, tq=tq, tk=tk, hpg=hpg),
        grid=(batch, MLA_HEADS // hpg, nq),
        in_specs=[pl.BlockSpec((tq, hpg * MLA_HEAD_PAD), lambda b, h, i: (b * nq + i, h)),
                  pl.BlockSpec((seq, hpg * MLA_HEAD_PAD), lambda b, h, i: (b, h)),
                  pl.BlockSpec((seq, hpg * MLA_V), lambda b, h, i: (b, h))],
        out_specs=pl.BlockSpec((tq, hpg * MLA_V), lambda b, h, i: (b * nq + i, h)),
        out_shape=jax.ShapeDtypeStruct((batch * seq, MLA_HEADS * MLA_V), BF16),
        compiler_params=_cparams(("parallel", "parallel", "arbitrary")),
        name="mla_attention",
    )(q, k, v)


def _split2(x):
    hi = x.astype(BF16)
    return hi, (x - hi.astype(F32)).astype(BF16)


def _dot_x3(a, b):
    ah, al = _split2(a)
    bh, bl = _split2(b)
    return _dot(jnp.concatenate([ah, al, ah], axis=1), jnp.concatenate([bh, bh, bl], axis=0))


def _dot_nt_x3(a, b):
    ah, al = _split2(a)
    bh, bl = _split2(b)
    return _dot_nt(jnp.concatenate([ah, al, ah], axis=1), jnp.concatenate([bh, bh, bl], axis=1))


def _dot_bf(a, b):
    return _dot(a.astype(BF16), b.astype(BF16))


def _unit_lower_inverse(a_list, levels, dot):
    n = a_list[0].shape[0]
    eye = (lax.broadcasted_iota(jnp.int32, (n, n), 0) == lax.broadcasted_iota(jnp.int32, (n, n), 1)).astype(F32)
    p = [-a for a in a_list]
    r = [eye + x for x in p]
    p = [dot(x, x) for x in p]
    for _ in range(levels - 2):
        s = [dot(jnp.concatenate([ri, pi], axis=0), pi) for ri, pi in zip(r, p)]
        r = [ri + si[:n] for ri, si in zip(r, s)]
        p = [si[n:] for si in s]
    return [ri + dot(ri, pi) for ri, pi in zip(r, p)]


def _cumsum_rows(x):
    c = x.shape[0]
    row = lax.broadcasted_iota(jnp.int32, x.shape, 0)
    s = 1
    while s < c:
        x = x + jnp.where(row >= s, pltpu.roll(x, s, 0), 0.0)
        s *= 2
    return x


def _col_of_row(row_vec, n):
    eye = lax.broadcasted_iota(jnp.int32, (n, n), 0) == lax.broadcasted_iota(jnp.int32, (n, n), 1)
    return jnp.sum(jnp.where(eye, jnp.broadcast_to(row_vec, (n, n)), 0.0), axis=1, keepdims=True)


def _gdn_kernel(gq_ref, gk_ref, gv_ref, gz_ref, sm_ref, cw_ref, alog_ref, dtb_ref, on_ref, o_ref,
                xq_scr, xk_scr, xv_scr, state_scr, *, chunk):
    c = chunk
    heads = range(GDN_HEADS)
    hs = [slice(h * LANES, (h + 1) * LANES) for h in heads]

    @pl.when(pl.program_id(1) == 0)
    def _():
        for scr in (xq_scr, xk_scr, xv_scr):
            scr[0:8, :] = jnp.zeros((8, scr.shape[1]), F32)
        state_scr[...] = jnp.zeros_like(state_scr)

    def conv_silu(x_ref, scr, widx):
        scr[8:8 + c, :] = x_ref[...].astype(F32)
        acc = cw_ref[widx, GDN_CONV - 1:GDN_CONV, :] * scr[8:8 + c, :]
        for j in range(GDN_CONV - 1):
            s = GDN_CONV - 1 - j
            acc = acc + cw_ref[widx, j:j + 1, :] * scr[8 - s:8 - s + c, :]
        scr[0:8, :] = scr[c:c + 8, :]
        return _silu(acc)

    def l2n(x):
        return x * lax.rsqrt(jnp.sum(x * x, axis=-1, keepdims=True) + 1e-6)

    def bf(x):
        return x.astype(BF16)

    qa = conv_silu(gq_ref, xq_scr, 0)
    ka = conv_silu(gk_ref, xk_scr, 1)
    va = conv_silu(gv_ref, xv_scr, 2)
    q = [l2n(qa[:, s]) * (GDN_DK ** -0.5) for s in hs]
    k = [l2n(ka[:, s]) for s in hs]
    v = [va[:, s] for s in hs]

    sm = sm_ref[...].astype(F32)
    lane = lax.broadcasted_iota(jnp.int32, (c, LANES), 1)
    beta = [_sigmoid(jnp.sum(jnp.where(lane == h, sm, 0.0), axis=-1, keepdims=True)) for h in heads]
    ga = [jnp.sum(jnp.where(lane == GDN_HEADS + h, sm, 0.0), axis=-1, keepdims=True) for h in heads]
    g = [-jnp.exp(alog_ref[:, hs[h]]) * _softplus(jnp.broadcast_to(ga[h], (c, LANES)) + dtb_ref[:, hs[h]])
         for h in heads]

    ri = lax.broadcasted_iota(jnp.int32, (c, c), 0)
    cj = lax.broadcasted_iota(jnp.int32, (c, c), 1)
    gam = [_cumsum_rows(x) for x in g]
    decay = [jnp.exp(jnp.where(cj <= ri, x - x.T, -1e30)) for x in gam]
    eg = [jnp.exp(x) for x in gam]
    gl = [x[c - 1:c, :] for x in gam]
    kb = [k[h] * beta[h] for h in heads]
    vb = [v[h] * beta[h] for h in heads]

    a_mat = [jnp.where(cj < ri, _dot_nt_x3(kb[h], k[h]) * decay[h], 0.0) for h in heads]
    tinv = _unit_lower_inverse(a_mat, 7, _dot_x3)
    rhs = [jnp.concatenate([kb[h] * eg[h], vb[h]], axis=1) for h in heads]
    wu = [bf(_dot_x3(tinv[h], rhs[h])) for h in heads]
    attn = [_dot_nt(bf(q[h]), bf(k[h])) * decay[h] for h in heads]
    k_dec = [k[h] * jnp.exp(gl[h] - gam[h]) for h in heads]
    lhs = [bf(jnp.concatenate([attn[h], k_dec[h].T], axis=0)) for h in heads]
    z = [_dot(lhs[h], wu[h]) for h in heads]
    state = [state_scr[h] for h in heads]
    l2 = [bf(jnp.concatenate([q[h] * eg[h] - z[h][:c, :LANES], -z[h][c:, :LANES]], axis=0)) for h in heads]
    rr = [_dot(l2[h], bf(state[h])) for h in heads]
    for h in heads:
        state_scr[h] = state[h] * jnp.exp(gl[h]) + rr[h][c:] + z[h][c:, LANES:]
        o = rr[h][:c] + z[h][:c, LANES:]
        o_ref[:, hs[h]] = (_rms(o, on_ref[...]) * _silu(gz_ref[:, hs[h]].astype(F32))).astype(o_ref.dtype)


def gdn_mixer(hcat, conv_w, a_log, dt_bias, out_norm, batch, seq):
    c = GDN_CHUNK
    assert c == LANES and seq % c == 0
    nc = seq // c
    width = GDN_HEADS * LANES
    blk = lambda off: pl.BlockSpec((c, width), lambda b, i, off=off: (b * nc + i, off // width))
    const2 = lambda b, i: (0, 0)
    cw = conv_w.reshape(GDN_CONV, 3, width).transpose(1, 0, 2)
    alog = jnp.repeat(a_log.astype(F32), LANES).reshape(1, width)
    dtb = jnp.repeat(dt_bias.astype(F32), LANES).reshape(1, width)
    return pl.pallas_call(
        functools.partial(_gdn_kernel, chunk=c),
        grid=(batch, nc),
        in_specs=[blk(HC_GQ), blk(HC_GK), blk(HC_GV), blk(HC_GZ),
                  pl.BlockSpec((c, LANES), lambda b, i: (b * nc + i, HC_SM // LANES)),
                  pl.BlockSpec((3, GDN_CONV, width), lambda b, i: (0, 0, 0)),
                  pl.BlockSpec((1, width), const2),
                  pl.BlockSpec((1, width), const2),
                  pl.BlockSpec((1, GDN_DV), const2)],
        out_specs=pl.BlockSpec((c, width), lambda b, i: (b * nc + i, 0)),
        out_shape=jax.ShapeDtypeStruct((batch * seq, width), BF16),
        scratch_shapes=[pltpu.VMEM((c + 8, width), F32)] * 3 + [pltpu.VMEM((GDN_HEADS, GDN_DK, GDN_DV), F32)],
        compiler_params=_cparams(("parallel", "arbitrary")),
        name="gdn_mixer",
    )(hcat, hcat, hcat, hcat, hcat, cw, alog, dtb, out_norm)


LZ_W, LZ_A, LZ_G, LZ_V, LZ_COLS = 0, 128, 256, 512, 640


def _rwkv_mix_kernel(h_ref, hp_ref, g_ref, mix_ref, a1_ref, o_ref, z_ref, *, tm, seq, has_v):
    i = pl.program_id(0)
    g = g_ref[...]
    xn = _rms(h_ref[...], g)
    prev_last = _rms(hp_ref[...], g)[7:8, :]
    prev_last = jnp.where((i * tm) % seq == 0, 0.0, prev_last)
    rows = lax.broadcasted_iota(jnp.int32, xn.shape, 0)
    xprev = jnp.where(rows == 0, prev_last, pltpu.roll(xn, 1, 0))
    xx = xprev - xn

    def mixed(j):
        return (xn + xx * mix_ref[j:j + 1, :]).astype(BF16)

    o_ref[0] = mixed(0)
    o_ref[1] = mixed(2)
    xv = mixed(3)
    o_ref[2] = xv
    z_ref[:, LZ_W:LZ_A] = jnp.tanh(_dot(mixed(1), a1_ref[:, LZ_W:LZ_A])).astype(BF16)
    z_ref[:, LZ_A:LZ_G] = _dot(mixed(4), a1_ref[:, LZ_A:LZ_G]).astype(BF16)
    z_ref[:, LZ_G:LZ_V] = _sigmoid(_dot(mixed(5), a1_ref[:, LZ_G:LZ_V])).astype(BF16)
    if has_v:
        z_ref[:, LZ_V:] = _dot(xv, a1_ref[:, LZ_V:]).astype(BF16)
    else:
        z_ref[:, LZ_V:] = jnp.zeros((tm, LZ_COLS - LZ_V), BF16)


def rwkv_mix(h, g, mix, a1_cat, seq, *, has_v, tm=512):
    m, d = h.shape
    tm = _tile(seq, tm)
    return pl.pallas_call(
        functools.partial(_rwkv_mix_kernel, tm=tm, seq=seq, has_v=has_v),
        grid=(m // tm,),
        in_specs=[pl.BlockSpec((tm, d), lambda i: (i, 0)),
                  pl.BlockSpec((8, d), lambda i: (jnp.maximum(i * (tm // 8) - 1, 0), 0)),
                  pl.BlockSpec((1, d), lambda i: (0, 0)),
                  pl.BlockSpec((6, d), lambda i: (0, 0)),
                  pl.BlockSpec((d, LZ_COLS), lambda i: (0, 0))],
        out_specs=[pl.BlockSpec((3, tm, d), lambda i: (0, i, 0)),
                   pl.BlockSpec((tm, LZ_COLS), lambda i: (i, 0))],
        out_shape=[jax.ShapeDtypeStruct((3, m, d), BF16),
                   jax.ShapeDtypeStruct((m, LZ_COLS), BF16)],
        compiler_params=_cparams(("parallel",)),
        name="rwkv_mix",
    )(h, h, g, mix, a1_cat)


def _lora_up_kernel(z_ref, b_ref, bias_ref, *rest, has_v):
    if has_v:
        v_ref, vf_ref, lw_ref, a_ref, gt_ref, vo_ref = rest
    else:
        lw_ref, a_ref, gt_ref = rest
    y = _dot(z_ref[:, LZ_W:LZ_A], b_ref[LZ_W:LZ_A, :]) + bias_ref[0:1, :]
    lw_ref[...] = -jnp.exp(-_softplus(-y) - 0.5)
    y = _dot(z_ref[:, LZ_A:LZ_G], b_ref[LZ_A:LZ_G, :]) + bias_ref[1:2, :]
    a_ref[...] = _sigmoid(y).astype(a_ref.dtype)
    gt_ref[...] = _dot(z_ref[:, LZ_G:LZ_V], b_ref[LZ_G:LZ_V, :]).astype(gt_ref.dtype)
    if has_v:
        y = _dot(z_ref[:, LZ_V:], b_ref[LZ_V:, :]) + bias_ref[2:3, :]
        v = v_ref[...].astype(F32)
        vo_ref[...] = (v + (vf_ref[...].astype(F32) - v) * _sigmoid(y)).astype(vo_ref.dtype)


def lora_up(z, b_cat, bias, v=None, v_first=None, *, tm=512):
    m = z.shape[0]
    d = b_cat.shape[1]
    has_v = v is not None
    tm = _tile(m, tm)
    row = lambda i: (i, 0)
    const = lambda i: (0, 0)
    big = pl.BlockSpec((tm, d), row)
    n_bf = 3 if has_v else 2
    return pl.pallas_call(
        functools.partial(_lora_up_kernel, has_v=has_v),
        grid=(m // tm,),
        in_specs=[pl.BlockSpec((tm, LZ_COLS), row), pl.BlockSpec((LZ_COLS, d), const),
                  pl.BlockSpec((3, d), const)] + ([big, big] if has_v else []),
        out_specs=[big] * (1 + n_bf),
        out_shape=[jax.ShapeDtypeStruct((m, d), F32)] + [jax.ShapeDtypeStruct((m, d), BF16)] * n_bf,
        compiler_params=_cparams(("parallel",)),
        name="lora_up",
    )(z, b_cat, bias, *((v, v_first) if has_v else ()))


def _rwkv_kernel(r_ref, k_ref, v_ref, a_ref, lw_ref, gt_ref, kk_ref, ka_ref, rk_ref, lnw_ref, lnb_ref,
                 o_ref, state_scr, *, chunk, pairs):
    c = chunk
    ps = range(pairs)
    sls = [slice(p * LANES, (p + 1) * LANES) for p in ps]

    @pl.when(pl.program_id(2) == 0)
    def _():
        state_scr[...] = jnp.zeros_like(state_scr)

    lane = lax.broadcasted_iota(jnp.int32, (c, LANES), 1)
    m0 = lane < RWKV_HEAD
    r2 =lax.broadcasted_iota(jnp.int32, (2 * c, 2 * c), 0)
    c2 = lax.broadcasted_iota(jnp.int32, (2 * c, 2 * c), 1)
    strict2 = c2 < r2
    rw = lax.broadcasted_iota(jnp.int32, (c, 2 * c), 0)
    cw = lax.broadcasted_iota(jnp.int32, (c, 2 * c), 1)
    incl_w = jnp.where(cw >= c, cw - c, cw) <= rw
    zero_blk = jnp.zeros((2 * c, LANES), BF16)

    def seg_sum(x):
        s0 = jnp.sum(jnp.where(m0, x, 0.0), axis=-1, keepdims=True)
        s1 = jnp.sum(jnp.where(m0, 0.0, x), axis=-1, keepdims=True)
        return jnp.where(m0, s0, s1)

    def stack(x):
        return jnp.concatenate([jnp.where(m0, x, 0.0), jnp.where(m0, 0.0, x)], axis=0)

    def bf(x):
        return x.astype(BF16)

    r = [r_ref[:, s].astype(F32) for s in sls]
    k = [k_ref[:, s].astype(F32) for s in sls]
    v = [v_ref[:, s].astype(F32) for s in sls]
    a = [a_ref[:, s].astype(F32) for s in sls]
    lw = [lw_ref[:, s] for s in sls]
    kk = [k[p] * kk_ref[:, sls[p]] for p in ps]
    kk = [x * lax.rsqrt(seg_sum(x * x) + 1e-12) for x in kk]
    kh = [k[p] * (1.0 + (a[p] - 1.0) * ka_ref[:, sls[p]]) for p in ps]
    bv = [kk[p] * a[p] for p in ps]

    cum = [_cumsum_rows(x) for x in lw]
    cl = [x[c - 1:c, :] for x in cum]
    e_neg = [jnp.exp(-x) for x in cum]
    e_end = [jnp.exp(cl[p] - cum[p]) for p in ps]
    rg = [r[p] * jnp.exp(cum[p]) for p in ps]
    rgc = [bf(x) for x in rg]
    xs = [bf(stack(kk[p] * jnp.exp(cum[p] - lw[p]))) for p in ps]
    bs = [bf(stack(bv[p] * e_neg[p])) for p in ps]
    ks = [bf(stack(kh[p] * e_neg[p])) for p in ps]
    vs = [bf(stack(v[p])) for p in ps]
    kht = [stack(kh[p] * e_end[p]).T for p in ps]
    bht = [stack(bv[p] * e_end[p]).T for p in ps]

    ab = [_dot_nt(xs[p], jnp.concatenate([bs[p], ks[p]], axis=0)) for p in ps]
    akk = [jnp.where(strict2, x[:, :2 * c], 0.0) for x in ab]
    akv = [bf(jnp.where(strict2, x[:, 2 * c:], 0.0)) for x in ab]
    tinv = [bf(x) for x in _unit_lower_inverse(akk, 6, _dot_bf)]
    w1 = [bf(_dot(akv[p], vs[p])) for p in ps]
    m12 = [bf(_dot(tinv[p], jnp.concatenate([xs[p], w1[p]], axis=1))) for p in ps]
    bb = [_dot_nt(rgc[p], jnp.concatenate([ks[p], bs[p]], axis=0)) for p in ps]
    lhs = [bf(jnp.concatenate([
        jnp.concatenate([jnp.where(incl_w, x[:, :2 * c], 0.0), -jnp.where(incl_w, x[:, 2 * c:], 0.0)], axis=1),
        jnp.concatenate([kht[p], -bht[p]], axis=1)], axis=0)) for p, x in enumerate(bb)]
    rhs = [jnp.concatenate([jnp.concatenate([vs[p], zero_blk], axis=1),
                            jnp.concatenate([m12[p][:, LANES:], m12[p][:, :LANES]], axis=1)], axis=0)
           for p in ps]
    z = [_dot(lhs[p], rhs[p]) for p in ps]
    state = [state_scr[p] for p in ps]
    l2 = [bf(jnp.concatenate([rg[p] + z[p][:c, LANES:], z[p][c:, LANES:]], axis=0)) for p in ps]
    rr = [_dot(l2[p], bf(state[p])) for p in ps]
    for p in ps:
        sl = sls[p]
        state_scr[p] = state[p] * _col_of_row(jnp.exp(cl[p]), LANES) + rr[p][c:] + z[p][c:, :LANES]
        y = rr[p][:c] + z[p][:c, :LANES]
        mu = seg_sum(y) * (1.0 / RWKV_HEAD)
        yc = y - mu
        var = seg_sum(yc * yc) * (1.0 / RWKV_HEAD)
        yn = yc * lax.rsqrt(var + RWKV_LN_EPS) * lnw_ref[:, sl] + lnb_ref[:, sl]
        bonus = seg_sum(r[p] * kh[p] * rk_ref[:, sl]) * v[p]
        o_ref[:, sl] = ((yn + bonus) * gt_ref[:, sl].astype(F32)).astype(o_ref.dtype)


def rwkv_recurrence(r, k, v, a, lw, gate, k_k, k_a, r_k, ln_w, ln_b, batch, seq, *, pairs=16):
    c = RWKV_CHUNK
    m, d = r.shape
    nc = seq // c
    width = pairs * LANES
    blk = pl.BlockSpec((c, width), lambda b, hp, i: (b * nc + i, hp))
    par = pl.BlockSpec((1, width), lambda b, hp, i: (0, hp))
    return pl.pallas_call(
        functools.partial(_rwkv_kernel, chunk=c, pairs=pairs),
        grid=(batch, d // width, nc),
        in_specs=[blk] * 6 + [par] * 5,
        out_specs=blk,
        out_shape=jax.ShapeDtypeStruct((m, d), BF16),
        scratch_shapes=[pltpu.VMEM((pairs, LANES, LANES), F32)],
        compiler_params=_cparams(("parallel", "parallel", "arbitrary")),
        name="rwkv_recurrence",
    )(r, k, v, a, lw, gate, k_k, k_a, r_k, ln_w, ln_b)


def _pad_cols(w, n):
    return jnp.pad(w, ((0, 0), (0, n - w.shape[1])))


def _rot_cols(w):
    half = MLA_ROPE // 2
    return jnp.concatenate([-w[..., half:], w[..., :half]], axis=-1)


def _input_projection_weight(w_in):
    o_kr = MLA_Q_LORA + MLA_KV_LORA
    o_g = o_kr + MLA_ROPE
    o_sm = o_g + 4 * GDN_HEADS * GDN_DK
    w_in = w_in.astype(BF16)
    w_kr = w_in[..., o_kr:o_g]
    small = w_in[..., o_sm:]
    small = jnp.pad(small, ((0, 0), (0, 0), (0, LANES - small.shape[-1])))
    return jnp.concatenate([w_in[..., :o_kr], w_in[..., o_g:o_sm], w_kr, _rot_cols(w_kr), small], axis=-1)


def _hybrid_layer(h, ct, st, batch, seq, g_pre, g_post, w_cat, q_norm, w_uq, kv_norm, w_ukv, conv_w, a_log,
                  dt_bias, out_norm, w_out, e):
    d = D_MODEL
    hcat = norm_matmul(h, g_pre.reshape(1, d), w_cat, e, tn=1792, out_dtype=BF16)

    wq = w_uq.reshape(MLA_Q_LORA, MLA_HEADS, MLA_QK)
    wq_rope = wq[..., MLA_NOPE:]
    wq = jnp.concatenate([wq[..., :MLA_NOPE], wq_rope, _rot_cols(wq_rope)], axis=-1)
    wq = wq.reshape(MLA_Q_LORA, MLA_HEADS * MLA_HEAD_PAD).astype(BF16)
    q, k, v = mla_proj(hcat, q_norm.reshape(1, -1), kv_norm.reshape(1, -1), wq, w_ukv.astype(BF16), ct, st)
    mla_out = mla_attention(q, k, v, batch, seq)
    gdn_out = gdn_mixer(hcat, conv_w, a_log, dt_bias, out_norm.reshape(1, -1), batch, seq)
    return matmul_norm_residual([mla_out, gdn_out], w_out, e, g_post.reshape(1, d), h, tn=d)


def _rwkv_layer(h, v_first, batch, seq, g_pre, g_post, mix, w_r, w_k, w_v, w_o, o, w0, w1, w2, a0, a1, a2,
                g1, g2, k_k, k_a, r_k, ln_w, ln_b, vres):
    d = D_MODEL
    row = lambda t: t.reshape(1, d).astype(F32)
    padc = lambda w, n: _pad_cols(w, n)
    padr = lambda w, n: jnp.pad(w, ((0, n - w.shape[0]), (0, 0)))
    has_v = vres is not None
    v0, v1, v2 = vres if has_v else (jnp.zeros((d,), F32), jnp.zeros((d, 0), F32), jnp.zeros((0, d), F32))
    a1_cat = jnp.concatenate([padc(w1, LZ_A - LZ_W), padc(a1, LZ_G - LZ_A), padc(g1, LZ_V - LZ_G),
                              padc(v1, LZ_COLS - LZ_V)], axis=1).astype(BF16)
    b_cat = jnp.concatenate([padr(w2, LZ_A - LZ_W), padr(a2, LZ_G - LZ_A), padr(g2, LZ_V - LZ_G),
                             padr(v2, LZ_COLS - LZ_V)], axis=0).astype(BF16)
    bias = jnp.stack([w0, a0, v0]).astype(F32)
    xs, z = rwkv_mix(h, g_pre.reshape(1, d), mix, a1_cat, seq, has_v=has_v)
    r = matmul(xs, 0, w_r, o, out_dtype=BF16)
    k = matmul(xs, 1, w_k, o, out_dtype=BF16)
    v = matmul(xs, 2, w_v, o, out_dtype=BF16)
    if has_v:
        lw, a, gate, v = lora_up(z, b_cat, bias, v, v_first)
    else:
        lw, a, gate = lora_up(z, b_cat, bias)
        v_first = v
    y = rwkv_recurrence(r, k, v, a, lw, gate, row(k_k), row(k_a), row(r_k), row(ln_w), row(ln_b), batch, seq)
    return matmul_norm_residual([y], w_o, o, g_post.reshape(1, d), h, tn=d), v_first


def _ffn_layer(h, g_pre, g_post, w_gate, w_up, w_down, layer):
    d = D_MODEL
    act = ffn_up(h, g_pre.reshape(1, d), w_gate, w_up, layer)
    return matmul_norm_residual([act], w_down, layer, g_post.reshape(1, d), h)


def _rope_tables(positions):
    inv_freq = 1.0 / (ROPE_THETA ** (jnp.arange(0, MLA_ROPE, 2, dtype=F32) / MLA_ROPE))
    ang = positions.astype(F32).reshape(-1, 1) * inv_freq
    zeros = jnp.zeros((ang.shape[0], LANES - MLA_ROPE), F32)
    cos, sin = jnp.cos(ang), jnp.sin(ang)
    return jnp.concatenate([cos, cos, zeros], axis=1), jnp.concatenate([sin, sin, zeros], axis=1)


def kernel(x, positions, norm_mix_pre, norm_mix_post, norm_ffn_pre, norm_ffn_post, hyb_w_in, mla_q_norm, mla_w_uq, mla_kv_norm, mla_w_ukv, gdn_conv_w, gdn_a_log, gdn_dt_bias, gdn_out_norm, hyb_w_out, rwkv_mix, rwkv_w_r, rwkv_w_k, rwkv_w_v, rwkv_w_o, rwkv_w0, rwkv_w1, rwkv_w2, rwkv_a0, rwkv_a1, rwkv_a2, rwkv_g1, rwkv_g2, rwkv_k_k, rwkv_k_a, rwkv_r_k, rwkv_ln_w, rwkv_ln_b, rwkv_v0, rwkv_v1, rwkv_v2, ffn_w_gate, ffn_w_up, ffn_w_down):
    batch, seq, d = x.shape
    h = x.reshape(batch * seq, d)
    ct, st = _rope_tables(positions)
    bf = lambda w: w.astype(BF16)
    ffn_w_gate, ffn_w_up, ffn_w_down, hyb_w_out = bf(ffn_w_gate), bf(ffn_w_up), bf(ffn_w_down), bf(hyb_w_out)
    rwkv_w_r, rwkv_w_k, rwkv_w_v, rwkv_w_o = bf(rwkv_w_r), bf(rwkv_w_k), bf(rwkv_w_v), bf(rwkv_w_o)
    hyb_w_cat = _input_projection_weight(hyb_w_in)
    v_first = None
    for layer in range(DEPTH):
        if layer % 2 == 0:
            e = layer // 2
            h = _hybrid_layer(h, ct, st, batch, seq, norm_mix_pre[layer], norm_mix_post[layer], hyb_w_cat,
                              mla_q_norm[e], mla_w_uq[e], mla_kv_norm[e], mla_w_ukv[e], gdn_conv_w[e],
                              gdn_a_log[e], gdn_dt_bias[e], gdn_out_norm[e], hyb_w_out, e)
        else:
            o = layer // 2
            vres = None if o == 0 else (rwkv_v0[o - 1], rwkv_v1[o - 1], rwkv_v2[o - 1])
            h, v_first = _rwkv_layer(h, v_first, batch, seq, norm_mix_pre[layer], norm_mix_post[layer],
                                     rwkv_mix[o], rwkv_w_r, rwkv_w_k, rwkv_w_v, rwkv_w_o, o,
                                     rwkv_w0[o], rwkv_w1[o], rwkv_w2[o], rwkv_a0[o], rwkv_a1[o], rwkv_a2[o],
                                     rwkv_g1[o], rwkv_g2[o], rwkv_k_k[o], rwkv_k_a[o], rwkv_r_k[o],
                                     rwkv_ln_w[o], rwkv_ln_b[o], vres)
        h = _ffn_layer(h, norm_ffn_pre[layer], norm_ffn_post[layer], ffn_w_gate, ffn_w_up, ffn_w_down, layer)
    return h.reshape(batch, seq, d)
```

```python
import functools

import jax
import jax.numpy as jnp
import numpy as np
from jax import lax
from jax.experimental import pallas as pl
from jax.experimental.pallas import tpu as pltpu

F32 = jnp.float32
BF16 = jnp.bfloat16
HI = lax.Precision.HIGHEST

D_MODEL = 2048
DEPTH = 4
LANES = 128
VMEM_LIMIT = 56 * 1024 * 1024

MLA_HEADS = 8
MLA_Q_LORA = 512
MLA_KV_LORA = 512
MLA_NOPE = 128
MLA_ROPE = 64
MLA_V = 128
MLA_QK = MLA_NOPE + MLA_ROPE
MLA_HEAD_PAD = 256
ROPE_THETA = 10000.0

GDN_HEADS = 8
GDN_DK = 128
GDN_DV = 128
GDN_CONV = 4
GDN_CHUNK = 128

RWKV_HEAD = 64
RWKV_HEADS = D_MODEL // RWKV_HEAD
RWKV_CHUNK = 64
RWKV_LN_EPS = 64e-5
RMS_EPS = 1e-6

HC_CQ, HC_CKV, HC_GQ, HC_GK, HC_GV, HC_GZ, HC_KR, HC_SM = 0, 512, 1024, 2048, 3072, 4096, 5120, 5248
HC_COLS = 5376


def _cparams(sem):
    return pltpu.CompilerParams(dimension_semantics=sem, vmem_limit_bytes=VMEM_LIMIT)


def _tile(n, pref):
    t = min(n, pref)
    while n % t:
        t -= 8
    return t


def _layer_spec(w, layer, blk, imap):
    return pl.BlockSpec((None,) + blk, lambda *g: (layer,) + imap(*g))


def _dot(a, b, prec=None):
    return jnp.dot(a, b, preferred_element_type=F32, precision=prec)


def _dot_nt(a, b, prec=None):
    return lax.dot_general(a, b, (((1,), (1,)), ((), ())), preferred_element_type=F32, precision=prec)


def _rms(x, g, eps=RMS_EPS):
    return x * lax.rsqrt(jnp.mean(x * x, axis=-1, keepdims=True) + eps) * g


def _sigmoid(x):
    return 1.0 / (1.0 + jnp.exp(-x))


def _silu(x):
    return x * _sigmoid(x)


def _softplus(x):
    return jnp.maximum(x, 0.0) + jnp.log(1.0 + jnp.exp(-jnp.abs(x)))


def _norm_mm_kernel(x_ref, g_ref, w_ref, o_ref, xn_ref):
    @pl.when(pl.program_id(1) == 0)
    def _():
        xn_ref[...] = _rms(x_ref[...], g_ref[...]).astype(BF16)

    o_ref[...] = _dot(xn_ref[...], w_ref[...]).astype(o_ref.dtype)


def norm_matmul(x, g, w, layer, *, tn, out_dtype=F32, tm=1024):
    m, k = x.shape
    n = w.shape[2]
    tm = _tile(m, tm)
    return pl.pallas_call(
        _norm_mm_kernel,
        grid=(m // tm, n // tn),
        in_specs=[pl.BlockSpec((tm, k), lambda i, j: (i, 0)),
                  pl.BlockSpec((1, k), lambda i, j: (0, 0)),
                  _layer_spec(w, layer, (k, tn), lambda i, j: (0, j))],
        out_specs=pl.BlockSpec((tm, tn), lambda i, j: (i, j)),
        out_shape=jax.ShapeDtypeStruct((m, n), out_dtype),
        scratch_shapes=[pltpu.VMEM((tm, k), BF16)],
        compiler_params=_cparams(("parallel", "arbitrary")),
        name="norm_matmul",
    )(x, g, w)


def _ffn_up_kernel(x_ref, g_ref, wg_ref, wu_ref, o_ref, xn_ref):
    @pl.when(pl.program_id(1) == 0)
    def _():
        xn_ref[...] = _rms(x_ref[...], g_ref[...]).astype(BF16)

    xn = xn_ref[...]
    gate = _dot(xn, wg_ref[...])
    up = _dot(xn, wu_ref[...])
    o_ref[...] = (_silu(gate) * up).astype(o_ref.dtype)


def ffn_up(h, g, wg, wu, layer, *, tm=1024, tn=512):
    m, k = h.shape
    n = wg.shape[2]
    tm = _tile(m, tm)
    tn = _tile(n, tn)
    return pl.pallas_call(
        _ffn_up_kernel,
        grid=(m // tm, n // tn),
        in_specs=[pl.BlockSpec((tm, k), lambda i, j: (i, 0)),
                  pl.BlockSpec((1, k), lambda i, j: (0, 0)),
                  _layer_spec(wg, layer, (k, tn), lambda i, j: (0, j)),
                  _layer_spec(wu, layer, (k, tn), lambda i, j: (0, j))],
        out_specs=pl.BlockSpec((tm, tn), lambda i, j: (i, j)),
        out_shape=jax.ShapeDtypeStruct((m, n), BF16),
        scratch_shapes=[pltpu.VMEM((tm, k), BF16)],
        compiler_params=_cparams(("parallel", "arbitrary")),
        name="ffn_up",
    )(h, g, wg, wu)


def _mm_norm_res_kernel(*refs, tn, nb, n_a):
    a_refs = refs[:n_a]
    w_ref, g_ref, h_ref, o_ref, f_ref = refs[n_a:]
    j = pl.program_id(1)
    k0 = 0
    f = None
    for a_ref in a_refs:
        part = _dot(a_ref[...], w_ref[k0:k0 + a_ref.shape[1], :])
        f = part if f is None else f + part
        k0 += a_ref.shape[1]
    f_ref[j] = f

    @pl.when(j == nb - 1)
    def _():
        ssq = jnp.sum(f_ref[0] * f_ref[0], axis=-1, keepdims=True)
        for b in range(1, nb):
            ssq = ssq + jnp.sum(f_ref[b] * f_ref[b], axis=-1, keepdims=True)
        inv = lax.rsqrt(ssq * (1.0 / (nb * tn)) + RMS_EPS)
        for b in range(nb):
            cs = slice(b * tn, (b + 1) * tn)
            o_ref[:, cs] = h_ref[:, cs] + f_ref[b] * inv * g_ref[:, cs]


def matmul_norm_residual(a_parts, w, layer, g, h, *, tm=512, tn=512):
    m = a_parts[0].shape[0]
    _, k, n = w.shape
    tm = _tile(m, tm)
    tn = _tile(n, tn)
    nb = n // tn
    return pl.pallas_call(
        functools.partial(_mm_norm_res_kernel, tn=tn, nb=nb, n_a=len(a_parts)),
        grid=(m // tm, nb),
        in_specs=[pl.BlockSpec((tm, a.shape[1]), lambda i, j: (i, 0)) for a in a_parts] + [
                  _layer_spec(w, layer, (k, tn), lambda i, j: (0, j)),
                  pl.BlockSpec((1, n), lambda i, j: (0, 0)),
                  pl.BlockSpec((tm, n), lambda i, j: (i, 0))],
        out_specs=pl.BlockSpec((tm, n), lambda i, j: (i, 0)),
        out_shape=jax.ShapeDtypeStruct((m, n), F32),
        scratch_shapes=[pltpu.VMEM((nb, tm, tn), F32)],
        compiler_params=_cparams(("parallel", "arbitrary")),
        name="matmul_norm_residual",
    )(*a_parts, w, g, h)


def _mm_kernel(a_ref, w_ref, o_ref):
    o_ref[...] = _dot(a_ref[...], w_ref[...]).astype(o_ref.dtype)


def matmul(a, idx, w, layer, *, tm=1024, tn=1024, out_dtype=F32):
    _, m, k = a.shape
    n = w.shape[2]
    tm = _tile(m, tm)
    tn = _tile(n, tn)
    return pl.pallas_call(
        _mm_kernel,
        grid=(m // tm, n // tn),
        in_specs=[pl.BlockSpec((None, tm, k), lambda i, j: (idx, i, 0)),
                  _layer_spec(w, layer, (k, tn), lambda i, j: (0, j))],
        out_specs=pl.BlockSpec((tm, tn), lambda i, j: (i, j)),
        out_shape=jax.ShapeDtypeStruct((m, n), out_dtype),
        compiler_params=_cparams(("parallel", "arbitrary")),
        name="matmul",
    )(a, w)


def _rope_group(grp, ct, st):
    return grp * ct + pltpu.roll(grp, 64, 1) * st


def _mla_proj_kernel(cq_ref, ckv_ref, kr_ref, qn_ref, kvn_ref, wq_ref, wkv_ref, ct_ref, st_ref,
                     q_ref, k_ref, v_ref):
    scale = MLA_QK ** -0.5 * np.log2(np.e)
    ct = ct_ref[...]
    st = st_ref[...]
    xq = _rms(cq_ref[...].astype(F32), qn_ref[...]).astype(BF16)
    xkv = _rms(ckv_ref[...].astype(F32), kvn_ref[...]).astype(BF16)
    kr = _rope_group(kr_ref[...].astype(F32), ct, st).astype(BF16)
    for h in range(MLA_HEADS):
        c0 = h * MLA_HEAD_PAD
        yq = _dot(xq, wq_ref[:, c0:c0 + MLA_HEAD_PAD])
        q_ref[:, c0:c0 + LANES] = (yq[:, :LANES] * scale).astype(BF16)
        q_ref[:, c0 + LANES:c0 + 2 * LANES] = (_rope_group(yq[:, LANES:], ct, st) * scale).astype(BF16)
        ykv = _dot(xkv, wkv_ref[:, c0:c0 + MLA_HEAD_PAD])
        k_ref[:, c0:c0 + LANES] = ykv[:, :LANES].astype(BF16)
        k_ref[:, c0 + LANES:c0 + 2 * LANES] = kr
        v_ref[:, h * MLA_V:(h + 1) * MLA_V] = ykv[:, LANES:].astype(BF16)


def mla_proj(hcat, q_norm, kv_norm, wq, wkv, ct, st, *, tm=1024):
    m = hcat.shape[0]
    tm = _tile(m, tm)
    hp = MLA_HEADS * MLA_HEAD_PAD
    row = lambda i: (i, 0)
    const = lambda i: (0, 0)
    return pl.pallas_call(
        _mla_proj_kernel,
        grid=(m // tm,),
        in_specs=[pl.BlockSpec((tm, MLA_Q_LORA), lambda i: (i, HC_CQ // MLA_Q_LORA)),
                  pl.BlockSpec((tm, MLA_KV_LORA), lambda i: (i, HC_CKV // MLA_KV_LORA)),
                  pl.BlockSpec((tm, LANES), lambda i: (i, HC_KR // LANES)),
                  pl.BlockSpec((1, MLA_Q_LORA), const),
                  pl.BlockSpec((1, MLA_KV_LORA), const),
                  pl.BlockSpec((MLA_Q_LORA, hp), const),
                  pl.BlockSpec((MLA_KV_LORA, hp), const),
                  pl.BlockSpec((tm, LANES), row),
                  pl.BlockSpec((tm, LANES), row)],
        out_specs=[pl.BlockSpec((tm, hp), row),
                   pl.BlockSpec((tm, hp), row),
                   pl.BlockSpec((tm, MLA_HEADS * MLA_V), row)],
        out_shape=[jax.ShapeDtypeStruct((m, hp), BF16),
                   jax.ShapeDtypeStruct((m, hp), BF16),
                   jax.ShapeDtypeStruct((m, MLA_HEADS * MLA_V), BF16)],
        compiler_params=_cparams(("parallel",)),
        name="mla_proj",
    )(hcat, hcat, hcat, q_norm, kv_norm, wq, wkv, ct, st)


def _attn_kernel(q_ref, k_ref, v_ref, o_ref, *, tq, tk, hpg):
    qi = pl.program_id(2)
    hs = range(hpg)
    q = [q_ref[:, h * MLA_HEAD_PAD:(h + 1) * MLA_HEAD_PAD] for h in hs]

    def step(ki, carry, masked):
        start = pl.multiple_of(ki * tk, tk)
        s = [_dot_nt(q[h], k_ref[pl.ds(start, tk), h * MLA_HEAD_PAD:(h + 1) * MLA_HEAD_PAD]) for h in hs]
        if masked is not None:
            row = lax.broadcasted_iota(jnp.int32, (tq, tk), 0)
            col = lax.broadcasted_iota(jnp.int32, (tq, tk), 1) + masked * tk
            s = [jnp.where(col <= row, x, -1e30) for x in s]
        m_new = [jnp.maximum(carry[h][0], jnp.max(s[h], axis=-1, keepdims=True)) for h in hs]
        alpha = [jnp.exp2(carry[h][0] - m_new[h]) for h in hs]
        p = [jnp.exp2(s[h] - m_new[h]) for h in hs]
        l_new = [alpha[h] * carry[h][1] + jnp.sum(p[h], axis=-1, keepdims=True) for h in hs]
        pv = [_dot(p[h].astype(BF16), v_ref[pl.ds(start, tk), h * MLA_V:(h + 1) * MLA_V]) for h in hs]
        return tuple((m_new[h], l_new[h], alpha[h] * carry[h][2] + pv[h]) for h in hs)

    init = tuple((jnp.full((tq, 1), -1e30, F32), jnp.zeros((tq, 1), F32), jnp.zeros((tq, MLA_V), F32))
                 for _ in hs)
    per_q = tq // tk
    carry = lax.fori_loop(0, qi * per_q, lambda ki, c: step(ki, c, None), init)
    for j in range(per_q):
        carry = step(qi * per_q + j, carry, j)
    for h in hs:
        o_ref[:, h * MLA_V:(h + 1) * MLA_V] = (carry[h][2] / carry[h][1]).astype(o_ref.dtype)


def mla_attention(q, k, v, batch, seq, *, tq=1024, tk=1024, hpg=2):
    tq = _tile(seq, tq)
    tk = _tile(tq, tk)
    nq = seq // tq
    return pl.pallas_call(
        functools.partial(_attn_kernel, tq=tq, tk=tk, hpg=hpg),
        grid=(batch, MLA_HEADS // hpg, nq),
        in_specs=[pl.BlockSpec((tq, hpg * MLA_HEAD_PAD), lambda b, h, i: (b * nq + i, h)),
                  pl.BlockSpec((seq, hpg * MLA_HEAD_PAD), lambda b, h, i: (b, h)),
                  pl.BlockSpec((seq, hpg * MLA_V), lambda b, h, i: (b, h))],
        out_specs=pl.BlockSpec((tq, hpg * MLA_V), lambda b, h, i: (b * nq + i, h)),
        out_shape=jax.ShapeDtypeStruct((batch * seq, MLA_HEADS * MLA_V), BF16),
        compiler_params=_cparams(("parallel", "parallel", "arbitrary")),
        name="mla_attention",
    )(q, k, v)


def _split2(x):
    hi = x.astype(BF16)
    return hi, (x - hi.astype(F32)).astype(BF16)


def _dot_x3(a, b):
    ah, al = _split2(a)
    bh, bl = _split2(b)
    return _dot(jnp.concatenate([ah, al, ah], axis=1), jnp.concatenate([bh, bh, bl], axis=0))


def _dot_nt_x3(a, b):
    ah, al = _split2(a)
    bh, bl = _split2(b)
    return _dot_nt(jnp.concatenate([ah, al, ah], axis=1), jnp.concatenate([bh, bh, bl], axis=1))


def _dot_bf(a, b):
    return _dot(a.astype(BF16), b.astype(BF16))


def _unit_lower_inverse(a_list, levels, dot):
    n = a_list[0].shape[0]
    eye = (lax.broadcasted_iota(jnp.int32, (n, n), 0) == lax.broadcasted_iota(jnp.int32, (n, n), 1)).astype(F32)
    p = [-a for a in a_list]
    r = [eye + x for x in p]
    p = [dot(x, x) for x in p]
    for _ in range(levels - 2):
        s = [dot(jnp.concatenate([ri, pi], axis=0), pi) for ri, pi in zip(r, p)]
        r = [ri + si[:n] for ri, si in zip(r, s)]
        p = [si[n:] for si in s]
    return [ri + dot(ri, pi) for ri, pi in zip(r, p)]


def _cumsum_rows(x):
    c = x.shape[0]
    row = lax.broadcasted_iota(jnp.int32, x.shape, 0)
    s = 1
    while s < c:
        x = x + jnp.where(row >= s, pltpu.roll(x, s, 0), 0.0)
        s *= 2
    return x


def _col_of_row(row_vec, n):
    eye = lax.broadcasted_iota(jnp.int32, (n, n), 0) == lax.broadcasted_iota(jnp.int32, (n, n), 1)
    return jnp.sum(jnp.where(eye, jnp.broadcast_to(row_vec, (n, n)), 0.0), axis=1, keepdims=True)


def _gdn_kernel(gq_ref, gk_ref, gv_ref, gz_ref, sm_ref, cw_ref, alog_ref, dtb_ref, on_ref, o_ref,
                xq_scr, xk_scr, xv_scr, state_scr, *, chunk):
    c = chunk
    heads = range(GDN_HEADS)
    hs = [slice(h * LANES, (h + 1) * LANES) for h in heads]

    @pl.when(pl.program_id(1) == 0)
    def _():
        for scr in (xq_scr, xk_scr, xv_scr):
            scr[0:8, :] = jnp.zeros((8, scr.shape[1]), F32)
        state_scr[...] = jnp.zeros_like(state_scr)

    def conv_silu(x_ref, scr, widx):
        scr[8:8 + c, :] = x_ref[...].astype(F32)
        acc = cw_ref[widx, GDN_CONV - 1:GDN_CONV, :] * scr[8:8 + c, :]
        for j in range(GDN_CONV - 1):
            s = GDN_CONV - 1 - j
            acc = acc + cw_ref[widx, j:j + 1, :] * scr[8 - s:8 - s + c, :]
        scr[0:8, :] = scr[c:c + 8, :]
        return _silu(acc)

    def l2n(x):
        return x * lax.rsqrt(jnp.sum(x * x, axis=-1, keepdims=True) + 1e-6)

    def bf(x):
        return x.astype(BF16)

    qa = conv_silu(gq_ref, xq_scr, 0)
    ka = conv_silu(gk_ref, xk_scr, 1)
    va = conv_silu(gv_ref, xv_scr, 2)
    q = [l2n(qa[:, s]) * (GDN_DK ** -0.5) for s in hs]
    k = [l2n(ka[:, s]) for s in hs]
    v = [va[:, s] for s in hs]

    sm = sm_ref[...].astype(F32)
    lane = lax.broadcasted_iota(jnp.int32, (c, LANES), 1)
    beta = [_sigmoid(jnp.sum(jnp.where(lane == h, sm, 0.0), axis=-1, keepdims=True)) for h in heads]
    ga = [jnp.sum(jnp.where(lane == GDN_HEADS + h, sm, 0.0), axis=-1, keepdims=True) for h in heads]
    g = [-jnp.exp(alog_ref[:, hs[h]]) * _softplus(jnp.broadcast_to(ga[h], (c, LANES)) + dtb_ref[:, hs[h]])
         for h in heads]

    ri = lax.broadcasted_iota(jnp.int32, (c, c), 0)
    cj = lax.broadcasted_iota(jnp.int32, (c, c), 1)
    gam = [_cumsum_rows(x) for x in g]
    decay = [jnp.exp(jnp.where(cj <= ri, x - x.T, -1e30)) for x in gam]
    eg = [jnp.exp(x) for x in gam]
    gl = [x[c - 1:c, :] for x in gam]
    kb = [k[h] * beta[h] for h in heads]
    vb = [v[h] * beta[h] for h in heads]

    a_mat = [jnp.where(cj < ri, _dot_nt_x3(kb[h], k[h]) * decay[h], 0.0) for h in heads]
    tinv = _unit_lower_inverse(a_mat, 7, _dot_x3)
    rhs = [jnp.concatenate([kb[h] * eg[h], vb[h]], axis=1) for h in heads]
    wu = [bf(_dot_x3(tinv[h], rhs[h])) for h in heads]
    attn = [_dot_nt(bf(q[h]), bf(k[h])) * decay[h] for h in heads]
    k_dec = [k[h] * jnp.exp(gl[h] - gam[h]) for h in heads]
    lhs = [bf(jnp.concatenate([attn[h], k_dec[h].T], axis=0)) for h in heads]
    z = [_dot(lhs[h], wu[h]) for h in heads]
    state = [state_scr[h] for h in heads]
    l2 = [bf(jnp.concatenate([q[h] * eg[h] - z[h][:c, :LANES], -z[h][c:, :LANES]], axis=0)) for h in heads]
    rr = [_dot(l2[h], bf(state[h])) for h in heads]
    for h in heads:
        state_scr[h] = state[h] * jnp.exp(gl[h]) + rr[h][c:] + z[h][c:, LANES:]
        o = rr[h][:c] + z[h][:c, LANES:]
        o_ref[:, hs[h]] = (_rms(o, on_ref[...]) * _silu(gz_ref[:, hs[h]].astype(F32))).astype(o_ref.dtype)


def gdn_mixer(hcat, conv_w, a_log, dt_bias, out_norm, batch, seq):
    c = GDN_CHUNK
    assert c == LANES and seq % c == 0
    nc = seq // c
    width = GDN_HEADS * LANES
    blk = lambda off: pl.BlockSpec((c, width), lambda b, i, off=off: (b * nc + i, off // width))
    const2 = lambda b, i: (0, 0)
    cw = conv_w.reshape(GDN_CONV, 3, width).transpose(1, 0, 2)
    alog = jnp.repeat(a_log.astype(F32), LANES).reshape(1, width)
    dtb = jnp.repeat(dt_bias.astype(F32), LANES).reshape(1, width)
    return pl.pallas_call(
        functools.partial(_gdn_kernel, chunk=c),
        grid=(batch, nc),
        in_specs=[blk(HC_GQ), blk(HC_GK), blk(HC_GV), blk(HC_GZ),
                  pl.BlockSpec((c, LANES), lambda b, i: (b * nc + i, HC_SM // LANES)),
                  pl.BlockSpec((3, GDN_CONV, width), lambda b, i: (0, 0, 0)),
                  pl.BlockSpec((1, width), const2),
                  pl.BlockSpec((1, width), const2),
                  pl.BlockSpec((1, GDN_DV), const2)],
        out_specs=pl.BlockSpec((c, width), lambda b, i: (b * nc + i, 0)),
        out_shape=jax.ShapeDtypeStruct((batch * seq, width), BF16),
        scratch_shapes=[pltpu.VMEM((c + 8, width), F32)] * 3 + [pltpu.VMEM((GDN_HEADS, GDN_DK, GDN_DV), F32)],
        compiler_params=_cparams(("parallel", "arbitrary")),
        name="gdn_mixer",
    )(hcat, hcat, hcat, hcat, hcat, cw, alog, dtb, out_norm)


LZ_W, LZ_A, LZ_G, LZ_V, LZ_COLS = 0, 128, 256, 512, 640


def _rwkv_mix_kernel(h_ref, hp_ref, g_ref, mix_ref, a1_ref, o_ref, z_ref, *, tm, seq, has_v):
    i = pl.program_id(0)
    g = g_ref[...]
    xn = _rms(h_ref[...], g)
    prev_last = _rms(hp_ref[...], g)[7:8, :]
    prev_last = jnp.where((i * tm) % seq == 0, 0.0, prev_last)
    rows = lax.broadcasted_iota(jnp.int32, xn.shape, 0)
    xprev = jnp.where(rows == 0, prev_last, pltpu.roll(xn, 1, 0))
    xx = xprev - xn

    def mixed(j):
        return (xn + xx * mix_ref[j:j + 1, :]).astype(BF16)

    o_ref[0] = mixed(0)
    o_ref[1] = mixed(2)
    xv = mixed(3)
    o_ref[2] = xv
    z_ref[:, LZ_W:LZ_A] = jnp.tanh(_dot(mixed(1), a1_ref[:, LZ_W:LZ_A])).astype(BF16)
    z_ref[:, LZ_A:LZ_G] = _dot(mixed(4), a1_ref[:, LZ_A:LZ_G]).astype(BF16)
    z_ref[:, LZ_G:LZ_V] = _sigmoid(_dot(mixed(5), a1_ref[:, LZ_G:LZ_V])).astype(BF16)
    if has_v:
        z_ref[:, LZ_V:] = _dot(xv, a1_ref[:, LZ_V:]).astype(BF16)
    else:
        z_ref[:, LZ_V:] = jnp.zeros((tm, LZ_COLS - LZ_V), BF16)


def rwkv_mix(h, g, mix, a1_cat, seq, *, has_v, tm=512):
    m, d = h.shape
    tm = _tile(seq, tm)
    return pl.pallas_call(
        functools.partial(_rwkv_mix_kernel, tm=tm, seq=seq, has_v=has_v),
        grid=(m // tm,),
        in_specs=[pl.BlockSpec((tm, d), lambda i: (i, 0)),
                  pl.BlockSpec((8, d), lambda i: (jnp.maximum(i * (tm // 8) - 1, 0), 0)),
                  pl.BlockSpec((1, d), lambda i: (0, 0)),
                  pl.BlockSpec((6, d), lambda i: (0, 0)),
                  pl.BlockSpec((d, LZ_COLS), lambda i: (0, 0))],
        out_specs=[pl.BlockSpec((3, tm, d), lambda i: (0, i, 0)),
                   pl.BlockSpec((tm, LZ_COLS), lambda i: (i, 0))],
        out_shape=[jax.ShapeDtypeStruct((3, m, d), BF16),
                   jax.ShapeDtypeStruct((m, LZ_COLS), BF16)],
        compiler_params=_cparams(("parallel",)),
        name="rwkv_mix",
    )(h, h, g, mix, a1_cat)


def _lora_up_kernel(z_ref, b_ref, bias_ref, *rest, has_v):
    if has_v:
        v_ref, vf_ref, lw_ref, a_ref, gt_ref, vo_ref = rest
    else:
        lw_ref, a_ref, gt_ref = rest
    y = _dot(z_ref[:, LZ_W:LZ_A], b_ref[LZ_W:LZ_A, :]) + bias_ref[0:1, :]
    lw_ref[...] = -jnp.exp(-_softplus(-y) - 0.5)
    y = _dot(z_ref[:, LZ_A:LZ_G], b_ref[LZ_A:LZ_G, :]) + bias_ref[1:2, :]
    a_ref[...] = _sigmoid(y).astype(a_ref.dtype)
    gt_ref[...] = _dot(z_ref[:, LZ_G:LZ_V], b_ref[LZ_G:LZ_V, :]).astype(gt_ref.dtype)
    if has_v:
        y = _dot(z_ref[:, LZ_V:], b_ref[LZ_V:, :]) + bias_ref[2:3, :]
        v = v_ref[...].astype(F32)
        vo_ref[...] = (v + (vf_ref[...].astype(F32) - v) * _sigmoid(y)).astype(vo_ref.dtype)


def lora_up(z, b_cat, bias, v=None, v_first=None, *, tm=512):
    m = z.shape[0]
    d = b_cat.shape[1]
    has_v = v is not None
    tm = _tile(m, tm)
    row = lambda i: (i, 0)
    const = lambda i: (0, 0)
    big = pl.BlockSpec((tm, d), row)
    n_bf = 3 if has_v else 2
    return pl.pallas_call(
        functools.partial(_lora_up_kernel, has_v=has_v),
        grid=(m // tm,),
        in_specs=[pl.BlockSpec((tm, LZ_COLS), row), pl.BlockSpec((LZ_COLS, d), const),
                  pl.BlockSpec((3, d), const)] + ([big, big] if has_v else []),
        out_specs=[big] * (1 + n_bf),
        out_shape=[jax.ShapeDtypeStruct((m, d), F32)] + [jax.ShapeDtypeStruct((m, d), BF16)] * n_bf,
        compiler_params=_cparams(("parallel",)),
        name="lora_up",
    )(z, b_cat, bias, *((v, v_first) if has_v else ()))


def _rwkv_kernel(*refs, chunk, pairs, nsub):
    state_scr = refs[-1]

    @pl.when(pl.program_id(2) == 0)
    def _():
        state_scr[...] = jnp.zeros_like(state_scr)

    for sub in range(nsub):
        _rwkv_chunk(slice(sub * chunk, (sub + 1) * chunk), *refs, chunk=chunk, pairs=pairs)


def _rwkv_chunk(rows, r_ref, k_ref, v_ref, a_ref, lw_ref, gt_ref, kk_ref, ka_ref, rk_ref, lnw_ref, lnb_ref,
                o_ref, state_scr, *, chunk, pairs):
    c = chunk
    ps = range(pairs)
    sls = [slice(p * LANES, (p + 1) * LANES) for p in ps]

    lane =lax.broadcasted_iota(jnp.int32, (c, LANES), 1)
    m0 = lane < RWKV_HEAD
    r2 =lax.broadcasted_iota(jnp.int32, (2 * c, 2 * c), 0)
    c2 = lax.broadcasted_iota(jnp.int32, (2 * c, 2 * c), 1)
    strict2 = c2 < r2
    rw = lax.broadcasted_iota(jnp.int32, (c, 2 * c), 0)
    cw = lax.broadcasted_iota(jnp.int32, (c, 2 * c), 1)
    incl_w = jnp.where(cw >= c, cw - c, cw) <= rw
    zero_blk = jnp.zeros((2 * c, LANES), BF16)

    def seg_sum(x):
        s0 = jnp.sum(jnp.where(m0, x, 0.0), axis=-1, keepdims=True)
        s1 = jnp.sum(jnp.where(m0, 0.0, x), axis=-1, keepdims=True)
        return jnp.where(m0, s0, s1)

    def stack(x):
        return jnp.concatenate([jnp.where(m0, x, 0.0), jnp.where(m0, 0.0, x)], axis=0)

    def bf(x):
        return x.astype(BF16)

    r = [r_ref[rows, s].astype(F32) for s in sls]
    k = [k_ref[rows, s].astype(F32) for s in sls]
    v = [v_ref[rows, s].astype(F32) for s in sls]
    a = [a_ref[rows, s].astype(F32) for s in sls]
    lw = [lw_ref[rows, s] for s in sls]
    kk = [k[p] * kk_ref[:, sls[p]] for p in ps]
    kk = [x * lax.rsqrt(seg_sum(x * x) + 1e-12) for x in kk]
    kh = [k[p] * (1.0 + (a[p] - 1.0) * ka_ref[:, sls[p]]) for p in ps]
    bv = [kk[p] * a[p] for p in ps]

    cum = [_cumsum_rows(x) for x in lw]
    cl = [x[c - 1:c, :] for x in cum]
    e_neg = [jnp.exp(-x) for x in cum]
    e_end = [jnp.exp(cl[p] - cum[p]) for p in ps]
    rg = [r[p] * jnp.exp(cum[p]) for p in ps]
    rgc = [bf(x) for x in rg]
    xs = [bf(stack(kk[p] * jnp.exp(cum[p] - lw[p]))) for p in ps]
    bs = [bf(stack(bv[p] * e_neg[p])) for p in ps]
    ks = [bf(stack(kh[p] * e_neg[p])) for p in ps]
    vs = [bf(stack(v[p])) for p in ps]
    kht = [stack(kh[p] * e_end[p]).T for p in ps]
    bht = [stack(bv[p] * e_end[p]).T for p in ps]

    ab = [_dot_nt(xs[p], jnp.concatenate([bs[p], ks[p]], axis=0)) for p in ps]
    akk = [jnp.where(strict2, x[:, :2 * c], 0.0) for x in ab]
    akv = [bf(jnp.where(strict2, x[:, 2 * c:], 0.0)) for x in ab]
    tinv = [bf(x) for x in _unit_lower_inverse(akk, 6, _dot_bf)]
    w1 = [bf(_dot(akv[p], vs[p])) for p in ps]
    m12 = [bf(_dot(tinv[p], jnp.concatenate([xs[p], w1[p]], axis=1))) for p in ps]
    bb = [_dot_nt(rgc[p], jnp.concatenate([ks[p], bs[p]], axis=0)) for p in ps]
    lhs = [bf(jnp.concatenate([
        jnp.concatenate([jnp.where(incl_w, x[:, :2 * c], 0.0), -jnp.where(incl_w, x[:, 2 * c:], 0.0)], axis=1),
        jnp.concatenate([kht[p], -bht[p]], axis=1)], axis=0)) for p, x in enumerate(bb)]
    rhs = [jnp.concatenate([jnp.concatenate([vs[p], zero_blk], axis=1),
                            jnp.concatenate([m12[p][:, LANES:], m12[p][:, :LANES]], axis=1)], axis=0)
           for p in ps]
    z = [_dot(lhs[p], rhs[p]) for p in ps]
    state = [state_scr[p] for p in ps]
    l2 = [bf(jnp.concatenate([rg[p] + z[p][:c, LANES:], z[p][c:, LANES:]], axis=0)) for p in ps]
    rr = [_dot(l2[p], bf(state[p])) for p in ps]
    for p in ps:
        sl = sls[p]
        state_scr[p] = state[p] * _col_of_row(jnp.exp(cl[p]), LANES) + rr[p][c:] + z[p][c:, :LANES]
        y = rr[p][:c] + z[p][:c, :LANES]
        mu = seg_sum(y) * (1.0 / RWKV_HEAD)
        yc = y - mu
        var = seg_sum(yc * yc) * (1.0 / RWKV_HEAD)
        yn = yc * lax.rsqrt(var + RWKV_LN_EPS) * lnw_ref[:, sl] + lnb_ref[:, sl]
        bonus = seg_sum(r[p] * kh[p] * rk_ref[:, sl]) * v[p]
        o_ref[rows, sl] = ((yn + bonus) * gt_ref[rows, sl].astype(F32)).astype(o_ref.dtype)


def rwkv_recurrence(r, k, v, a, lw, gate, k_k, k_a, r_k, ln_w, ln_b, batch, seq, *, pairs=16, nsub=2):
    c = RWKV_CHUNK
    m, d = r.shape
    nc = seq // (c * nsub)
    width = pairs * LANES
    blk = pl.BlockSpec((c * nsub, width), lambda b, hp, i: (b * nc + i, hp))
    par = pl.BlockSpec((1, width), lambda b, hp, i: (0, hp))
    return pl.pallas_call(
        functools.partial(_rwkv_kernel, chunk=c, pairs=pairs, nsub=nsub),
        grid=(batch, d // width, nc),
        in_specs=[blk] * 6 + [par] * 5,
        out_specs=blk,
        out_shape=jax.ShapeDtypeStruct((m, d), BF16),
        scratch_shapes=[pltpu.VMEM((pairs, LANES, LANES), F32)],
        compiler_params=_cparams(("parallel", "parallel", "arbitrary")),
        name="rwkv_recurrence",
    )(r, k, v, a, lw, gate, k_k, k_a, r_k, ln_w, ln_b)


def _pad_cols(w, n):
    return jnp.pad(w, ((0, 0), (0, n - w.shape[1])))


def _rot_cols(w):
    half = MLA_ROPE // 2
    return jnp.concatenate([-w[..., half:], w[..., :half]], axis=-1)


def _input_projection_weight(w_in):
    o_kr = MLA_Q_LORA + MLA_KV_LORA
    o_g = o_kr + MLA_ROPE
    o_sm = o_g + 4 * GDN_HEADS * GDN_DK
    w_in = w_in.astype(BF16)
    w_kr = w_in[..., o_kr:o_g]
    small = w_in[..., o_sm:]
    small = jnp.pad(small, ((0, 0), (0, 0), (0, LANES - small.shape[-1])))
    return jnp.concatenate([w_in[..., :o_kr], w_in[..., o_g:o_sm], w_kr, _rot_cols(w_kr), small], axis=-1)


def _hybrid_layer(h, ct, st, batch, seq, g_pre, g_post, w_cat, q_norm, w_uq, kv_norm, w_ukv, conv_w, a_log,
                  dt_bias, out_norm, w_out, e):
    d = D_MODEL
    hcat = norm_matmul(h, g_pre.reshape(1, d), w_cat, e, tn=1792, out_dtype=BF16)

    wq = w_uq.reshape(MLA_Q_LORA, MLA_HEADS, MLA_QK)
    wq_rope = wq[..., MLA_NOPE:]
    wq = jnp.concatenate([wq[..., :MLA_NOPE], wq_rope, _rot_cols(wq_rope)], axis=-1)
    wq = wq.reshape(MLA_Q_LORA, MLA_HEADS * MLA_HEAD_PAD).astype(BF16)
    q, k, v = mla_proj(hcat, q_norm.reshape(1, -1), kv_norm.reshape(1, -1), wq, w_ukv.astype(BF16), ct, st)
    mla_out = mla_attention(q, k, v, batch, seq)
    gdn_out = gdn_mixer(hcat, conv_w, a_log, dt_bias, out_norm.reshape(1, -1), batch, seq)
    return matmul_norm_residual([mla_out, gdn_out], w_out, e, g_post.reshape(1, d), h, tn=d)


def _rwkv_layer(h, v_first, batch, seq, g_pre, g_post, mix, w_r, w_k, w_v, w_o, o, w0, w1, w2, a0, a1, a2,
                g1, g2, k_k, k_a, r_k, ln_w, ln_b, vres):
    d = D_MODEL
    row = lambda t: t.reshape(1, d).astype(F32)
    padc = lambda w, n: _pad_cols(w, n)
    padr = lambda w, n: jnp.pad(w, ((0, n - w.shape[0]), (0, 0)))
    has_v = vres is not None
    v0, v1, v2 = vres if has_v else (jnp.zeros((d,), F32), jnp.zeros((d, 0), F32), jnp.zeros((0, d), F32))
    a1_cat = jnp.concatenate([padc(w1, LZ_A - LZ_W), padc(a1, LZ_G - LZ_A), padc(g1, LZ_V - LZ_G),
                              padc(v1, LZ_COLS - LZ_V)], axis=1).astype(BF16)
    b_cat = jnp.concatenate([padr(w2, LZ_A - LZ_W), padr(a2, LZ_G - LZ_A), padr(g2, LZ_V - LZ_G),
                             padr(v2, LZ_COLS - LZ_V)], axis=0).astype(BF16)
    bias = jnp.stack([w0, a0, v0]).astype(F32)
    xs, z = rwkv_mix(h, g_pre.reshape(1, d), mix, a1_cat, seq, has_v=has_v)
    r = matmul(xs, 0, w_r, o, out_dtype=BF16)
    k = matmul(xs, 1, w_k, o, out_dtype=BF16)
    v = matmul(xs, 2, w_v, o, out_dtype=BF16)
    if has_v:
        lw, a, gate, v = lora_up(z, b_cat, bias, v, v_first)
    else:
        lw, a, gate = lora_up(z, b_cat, bias)
        v_first = v
    y = rwkv_recurrence(r, k, v, a, lw, gate, row(k_k), row(k_a), row(r_k), row(ln_w), row(ln_b), batch, seq)
    return matmul_norm_residual([y], w_o, o, g_post.reshape(1, d), h, tn=d), v_first


def _ffn_layer(h, g_pre, g_post, w_gate, w_up, w_down, layer):
    d = D_MODEL
    act = ffn_up(h, g_pre.reshape(1, d), w_gate, w_up, layer)
    return matmul_norm_residual([act], w_down, layer, g_post.reshape(1, d), h)


def _rope_tables(positions):
    inv_freq = 1.0 / (ROPE_THETA ** (jnp.arange(0, MLA_ROPE, 2, dtype=F32) / MLA_ROPE))
    ang = positions.astype(F32).reshape(-1, 1) * inv_freq
    zeros = jnp.zeros((ang.shape[0], LANES - MLA_ROPE), F32)
    cos, sin = jnp.cos(ang), jnp.sin(ang)
    return jnp.concatenate([cos, cos, zeros], axis=1), jnp.concatenate([sin, sin, zeros], axis=1)


def kernel(x, positions, norm_mix_pre, norm_mix_post, norm_ffn_pre, norm_ffn_post, hyb_w_in, mla_q_norm, mla_w_uq, mla_kv_norm, mla_w_ukv, gdn_conv_w, gdn_a_log, gdn_dt_bias, gdn_out_norm, hyb_w_out, rwkv_mix, rwkv_w_r, rwkv_w_k, rwkv_w_v, rwkv_w_o, rwkv_w0, rwkv_w1, rwkv_w2, rwkv_a0, rwkv_a1, rwkv_a2, rwkv_g1, rwkv_g2, rwkv_k_k, rwkv_k_a, rwkv_r_k, rwkv_ln_w, rwkv_ln_b, rwkv_v0, rwkv_v1, rwkv_v2, ffn_w_gate, ffn_w_up, ffn_w_down):
    batch, seq, d = x.shape
    h = x.reshape(batch * seq, d)
    ct, st = _rope_tables(positions)
    bf = lambda w: w.astype(BF16)
    ffn_w_gate, ffn_w_up, ffn_w_down, hyb_w_out = bf(ffn_w_gate), bf(ffn_w_up), bf(ffn_w_down), bf(hyb_w_out)
    rwkv_w_r, rwkv_w_k, rwkv_w_v, rwkv_w_o = bf(rwkv_w_r), bf(rwkv_w_k), bf(rwkv_w_v), bf(rwkv_w_o)
    hyb_w_cat = _input_projection_weight(hyb_w_in)
    v_first = None
    for layer in range(DEPTH):
        if layer % 2 == 0:
            e = layer // 2
            h = _hybrid_layer(h, ct, st, batch, seq, norm_mix_pre[layer], norm_mix_post[layer], hyb_w_cat,
                              mla_q_norm[e], mla_w_uq[e], mla_kv_norm[e], mla_w_ukv[e], gdn_conv_w[e],
                              gdn_a_log[e], gdn_dt_bias[e], gdn_out_norm[e], hyb_w_out, e)
        else:
            o = layer // 2
            vres = None if o == 0 else (rwkv_v0[o - 1], rwkv_v1[o - 1], rwkv_v2[o - 1])
            h, v_first = _rwkv_layer(h, v_first, batch, seq, norm_mix_pre[layer], norm_mix_post[layer],
                                     rwkv_mix[o], rwkv_w_r, rwkv_w_k, rwkv_w_v, rwkv_w_o, o,
                                     rwkv_w0[o], rwkv_w1[o], rwkv_w2[o], rwkv_a0[o], rwkv_a1[o], rwkv_a2[o],
                                     rwkv_g1[o], rwkv_g2[o], rwkv_k_k[o], rwkv_k_a[o], rwkv_r_k[o],
                                     rwkv_ln_w[o], rwkv_ln_b[o], vres)
        h = _ffn_layer(h, norm_ffn_pre[layer], norm_ffn_post[layer], ffn_w_gate, ffn_w_up, ffn_w_down, layer)
    return h.reshape(batch, seq, d)
```

```python
import functools

import jax
import jax.numpy as jnp
import numpy as np
from jax import lax
from jax.experimental import pallas as pl
from jax.experimental.pallas import tpu as pltpu

F32 = jnp.float32
BF16 = jnp.bfloat16
HI = lax.Precision.HIGHEST

D_MODEL = 2048
DEPTH = 4
LANES = 128
VMEM_LIMIT = 56 * 1024 * 1024

MLA_HEADS = 8
MLA_Q_LORA = 512
MLA_KV_LORA = 512
MLA_NOPE = 128
MLA_ROPE = 64
MLA_V = 128
MLA_QK = MLA_NOPE + MLA_ROPE
MLA_HEAD_PAD = 256
ROPE_THETA = 10000.0

GDN_HEADS = 8
GDN_DK = 128
GDN_DV = 128
GDN_CONV = 4
GDN_CHUNK = 128

RWKV_HEAD = 64
RWKV_HEADS = D_MODEL // RWKV_HEAD
RWKV_CHUNK = 64
RWKV_LN_EPS = 64e-5
RMS_EPS = 1e-6

HC_CQ, HC_CKV, HC_GQ, HC_GK, HC_GV, HC_GZ, HC_KR, HC_SM = 0, 512, 1024, 2048, 3072, 4096, 5120, 5248
HC_COLS = 5376


def _cparams(sem):
    return pltpu.CompilerParams(dimension_semantics=sem, vmem_limit_bytes=VMEM_LIMIT)


def _tile(n, pref):
    t = min(n, pref)
    while n % t:
        t -= 8
    return t


def _layer_spec(w, layer, blk, imap):
    return pl.BlockSpec((None,) + blk, lambda *g: (layer,) + imap(*g))


def _dot(a, b, prec=None):
    return jnp.dot(a, b, preferred_element_type=F32, precision=prec)


def _dot_nt(a, b, prec=None):
    return lax.dot_general(a, b, (((1,), (1,)), ((), ())), preferred_element_type=F32, precision=prec)


def _rms(x, g, eps=RMS_EPS):
    return x * lax.rsqrt(jnp.mean(x * x, axis=-1, keepdims=True) + eps) * g


def _sigmoid(x):
    return 1.0 / (1.0 + jnp.exp(-x))


def _silu(x):
    return x * _sigmoid(x)


def _softplus(x):
    return jnp.maximum(x, 0.0) + jnp.log(1.0 + jnp.exp(-jnp.abs(x)))


def _norm_mm_kernel(x_ref, g_ref, w_ref, o_ref, xn_ref):
    @pl.when(pl.program_id(1) == 0)
    def _():
        xn_ref[...] = _rms(x_ref[...], g_ref[...]).astype(BF16)

    o_ref[...] = _dot(xn_ref[...], w_ref[...]).astype(o_ref.dtype)


def norm_matmul(x, g, w, layer, *, tn, out_dtype=F32, tm=1024):
    m, k = x.shape
    n = w.shape[2]
    tm = _tile(m, tm)
    return pl.pallas_call(
        _norm_mm_kernel,
        grid=(m // tm, n // tn),
        in_specs=[pl.BlockSpec((tm, k), lambda i, j: (i, 0)),
                  pl.BlockSpec((1, k), lambda i, j: (0, 0)),
                  _layer_spec(w, layer, (k, tn), lambda i, j: (0, j))],
        out_specs=pl.BlockSpec((tm, tn), lambda i, j: (i, j)),
        out_shape=jax.ShapeDtypeStruct((m, n), out_dtype),
        scratch_shapes=[pltpu.VMEM((tm, k), BF16)],
        compiler_params=_cparams(("parallel", "arbitrary")),
        name="norm_matmul",
    )(x, g, w)


def _ffn_up_kernel(x_ref, g_ref, wg_ref, wu_ref, o_ref, xn_ref):
    @pl.when(pl.program_id(1) == 0)
    def _():
        xn_ref[...] = _rms(x_ref[...], g_ref[...]).astype(BF16)

    xn = xn_ref[...]
    gate = _dot(xn, wg_ref[...])
    up = _dot(xn, wu_ref[...])
    o_ref[...] = (_silu(gate) * up).astype(o_ref.dtype)


def ffn_up(h, g, wg, wu, layer, *, tm=1024, tn=512):
    m, k = h.shape
    n = wg.shape[2]
    tm = _tile(m, tm)
    tn = _tile(n, tn)
    return pl.pallas_call(
        _ffn_up_kernel,
        grid=(m // tm, n // tn),
        in_specs=[pl.BlockSpec((tm, k), lambda i, j: (i, 0)),
                  pl.BlockSpec((1, k), lambda i, j: (0, 0)),
                  _layer_spec(wg, layer, (k, tn), lambda i, j: (0, j)),
                  _layer_spec(wu, layer, (k, tn), lambda i, j: (0, j))],
        out_specs=pl.BlockSpec((tm, tn), lambda i, j: (i, j)),
        out_shape=jax.ShapeDtypeStruct((m, n), BF16),
        scratch_shapes=[pltpu.VMEM((tm, k), BF16)],
        compiler_params=_cparams(("parallel", "arbitrary")),
        name="ffn_up",
    )(h, g, wg, wu)


def _mm_norm_res_kernel(*refs, tn, nb, n_a):
    a_refs = refs[:n_a]
    w_ref, g_ref, h_ref, o_ref, f_ref = refs[n_a:]
    j = pl.program_id(1)
    k0 = 0
    f = None
    for a_ref in a_refs:
        part = _dot(a_ref[...], w_ref[k0:k0 + a_ref.shape[1], :])
        f = part if f is None else f + part
        k0 += a_ref.shape[1]
    f_ref[j] = f

    @pl.when(j == nb - 1)
    def _():
        ssq = jnp.sum(f_ref[0] * f_ref[0], axis=-1, keepdims=True)
        for b in range(1, nb):
            ssq = ssq + jnp.sum(f_ref[b] * f_ref[b], axis=-1, keepdims=True)
        inv = lax.rsqrt(ssq * (1.0 / (nb * tn)) + RMS_EPS)
        for b in range(nb):
            cs = slice(b * tn, (b + 1) * tn)
            o_ref[:, cs] = h_ref[:, cs] + f_ref[b] * inv * g_ref[:, cs]


def matmul_norm_residual(a_parts, w, layer, g, h, *, tm=512, tn=512):
    m = a_parts[0].shape[0]
    _, k, n = w.shape
    tm = _tile(m, tm)
    tn = _tile(n, tn)
    nb = n // tn
    return pl.pallas_call(
        functools.partial(_mm_norm_res_kernel, tn=tn, nb=nb, n_a=len(a_parts)),
        grid=(m // tm, nb),
        in_specs=[pl.BlockSpec((tm, a.shape[1]), lambda i, j: (i, 0)) for a in a_parts] + [
                  _layer_spec(w, layer, (k, tn), lambda i, j: (0, j)),
                  pl.BlockSpec((1, n), lambda i, j: (0, 0)),
                  pl.BlockSpec((tm, n), lambda i, j: (i, 0))],
        out_specs=pl.BlockSpec((tm, n), lambda i, j: (i, 0)),
        out_shape=jax.ShapeDtypeStruct((m, n), F32),
        scratch_shapes=[pltpu.VMEM((nb, tm, tn), F32)],
        compiler_params=_cparams(("parallel", "arbitrary")),
        name="matmul_norm_residual",
    )(*a_parts, w, g, h)


def _mm_kernel(a_ref, w_ref, o_ref):
    o_ref[...] = _dot(a_ref[...], w_ref[...]).astype(o_ref.dtype)


def matmul(a, idx, w, layer, *, tm=1024, tn=1024, out_dtype=F32):
    _, m, k = a.shape
    n = w.shape[2]
    tm = _tile(m, tm)
    tn = _tile(n, tn)
    return pl.pallas_call(
        _mm_kernel,
        grid=(m // tm, n // tn),
        in_specs=[pl.BlockSpec((None, tm, k), lambda i, j: (idx, i, 0)),
                  _layer_spec(w, layer, (k, tn), lambda i, j: (0, j))],
        out_specs=pl.BlockSpec((tm, tn), lambda i, j: (i, j)),
        out_shape=jax.ShapeDtypeStruct((m, n), out_dtype),
        compiler_params=_cparams(("parallel", "arbitrary")),
        name="matmul",
    )(a, w)


def _rope_group(grp, ct, st):
    return grp * ct + pltpu.roll(grp, 64, 1) * st


def _mla_proj_kernel(cq_ref, ckv_ref, kr_ref, qn_ref, kvn_ref, wq_ref, wkv_ref, ct_ref, st_ref,
                     q_ref, k_ref, v_ref):
    scale = MLA_QK ** -0.5 * np.log2(np.e)
    ct = ct_ref[...]
    st = st_ref[...]
    xq = _rms(cq_ref[...].astype(F32), qn_ref[...]).astype(BF16)
    xkv = _rms(ckv_ref[...].astype(F32), kvn_ref[...]).astype(BF16)
    kr = _rope_group(kr_ref[...].astype(F32), ct, st).astype(BF16)
    for h in range(MLA_HEADS):
        c0 = h * MLA_HEAD_PAD
        yq = _dot(xq, wq_ref[:, c0:c0 + MLA_HEAD_PAD])
        q_ref[:, c0:c0 + LANES] = (yq[:, :LANES] * scale).astype(BF16)
        q_ref[:, c0 + LANES:c0 + 2 * LANES] = (_rope_group(yq[:, LANES:], ct, st) * scale).astype(BF16)
        ykv = _dot(xkv, wkv_ref[:, c0:c0 + MLA_HEAD_PAD])
        k_ref[:, c0:c0 + LANES] = ykv[:, :LANES].astype(BF16)
        k_ref[:, c0 + LANES:c0 + 2 * LANES] = kr
        v_ref[:, h * MLA_V:(h + 1) * MLA_V] = ykv[:, LANES:].astype(BF16)


def mla_proj(hcat, q_norm, kv_norm, wq, wkv, ct, st, *, tm=1024):
    m = hcat.shape[0]
    tm = _tile(m, tm)
    hp = MLA_HEADS * MLA_HEAD_PAD
    row = lambda i: (i, 0)
    const = lambda i: (0, 0)
    return pl.pallas_call(
        _mla_proj_kernel,
        grid=(m // tm,),
        in_specs=[pl.BlockSpec((tm, MLA_Q_LORA), lambda i: (i, HC_CQ // MLA_Q_LORA)),
                  pl.BlockSpec((tm, MLA_KV_LORA), lambda i: (i, HC_CKV // MLA_KV_LORA)),
                  pl.BlockSpec((tm, LANES), lambda i: (i, HC_KR // LANES)),
                  pl.BlockSpec((1, MLA_Q_LORA), const),
                  pl.BlockSpec((1, MLA_KV_LORA), const),
                  pl.BlockSpec((MLA_Q_LORA, hp), const),
                  pl.BlockSpec((MLA_KV_LORA, hp), const),
                  pl.BlockSpec((tm, LANES), row),
                  pl.BlockSpec((tm, LANES), row)],
        out_specs=[pl.BlockSpec((tm, hp), row),
                   pl.BlockSpec((tm, hp), row),
                   pl.BlockSpec((tm, MLA_HEADS * MLA_V), row)],
        out_shape=[jax.ShapeDtypeStruct((m, hp), BF16),
                   jax.ShapeDtypeStruct((m, hp), BF16),
                   jax.ShapeDtypeStruct((m, MLA_HEADS * MLA_V), BF16)],
        compiler_params=_cparams(("parallel",)),
        name="mla_proj",
    )(hcat, hcat, hcat, q_norm, kv_norm, wq, wkv, ct, st)


def _attn_kernel(q_ref, k_ref, v_ref, o_ref, *, tq, tk, hpg):
    qi = pl.program_id(2)
    hs = range(hpg)
    q = [q_ref[:, h * MLA_HEAD_PAD:(h + 1) * MLA_HEAD_PAD] for h in hs]

    def step(ki, carry, masked):
        start = pl.multiple_of(ki * tk, tk)
        s = [_dot_nt(q[h], k_ref[pl.ds(start, tk), h * MLA_HEAD_PAD:(h + 1) * MLA_HEAD_PAD]) for h in hs]
        if masked is not None:
            row = lax.broadcasted_iota(jnp.int32, (tq, tk), 0)
            col = lax.broadcasted_iota(jnp.int32, (tq, tk), 1) + masked * tk
            s = [jnp.where(col <= row, x, -1e30) for x in s]
        m_new = [jnp.maximum(carry[h][0], jnp.max(s[h], axis=-1, keepdims=True)) for h in hs]
        alpha = [jnp.exp2(carry[h][0] - m_new[h]) for h in hs]
        p = [jnp.exp2(s[h] - m_new[h]) for h in hs]
        l_new = [alpha[h] * carry[h][1] + jnp.sum(p[h], axis=-1, keepdims=True) for h in hs]
        pv = [_dot(p[h].astype(BF16), v_ref[pl.ds(start, tk), h * MLA_V:(h + 1) * MLA_V]) for h in hs]
        return tuple((m_new[h], l_new[h], alpha[h] * carry[h][2] + pv[h]) for h in hs)

    init = tuple((jnp.full((tq, 1), -1e30, F32), jnp.zeros((tq, 1), F32), jnp.zeros((tq, MLA_V), F32))
                 for _ in hs)
    per_q = tq // tk
    carry = lax.fori_loop(0, qi * per_q, lambda ki, c: step(ki, c, None), init)
    for j in range(per_q):
        carry = step(qi * per_q + j, carry, j)
    for h in hs:
        o_ref[:, h * MLA_V:(h + 1) * MLA_V] = (carry[h][2] / carry[h][1]).astype(o_ref.dtype)


def mla_attention(q, k, v, batch, seq, *, tq=1024, tk=1024, hpg=2):
    tq = _tile(seq, tq)
    tk = _tile(tq, tk)
    nq = seq // tq
    return pl.pallas_call(
        functools.partial(_attn_kernel, tq=tq, tk=tk, hpg=hpg),
        grid=(batch, MLA_HEADS // hpg, nq),
        in_specs=[pl.BlockSpec((tq, hpg * MLA_HEAD_PAD), lambda b, h, i: (b * nq + i, h)),
                  pl.BlockSpec((seq, hpg * MLA_HEAD_PAD), lambda b, h, i: (b, h)),
                  pl.BlockSpec((seq, hpg * MLA_V), lambda b, h, i: (b, h))],
        out_specs=pl.BlockSpec((tq, hpg * MLA_V), lambda b, h, i: (b * nq + i, h)),
        out_shape=jax.ShapeDtypeStruct((batch * seq, MLA_HEADS * MLA_V), BF16),
        compiler_params=_cparams(("parallel", "parallel", "arbitrary")),
        name="mla_attention",
    )(q, k, v)


def _split2(x):
    hi = x.astype(BF16)
    return hi, (x - hi.astype(F32)).astype(BF16)


def _dot_x3(a, b):
    ah, al = _split2(a)
    bh, bl = _split2(b)
    return _dot(jnp.concatenate([ah, al, ah], axis=1), jnp.concatenate([bh, bh, bl], axis=0))


def _dot_nt_x3(a, b):
    ah, al = _split2(a)
    bh, bl = _split2(b)
    return _dot_nt(jnp.concatenate([ah, al, ah], axis=1), jnp.concatenate([bh, bh, bl], axis=1))


def _dot_bf(a, b):
    return _dot(a.astype(BF16), b.astype(BF16))


def _unit_lower_inverse(a_list, levels, dot):
    n = a_list[0].shape[0]
    eye = (lax.broadcasted_iota(jnp.int32, (n, n), 0) == lax.broadcasted_iota(jnp.int32, (n, n), 1)).astype(F32)
    p = [-a for a in a_list]
    r = [eye + x for x in p]
    p = [dot(x, x) for x in p]
    for _ in range(levels - 2):
        s = [dot(jnp.concatenate([ri, pi], axis=0), pi) for ri, pi in zip(r, p)]
        r = [ri + si[:n] for ri, si in zip(r, s)]
        p = [si[n:] for si in s]
    return [ri + dot(ri, pi) for ri, pi in zip(r, p)]


def _cumsum_rows(x):
    c = x.shape[0]
    row = lax.broadcasted_iota(jnp.int32, x.shape, 0)
    s = 1
    while s < c:
        x = x + jnp.where(row >= s, pltpu.roll(x, s, 0), 0.0)
        s *= 2
    return x


def _col_of_row(row_vec, n):
    eye = lax.broadcasted_iota(jnp.int32, (n, n), 0) == lax.broadcasted_iota(jnp.int32, (n, n), 1)
    return jnp.sum(jnp.where(eye, jnp.broadcast_to(row_vec, (n, n)), 0.0), axis=1, keepdims=True)


def _gdn_kernel(gq_ref, gk_ref, gv_ref, gz_ref, sm_ref, cw_ref, alog_ref, dtb_ref, on_ref, o_ref,
                xq_scr, xk_scr, xv_scr, state_scr, *, chunk):
    c = chunk
    heads = range(GDN_HEADS)
    hs = [slice(h * LANES, (h + 1) * LANES) for h in heads]

    @pl.when(pl.program_id(1) == 0)
    def _():
        for scr in (xq_scr, xk_scr, xv_scr):
            scr[0:8, :] = jnp.zeros((8, scr.shape[1]), F32)
        state_scr[...] = jnp.zeros_like(state_scr)

    def conv_silu(x_ref, scr, widx):
        scr[8:8 + c, :] = x_ref[...].astype(F32)
        acc = cw_ref[widx, GDN_CONV - 1:GDN_CONV, :] * scr[8:8 + c, :]
        for j in range(GDN_CONV - 1):
            s = GDN_CONV - 1 - j
            acc = acc + cw_ref[widx, j:j + 1, :] * scr[8 - s:8 - s + c, :]
        scr[0:8, :] = scr[c:c + 8, :]
        return _silu(acc)

    def l2n(x):
        return x * lax.rsqrt(jnp.sum(x * x, axis=-1, keepdims=True) + 1e-6)

    def bf(x):
        return x.astype(BF16)

    qa = conv_silu(gq_ref, xq_scr, 0)
    ka = conv_silu(gk_ref, xk_scr, 1)
    va = conv_silu(gv_ref, xv_scr, 2)
    q = [l2n(qa[:, s]) * (GDN_DK ** -0.5) for s in hs]
    k = [l2n(ka[:, s]) for s in hs]
    v = [va[:, s] for s in hs]

    sm = sm_ref[...].astype(F32)
    lane = lax.broadcasted_iota(jnp.int32, (c, LANES), 1)
    beta = [_sigmoid(jnp.sum(jnp.where(lane == h, sm, 0.0), axis=-1, keepdims=True)) for h in heads]
    ga = [jnp.sum(jnp.where(lane == GDN_HEADS + h, sm, 0.0), axis=-1, keepdims=True) for h in heads]
    g = [-jnp.exp(alog_ref[:, hs[h]]) * _softplus(jnp.broadcast_to(ga[h], (c, LANES)) + dtb_ref[:, hs[h]])
         for h in heads]

    ri = lax.broadcasted_iota(jnp.int32, (c, c), 0)
    cj = lax.broadcasted_iota(jnp.int32, (c, c), 1)
    gam = [_cumsum_rows(x) for x in g]
    decay = [jnp.exp(jnp.where(cj <= ri, x - x.T, -1e30)) for x in gam]
    eg = [jnp.exp(x) for x in gam]
    gl = [x[c - 1:c, :] for x in gam]
    kb = [k[h] * beta[h] for h in heads]
    vb = [v[h] * beta[h] for h in heads]

    a_mat = [jnp.where(cj < ri, _dot_nt_x3(kb[h], k[h]) * decay[h], 0.0) for h in heads]
    tinv = _unit_lower_inverse(a_mat, 7, _dot_x3)
    rhs = [jnp.concatenate([kb[h] * eg[h], vb[h]], axis=1) for h in heads]
    wu = [bf(_dot_x3(tinv[h], rhs[h])) for h in heads]
    attn = [_dot_nt(bf(q[h]), bf(k[h])) * decay[h] for h in heads]
    k_dec = [k[h] * jnp.exp(gl[h] - gam[h]) for h in heads]
    lhs = [bf(jnp.concatenate([attn[h], k_dec[h].T], axis=0)) for h in heads]
    z = [_dot(lhs[h], wu[h]) for h in heads]
    state = [state_scr[h] for h in heads]
    l2 = [bf(jnp.concatenate([q[h] * eg[h] - z[h][:c, :LANES], -z[h][c:, :LANES]], axis=0)) for h in heads]
    rr = [_dot(l2[h], bf(state[h])) for h in heads]
    for h in heads:
        state_scr[h] = state[h] * jnp.exp(gl[h]) + rr[h][c:] + z[h][c:, LANES:]
        o = rr[h][:c] + z[h][:c, LANES:]
        o_ref[:, hs[h]] = (_rms(o, on_ref[...]) * _silu(gz_ref[:, hs[h]].astype(F32))).astype(o_ref.dtype)


def gdn_mixer(hcat, conv_w, a_log, dt_bias, out_norm, batch, seq):
    c = GDN_CHUNK
    assert c == LANES and seq % c == 0
    nc = seq // c
    width = GDN_HEADS * LANES
    blk = lambda off: pl.BlockSpec((c, width), lambda b, i, off=off: (b * nc + i, off // width))
    const2 = lambda b, i: (0, 0)
    cw = conv_w.reshape(GDN_CONV, 3, width).transpose(1, 0, 2)
    alog = jnp.repeat(a_log.astype(F32), LANES).reshape(1, width)
    dtb = jnp.repeat(dt_bias.astype(F32), LANES).reshape(1, width)
    return pl.pallas_call(
        functools.partial(_gdn_kernel, chunk=c),
        grid=(batch, nc),
        in_specs=[blk(HC_GQ), blk(HC_GK), blk(HC_GV), blk(HC_GZ),
                  pl.BlockSpec((c, LANES), lambda b, i: (b * nc + i, HC_SM // LANES)),
                  pl.BlockSpec((3, GDN_CONV, width), lambda b, i: (0, 0, 0)),
                  pl.BlockSpec((1, width), const2),
                  pl.BlockSpec((1, width), const2),
                  pl.BlockSpec((1, GDN_DV), const2)],
        out_specs=pl.BlockSpec((c, width), lambda b, i: (b * nc + i, 0)),
        out_shape=jax.ShapeDtypeStruct((batch * seq, width), BF16),
        scratch_shapes=[pltpu.VMEM((c + 8, width), F32)] * 3 + [pltpu.VMEM((GDN_HEADS, GDN_DK, GDN_DV), F32)],
        compiler_params=_cparams(("parallel", "arbitrary")),
        name="gdn_mixer",
    )(hcat, hcat, hcat, hcat, hcat, cw, alog, dtb, out_norm)


LZ_W, LZ_A, LZ_G, LZ_V, LZ_COLS = 0, 128, 256, 512, 640


def _rwkv_mix_kernel(h_ref, hp_ref, g_ref, mix_ref, a1_ref, o_ref, z_ref, *, tm, seq, has_v):
    i = pl.program_id(0)
    g = g_ref[...]
    xn = _rms(h_ref[...], g)
    prev_last = _rms(hp_ref[...], g)[7:8, :]
    prev_last = jnp.where((i * tm) % seq == 0, 0.0, prev_last)
    rows = lax.broadcasted_iota(jnp.int32, xn.shape, 0)
    xprev = jnp.where(rows == 0, prev_last, pltpu.roll(xn, 1, 0))
    xx = xprev - xn

    def mixed(j):
        return (xn + xx * mix_ref[j:j + 1, :]).astype(BF16)

    o_ref[0] = mixed(0)
    o_ref[1] = mixed(2)
    xv = mixed(3)
    o_ref[2] = xv
    z_ref[:, LZ_W:LZ_A] = jnp.tanh(_dot(mixed(1), a1_ref[:, LZ_W:LZ_A])).astype(BF16)
    z_ref[:, LZ_A:LZ_G] = _dot(mixed(4), a1_ref[:, LZ_A:LZ_G]).astype(BF16)
    z_ref[:, LZ_G:LZ_V] = _sigmoid(_dot(mixed(5), a1_ref[:, LZ_G:LZ_V])).astype(BF16)
    if has_v:
        z_ref[:, LZ_V:] = _dot(xv, a1_ref[:, LZ_V:]).astype(BF16)
    else:
        z_ref[:, LZ_V:] = jnp.zeros((tm, LZ_COLS - LZ_V), BF16)


def rwkv_mix(h, g, mix, a1_cat, seq, *, has_v, tm=512):
    m, d = h.shape
    tm = _tile(seq, tm)
    return pl.pallas_call(
        functools.partial(_rwkv_mix_kernel, tm=tm, seq=seq, has_v=has_v),
        grid=(m // tm,),
        in_specs=[pl.BlockSpec((tm, d), lambda i: (i, 0)),
                  pl.BlockSpec((8, d), lambda i: (jnp.maximum(i * (tm // 8) - 1, 0), 0)),
                  pl.BlockSpec((1, d), lambda i: (0, 0)),
                  pl.BlockSpec((6, d), lambda i: (0, 0)),
                  pl.BlockSpec((d, LZ_COLS), lambda i: (0, 0))],
        out_specs=[pl.BlockSpec((3, tm, d), lambda i: (0, i, 0)),
                   pl.BlockSpec((tm, LZ_COLS), lambda i: (i, 0))],
        out_shape=[jax.ShapeDtypeStruct((3, m, d), BF16),
                   jax.ShapeDtypeStruct((m, LZ_COLS), BF16)],
        compiler_params=_cparams(("parallel",)),
        name="rwkv_mix",
    )(h, h, g, mix, a1_cat)


def _lora_up_kernel(z_ref, b_ref, bias_ref, *rest, has_v):
    if has_v:
        v_ref, vf_ref, lw_ref, a_ref, gt_ref, vo_ref = rest
    else:
        lw_ref, a_ref, gt_ref = rest
    y = _dot(z_ref[:, LZ_W:LZ_A], b_ref[LZ_W:LZ_A, :]) + bias_ref[0:1, :]
    lw_ref[...] = -jnp.exp(-_softplus(-y) - 0.5)
    y = _dot(z_ref[:, LZ_A:LZ_G], b_ref[LZ_A:LZ_G, :]) + bias_ref[1:2, :]
    a_ref[...] = _sigmoid(y).astype(a_ref.dtype)
    gt_ref[...] = _dot(z_ref[:, LZ_G:LZ_V], b_ref[LZ_G:LZ_V, :]).astype(gt_ref.dtype)
    if has_v:
        y = _dot(z_ref[:, LZ_V:], b_ref[LZ_V:, :]) + bias_ref[2:3, :]
        v = v_ref[...].astype(F32)
        vo_ref[...] = (v + (vf_ref[...].astype(F32) - v) * _sigmoid(y)).astype(vo_ref.dtype)


def lora_up(z, b_cat, bias, v=None, v_first=None, *, tm=512):
    m = z.shape[0]
    d = b_cat.shape[1]
    has_v = v is not None
    tm = _tile(m, tm)
    row = lambda i: (i, 0)
    const = lambda i: (0, 0)
    big = pl.BlockSpec((tm, d), row)
    n_bf = 3 if has_v else 2
    return pl.pallas_call(
        functools.partial(_lora_up_kernel, has_v=has_v),
        grid=(m // tm,),
        in_specs=[pl.BlockSpec((tm, LZ_COLS), row), pl.BlockSpec((LZ_COLS, d), const),
                  pl.BlockSpec((3, d), const)] + ([big, big] if has_v else []),
        out_specs=[big] * (1 + n_bf),
        out_shape=[jax.ShapeDtypeStruct((m, d), F32)] + [jax.ShapeDtypeStruct((m, d), BF16)] * n_bf,
        compiler_params=_cparams(("parallel",)),
        name="lora_up",
    )(z, b_cat, bias, *((v, v_first) if has_v else ()))


def _rwkv_kernel(*refs, chunk, pairs, nsub):
    state_scr = refs[-1]

    @pl.when(pl.program_id(2) == 0)
    def _():
        state_scr[...] = jnp.zeros_like(state_scr)

    for sub in range(nsub):
        _rwkv_chunk(slice(sub * chunk, (sub + 1) * chunk), *refs, chunk=chunk, pairs=pairs)


def _rwkv_chunk(rows, r_ref, k_ref, v_ref, a_ref, lw_ref, gt_ref, kk_ref, ka_ref, rk_ref, lnw_ref, lnb_ref,
                o_ref, state_scr, *, chunk, pairs):
    c = chunk
    ps = range(pairs)
    sls = [slice(p * LANES, (p + 1) * LANES) for p in ps]

    lane =lax.broadcasted_iota(jnp.int32, (c, LANES), 1)
    m0 = lane < RWKV_HEAD
    r2 =lax.broadcasted_iota(jnp.int32, (2 * c, 2 * c), 0)
    c2 = lax.broadcasted_iota(jnp.int32, (2 * c, 2 * c), 1)
    strict2 = c2 < r2
    rw = lax.broadcasted_iota(jnp.int32, (c, 2 * c), 0)
    cw = lax.broadcasted_iota(jnp.int32, (c, 2 * c), 1)
    incl_w = jnp.where(cw >= c, cw - c, cw) <= rw
    zero_blk = jnp.zeros((2 * c, LANES), BF16)

    def seg_sum(x):
        s0 = jnp.sum(jnp.where(m0, x, 0.0), axis=-1, keepdims=True)
        s1 = jnp.sum(jnp.where(m0, 0.0, x), axis=-1, keepdims=True)
        return jnp.where(m0, s0, s1)

    def stack(x):
        return jnp.concatenate([jnp.where(m0, x, 0.0), jnp.where(m0, 0.0, x)], axis=0)

    def bf(x):
        return x.astype(BF16)

    r = [r_ref[rows, s].astype(F32) for s in sls]
    k = [k_ref[rows, s].astype(F32) for s in sls]
    v = [v_ref[rows, s].astype(F32) for s in sls]
    a = [a_ref[rows, s].astype(F32) for s in sls]
    lw = [lw_ref[rows, s] for s in sls]
    kk = [k[p] * kk_ref[:, sls[p]] for p in ps]
    kk = [x * lax.rsqrt(seg_sum(x * x) + 1e-12) for x in kk]
    kh = [k[p] * (1.0 + (a[p] - 1.0) * ka_ref[:, sls[p]]) for p in ps]
    bv = [kk[p] * a[p] for p in ps]

    cum = [_cumsum_rows(x) for x in lw]
    cl = [x[c - 1:c, :] for x in cum]
    e_neg = [jnp.exp(-x) for x in cum]
    e_end = [jnp.exp(cl[p] - cum[p]) for p in ps]
    rg = [r[p] * jnp.exp(cum[p]) for p in ps]
    rgc = [bf(x) for x in rg]
    xs = [bf(stack(kk[p] * jnp.exp(cum[p] - lw[p]))) for p in ps]
    bs = [bf(stack(bv[p] * e_neg[p])) for p in ps]
    ks = [bf(stack(kh[p] * e_neg[p])) for p in ps]
    vs = [bf(stack(v[p])) for p in ps]
    kht = [stack(kh[p] * e_end[p]).T for p in ps]
    bht = [stack(bv[p] * e_end[p]).T for p in ps]

    ab = [_dot_nt(xs[p], jnp.concatenate([bs[p], ks[p]], axis=0)) for p in ps]
    akk = [jnp.where(strict2, x[:, :2 * c], 0.0) for x in ab]
    akv = [bf(jnp.where(strict2, x[:, 2 * c:], 0.0)) for x in ab]
    tinv = [bf(x) for x in _unit_lower_inverse(akk, 6, _dot_bf)]
    w1 = [bf(_dot(akv[p], vs[p])) for p in ps]
    m12 = [bf(_dot(tinv[p], jnp.concatenate([xs[p], w1[p]], axis=1))) for p in ps]
    bb = [_dot_nt(rgc[p], jnp.concatenate([ks[p], bs[p]], axis=0)) for p in ps]
    lhs = [bf(jnp.concatenate([
        jnp.concatenate([jnp.where(incl_w, x[:, :2 * c], 0.0), -jnp.where(incl_w, x[:, 2 * c:], 0.0)], axis=1),
        jnp.concatenate([kht[p], -bht[p]], axis=1)], axis=0)) for p, x in enumerate(bb)]
    rhs = [jnp.concatenate([jnp.concatenate([vs[p], zero_blk], axis=1),
                            jnp.concatenate([m12[p][:, LANES:], m12[p][:, :LANES]], axis=1)], axis=0)
           for p in ps]
    z = [_dot(lhs[p], rhs[p]) for p in ps]
    state = [state_scr[p] for p in ps]
    l2 = [bf(jnp.concatenate([rg[p] + z[p][:c, LANES:], z[p][c:, LANES:]], axis=0)) for p in ps]
    rr = [_dot(l2[p], bf(state[p])) for p in ps]
    for p in ps:
        sl = sls[p]
        state_scr[p] = state[p] * _col_of_row(jnp.exp(cl[p]), LANES) + rr[p][c:] + z[p][c:, :LANES]
        y = rr[p][:c] + z[p][:c, :LANES]
        mu = seg_sum(y) * (1.0 / RWKV_HEAD)
        yc = y - mu
        var = seg_sum(yc * yc) * (1.0 / RWKV_HEAD)
        yn = yc * lax.rsqrt(var + RWKV_LN_EPS) * lnw_ref[:, sl] + lnb_ref[:, sl]
        bonus = seg_sum(r[p] * kh[p] * rk_ref[:, sl]) * v[p]
        o_ref[rows, sl] = ((yn + bonus) * gt_ref[rows, sl].astype(F32)).astype(o_ref.dtype)


def rwkv_recurrence(r, k, v, a, lw, gate, k_k, k_a, r_k, ln_w, ln_b, batch, seq, *, pairs=16, nsub=2):
    c = RWKV_CHUNK
    m, d = r.shape
    nc = seq // (c * nsub)
    width = pairs * LANES
    blk = pl.BlockSpec((c * nsub, width), lambda b, hp, i: (b * nc + i, hp))
    par = pl.BlockSpec((1, width), lambda b, hp, i: (0, hp))
    return pl.pallas_call(
        functools.partial(_rwkv_kernel, chunk=c, pairs=pairs, nsub=nsub),
        grid=(batch, d // width, nc),
        in_specs=[blk] * 6 + [par] * 5,
        out_specs=blk,
        out_shape=jax.ShapeDtypeStruct((m, d), BF16),
        scratch_shapes=[pltpu.VMEM((pairs, LANES, LANES), F32)],
        compiler_params=_cparams(("parallel", "parallel", "arbitrary")),
        name="rwkv_recurrence",
    )(r, k, v, a, lw, gate, k_k, k_a, r_k, ln_w, ln_b)


def _pad_cols(w, n):
    return jnp.pad(w, ((0, 0), (0, n - w.shape[1])))


def _rot_cols(w):
    half = MLA_ROPE // 2
    return jnp.concatenate([-w[..., half:], w[..., :half]], axis=-1)


def _input_projection_weight(w_in):
    o_kr = MLA_Q_LORA + MLA_KV_LORA
    o_g = o_kr + MLA_ROPE
    o_sm = o_g + 4 * GDN_HEADS * GDN_DK
    w_in = w_in.astype(BF16)
    w_kr = w_in[..., o_kr:o_g]
    small = w_in[..., o_sm:]
    small = jnp.pad(small, ((0, 0), (0, 0), (0, LANES - small.shape[-1])))
    return jnp.concatenate([w_in[..., :o_kr], w_in[..., o_g:o_sm], w_kr, _rot_cols(w_kr), small], axis=-1)


def _hybrid_layer(h, ct, st, batch, seq, g_pre, g_post, w_cat, q_norm, w_uq, kv_norm, w_ukv, conv_w, a_log,
                  dt_bias, out_norm, w_out, e):
    d = D_MODEL
    hcat = norm_matmul(h, g_pre.reshape(1, d), w_cat, e, tn=1792, out_dtype=BF16)

    wq = w_uq.reshape(MLA_Q_LORA, MLA_HEADS, MLA_QK)
    wq_rope = wq[..., MLA_NOPE:]
    wq = jnp.concatenate([wq[..., :MLA_NOPE], wq_rope, _rot_cols(wq_rope)], axis=-1)
    wq = wq.reshape(MLA_Q_LORA, MLA_HEADS * MLA_HEAD_PAD).astype(BF16)
    q, k, v = mla_proj(hcat, q_norm.reshape(1, -1), kv_norm.reshape(1, -1), wq, w_ukv.astype(BF16), ct, st)
    mla_out = mla_attention(q, k, v, batch, seq)
    gdn_out = gdn_mixer(hcat, conv_w, a_log, dt_bias, out_norm.reshape(1, -1), batch, seq)
    return matmul_norm_residual([mla_out, gdn_out], w_out, e, g_post.reshape(1, d), h, tn=d)


def _rwkv_layer(h, v_first, batch, seq, g_pre, g_post, mix, w_r, w_k, w_v, w_o, o, w0, w1, w2, a0, a1, a2,
                g1, g2, k_k, k_a, r_k, ln_w, ln_b, vres):
    d = D_MODEL
    row = lambda t: t.reshape(1, d).astype(F32)
    padc = lambda w, n: _pad_cols(w, n)
    padr = lambda w, n: jnp.pad(w, ((0, n - w.shape[0]), (0, 0)))
    has_v = vres is not None
    v0, v1, v2 = vres if has_v else (jnp.zeros((d,), F32), jnp.zeros((d, 0), F32), jnp.zeros((0, d), F32))
    a1_cat = jnp.concatenate([padc(w1, LZ_A - LZ_W), padc(a1, LZ_G - LZ_A), padc(g1, LZ_V - LZ_G),
                              padc(v1, LZ_COLS - LZ_V)], axis=1).astype(BF16)
    b_cat = jnp.concatenate([padr(w2, LZ_A - LZ_W), padr(a2, LZ_G - LZ_A), padr(g2, LZ_V - LZ_G),
                             padr(v2, LZ_COLS - LZ_V)], axis=0).astype(BF16)
    bias = jnp.stack([w0, a0, v0]).astype(F32)
    xs, z = rwkv_mix(h, g_pre.reshape(1, d), mix, a1_cat, seq, has_v=has_v)
    r = matmul(xs, 0, w_r, o, out_dtype=BF16)
    k = matmul(xs, 1, w_k, o, out_dtype=BF16)
    v = matmul(xs, 2, w_v, o, out_dtype=BF16)
    if has_v:
        lw, a, gate, v = lora_up(z, b_cat, bias, v, v_first)
    else:
        lw, a, gate = lora_up(z, b_cat, bias)
        v_first = v
    y = rwkv_recurrence(r, k, v, a, lw, gate, row(k_k), row(k_a), row(r_k), row(ln_w), row(ln_b), batch, seq)
    return matmul_norm_residual([y], w_o, o, g_post.reshape(1, d), h, tn=d), v_first


def _ffn_layer(h, g_pre, g_post, w_gate, w_up, w_down, layer):
    d = D_MODEL
    act = ffn_up(h, g_pre.reshape(1, d), w_gate, w_up, layer)
    return matmul_norm_residual([act], w_down, layer, g_post.reshape(1, d), h, tn=1024)


def _rope_tables(positions):
    inv_freq = 1.0 / (ROPE_THETA ** (jnp.arange(0, MLA_ROPE, 2, dtype=F32) / MLA_ROPE))
    ang = positions.astype(F32).reshape(-1, 1) * inv_freq
    zeros = jnp.zeros((ang.shape[0], LANES - MLA_ROPE), F32)
    cos, sin = jnp.cos(ang), jnp.sin(ang)
    return jnp.concatenate([cos, cos, zeros], axis=1), jnp.concatenate([sin, sin, zeros], axis=1)


def kernel(x, positions, norm_mix_pre, norm_mix_post, norm_ffn_pre, norm_ffn_post, hyb_w_in, mla_q_norm, mla_w_uq, mla_kv_norm, mla_w_ukv, gdn_conv_w, gdn_a_log, gdn_dt_bias, gdn_out_norm, hyb_w_out, rwkv_mix, rwkv_w_r, rwkv_w_k, rwkv_w_v, rwkv_w_o, rwkv_w0, rwkv_w1, rwkv_w2, rwkv_a0, rwkv_a1, rwkv_a2, rwkv_g1, rwkv_g2, rwkv_k_k, rwkv_k_a, rwkv_r_k, rwkv_ln_w, rwkv_ln_b, rwkv_v0, rwkv_v1, rwkv_v2, ffn_w_gate, ffn_w_up, ffn_w_down):
    batch, seq, d = x.shape
    h = x.reshape(batch * seq, d)
    ct, st = _rope_tables(positions)
    bf = lambda w: w.astype(BF16)
    ffn_w_gate, ffn_w_up, ffn_w_down, hyb_w_out = bf(ffn_w_gate), bf(ffn_w_up), bf(ffn_w_down), bf(hyb_w_out)
    rwkv_w_r, rwkv_w_k, rwkv_w_v, rwkv_w_o = bf(rwkv_w_r), bf(rwkv_w_k), bf(rwkv_w_v), bf(rwkv_w_o)
    hyb_w_cat = _input_projection_weight(hyb_w_in)
    v_first = None
    for layer in range(DEPTH):
        if layer % 2 == 0:
            e = layer // 2
            h = _hybrid_layer(h, ct, st, batch, seq, norm_mix_pre[layer], norm_mix_post[layer], hyb_w_cat,
                              mla_q_norm[e], mla_w_uq[e], mla_kv_norm[e], mla_w_ukv[e], gdn_conv_w[e],
                              gdn_a_log[e], gdn_dt_bias[e], gdn_out_norm[e], hyb_w_out, e)
        else:
            o = layer // 2
            vres = None if o == 0 else (rwkv_v0[o - 1], rwkv_v1[o - 1], rwkv_v2[o - 1])
            h, v_first = _rwkv_layer(h, v_first, batch, seq, norm_mix_pre[layer], norm_mix_post[layer],
                                     rwkv_mix[o], rwkv_w_r, rwkv_w_k, rwkv_w_v, rwkv_w_o, o,
                                     rwkv_w0[o], rwkv_w1[o], rwkv_w2[o], rwkv_a0[o], rwkv_a1[o], rwkv_a2[o],
                                     rwkv_g1[o], rwkv_g2[o], rwkv_k_k[o], rwkv_k_a[o], rwkv_r_k[o],
                                     rwkv_ln_w[o], rwkv_ln_b[o], vres)
        h = _ffn_layer(h, norm_ffn_pre[layer], norm_ffn_post[layer], ffn_w_gate, ffn_w_up, ffn_w_down, layer)
    return h.reshape(batch, seq, d)
```

```python
import functools

import jax
import jax.numpy as jnp
import numpy as np
from jax import lax
from jax.experimental import pallas as pl
from jax.experimental.pallas import tpu as pltpu

F32 = jnp.float32
BF16 = jnp.bfloat16
HI = lax.Precision.HIGHEST

D_MODEL = 2048
DEPTH = 4
LANES = 128
VMEM_LIMIT = 56 * 1024 * 1024

MLA_HEADS = 8
MLA_Q_LORA = 512
MLA_KV_LORA = 512
MLA_NOPE = 128
MLA_ROPE = 64
MLA_V = 128
MLA_QK = MLA_NOPE + MLA_ROPE
MLA_HEAD_PAD = 256
ROPE_THETA = 10000.0

GDN_HEADS = 8
GDN_DK = 128
GDN_DV = 128
GDN_CONV = 4
GDN_CHUNK = 128

RWKV_HEAD = 64
RWKV_HEADS = D_MODEL // RWKV_HEAD
RWKV_CHUNK = 64
RWKV_LN_EPS = 64e-5
RMS_EPS = 1e-6

HC_CQ, HC_CKV, HC_GQ, HC_GK, HC_GV, HC_GZ, HC_KR, HC_SM = 0, 512, 1024, 2048, 3072, 4096, 5120, 5248
HC_COLS = 5376


def _cparams(sem):
    return pltpu.CompilerParams(dimension_semantics=sem, vmem_limit_bytes=VMEM_LIMIT)


def _tile(n, pref):
    t = min(n, pref)
    while n % t:
        t -= 8
    return t


def _layer_spec(w, layer, blk, imap):
    return pl.BlockSpec((None,) + blk, lambda *g: (layer,) + imap(*g))


def _dot(a, b, prec=None):
    return jnp.dot(a, b, preferred_element_type=F32, precision=prec)


def _dot_nt(a, b, prec=None):
    return lax.dot_general(a, b, (((1,), (1,)), ((), ())), preferred_element_type=F32, precision=prec)


def _rms(x, g, eps=RMS_EPS):
    return x * lax.rsqrt(jnp.mean(x * x, axis=-1, keepdims=True) + eps) * g


def _sigmoid(x):
    return 1.0 / (1.0 + jnp.exp(-x))


def _silu(x):
    return x * _sigmoid(x)


def _softplus(x):
    return jnp.maximum(x, 0.0) + jnp.log(1.0 + jnp.exp(-jnp.abs(x)))


def _norm_mm_kernel(x_ref, g_ref, w_ref, o_ref, xn_ref):
    @pl.when(pl.program_id(1) == 0)
    def _():
        xn_ref[...] = _rms(x_ref[...], g_ref[...]).astype(BF16)

    o_ref[...] = _dot(xn_ref[...], w_ref[...]).astype(o_ref.dtype)


def norm_matmul(x, g, w, layer, *, tn, out_dtype=F32, tm=1024):
    m, k = x.shape
    n = w.shape[2]
    tm = _tile(m, tm)
    return pl.pallas_call(
        _norm_mm_kernel,
        grid=(m // tm, n // tn),
        in_specs=[pl.BlockSpec((tm, k), lambda i, j: (i, 0)),
                  pl.BlockSpec((1, k), lambda i, j: (0, 0)),
                  _layer_spec(w, layer, (k, tn), lambda i, j: (0, j))],
        out_specs=pl.BlockSpec((tm, tn), lambda i, j: (i, j)),
        out_shape=jax.ShapeDtypeStruct((m, n), out_dtype),
        scratch_shapes=[pltpu.VMEM((tm, k), BF16)],
        compiler_params=_cparams(("parallel", "arbitrary")),
        name="norm_matmul",
    )(x, g, w)


def _ffn_up_kernel(x_ref, g_ref, wg_ref, wu_ref, o_ref, xn_ref):
    @pl.when(pl.program_id(1) == 0)
    def _():
        xn_ref[...] = _rms(x_ref[...], g_ref[...]).astype(BF16)

    xn = xn_ref[...]
    gate = _dot(xn, wg_ref[...])
    up = _dot(xn, wu_ref[...])
    o_ref[...] = (_silu(gate) * up).astype(o_ref.dtype)


def ffn_up(h, g, wg, wu, layer, *, tm=1024, tn=512):
    m, k = h.shape
    n = wg.shape[2]
    tm = _tile(m, tm)
    tn = _tile(n, tn)
    return pl.pallas_call(
        _ffn_up_kernel,
        grid=(m // tm, n // tn),
        in_specs=[pl.BlockSpec((tm, k), lambda i, j: (i, 0)),
                  pl.BlockSpec((1, k), lambda i, j: (0, 0)),
                  _layer_spec(wg, layer, (k, tn), lambda i, j: (0, j)),
                  _layer_spec(wu, layer, (k, tn), lambda i, j: (0, j))],
        out_specs=pl.BlockSpec((tm, tn), lambda i, j: (i, j)),
        out_shape=jax.ShapeDtypeStruct((m, n), BF16),
        scratch_shapes=[pltpu.VMEM((tm, k), BF16)],
        compiler_params=_cparams(("parallel", "arbitrary")),
        name="ffn_up",
    )(h, g, wg, wu)


def _mm_norm_res_kernel(*refs, tn, nb, n_a):
    a_refs = refs[:n_a]
    w_ref, g_ref, h_ref, o_ref, f_ref = refs[n_a:]
    j = pl.program_id(1)
    k0 = 0
    f = None
    for a_ref in a_refs:
        part = _dot(a_ref[...], w_ref[k0:k0 + a_ref.shape[1], :])
        f = part if f is None else f + part
        k0 += a_ref.shape[1]
    f_ref[j] = f

    @pl.when(j == nb - 1)
    def _():
        ssq = jnp.sum(f_ref[0] * f_ref[0], axis=-1, keepdims=True)
        for b in range(1, nb):
            ssq = ssq + jnp.sum(f_ref[b] * f_ref[b], axis=-1, keepdims=True)
        inv = lax.rsqrt(ssq * (1.0 / (nb * tn)) + RMS_EPS)
        for b in range(nb):
            cs = slice(b * tn, (b + 1) * tn)
            o_ref[:, cs] = h_ref[:, cs] + f_ref[b] * inv * g_ref[:, cs]


def matmul_norm_residual(a_parts, w, layer, g, h, *, tm=512, tn=512):
    m = a_parts[0].shape[0]
    _, k, n = w.shape
    tm = _tile(m, tm)
    tn = _tile(n, tn)
    nb = n // tn
    return pl.pallas_call(
        functools.partial(_mm_norm_res_kernel, tn=tn, nb=nb, n_a=len(a_parts)),
        grid=(m // tm, nb),
        in_specs=[pl.BlockSpec((tm, a.shape[1]), lambda i, j: (i, 0)) for a in a_parts] + [
                  _layer_spec(w, layer, (k, tn), lambda i, j: (0, j)),
                  pl.BlockSpec((1, n), lambda i, j: (0, 0)),
                  pl.BlockSpec((tm, n), lambda i, j: (i, 0))],
        out_specs=pl.BlockSpec((tm, n), lambda i, j: (i, 0)),
        out_shape=jax.ShapeDtypeStruct((m, n), F32),
        scratch_shapes=[pltpu.VMEM((nb, tm, tn), F32)],
        compiler_params=_cparams(("parallel", "arbitrary")),
        name="matmul_norm_residual",
    )(*a_parts, w, g, h)


def _mm_kernel(a_ref, w_ref, o_ref):
    o_ref[...] = _dot(a_ref[...], w_ref[...]).astype(o_ref.dtype)


def matmul(a, idx, w, layer, *, tm=1024, tn=1024, out_dtype=F32):
    _, m, k = a.shape
    n = w.shape[2]
    tm = _tile(m, tm)
    tn = _tile(n, tn)
    return pl.pallas_call(
        _mm_kernel,
        grid=(m // tm, n // tn),
        in_specs=[pl.BlockSpec((None, tm, k), lambda i, j: (idx, i, 0)),
                  _layer_spec(w, layer, (k, tn), lambda i, j: (0, j))],
        out_specs=pl.BlockSpec((tm, tn), lambda i, j: (i, j)),
        out_shape=jax.ShapeDtypeStruct((m, n), out_dtype),
        compiler_params=_cparams(("parallel", "arbitrary")),
        name="matmul",
    )(a, w)


def _rope_group(grp, ct, st):
    return grp * ct + pltpu.roll(grp, 64, 1) * st


def _mla_proj_kernel(cq_ref, ckv_ref, kr_ref, qn_ref, kvn_ref, wq_ref, wkv_ref, ct_ref, st_ref,
                     q_ref, k_ref, v_ref):
    scale = MLA_QK ** -0.5 * np.log2(np.e)
    ct = ct_ref[...]
    st = st_ref[...]
    xq = _rms(cq_ref[...].astype(F32), qn_ref[...]).astype(BF16)
    xkv = _rms(ckv_ref[...].astype(F32), kvn_ref[...]).astype(BF16)
    kr = _rope_group(kr_ref[...].astype(F32), ct, st).astype(BF16)
    for h in range(MLA_HEADS):
        c0 = h * MLA_HEAD_PAD
        yq = _dot(xq, wq_ref[:, c0:c0 + MLA_HEAD_PAD])
        q_ref[:, c0:c0 + LANES] = (yq[:, :LANES] * scale).astype(BF16)
        q_ref[:, c0 + LANES:c0 + 2 * LANES] = (_rope_group(yq[:, LANES:], ct, st) * scale).astype(BF16)
        ykv = _dot(xkv, wkv_ref[:, c0:c0 + MLA_HEAD_PAD])
        k_ref[:, c0:c0 + LANES] = ykv[:, :LANES].astype(BF16)
        k_ref[:, c0 + LANES:c0 + 2 * LANES] = kr
        v_ref[:, h * MLA_V:(h + 1) * MLA_V] = ykv[:, LANES:].astype(BF16)


def mla_proj(hcat, q_norm, kv_norm, wq, wkv, ct, st, *, tm=1024):
    m = hcat.shape[0]
    tm = _tile(m, tm)
    hp = MLA_HEADS * MLA_HEAD_PAD
    row = lambda i: (i, 0)
    const = lambda i: (0, 0)
    return pl.pallas_call(
        _mla_proj_kernel,
        grid=(m // tm,),
        in_specs=[pl.BlockSpec((tm, MLA_Q_LORA), lambda i: (i, HC_CQ // MLA_Q_LORA)),
                  pl.BlockSpec((tm, MLA_KV_LORA), lambda i: (i, HC_CKV // MLA_KV_LORA)),
                  pl.BlockSpec((tm, LANES), lambda i: (i, HC_KR // LANES)),
                  pl.BlockSpec((1, MLA_Q_LORA), const),
                  pl.BlockSpec((1, MLA_KV_LORA), const),
                  pl.BlockSpec((MLA_Q_LORA, hp), const),
                  pl.BlockSpec((MLA_KV_LORA, hp), const),
                  pl.BlockSpec((tm, LANES), row),
                  pl.BlockSpec((tm, LANES), row)],
        out_specs=[pl.BlockSpec((tm, hp), row),
                   pl.BlockSpec((tm, hp), row),
                   pl.BlockSpec((tm, MLA_HEADS * MLA_V), row)],
        out_shape=[jax.ShapeDtypeStruct((m, hp), BF16),
                   jax.ShapeDtypeStruct((m, hp), BF16),
                   jax.ShapeDtypeStruct((m, MLA_HEADS * MLA_V), BF16)],
        compiler_params=_cparams(("parallel",)),
        name="mla_proj",
    )(hcat, hcat, hcat, q_norm, kv_norm, wq, wkv, ct, st)


def _attn_kernel(q_ref, k_ref, v_ref, o_ref, *, tq, tk, hpg):
    qi = pl.program_id(2)
    hs = range(hpg)
    q = [q_ref[:, h * MLA_HEAD_PAD:(h + 1) * MLA_HEAD_PAD] for h in hs]

    def step(ki, carry, masked):
        start = pl.multiple_of(ki * tk, tk)
        s = [_dot_nt(q[h], k_ref[pl.ds(start, tk), h * MLA_HEAD_PAD:(h + 1) * MLA_HEAD_PAD]) for h in hs]
        if masked is not None:
            row = lax.broadcasted_iota(jnp.int32, (tq, tk), 0)
            col = lax.broadcasted_iota(jnp.int32, (tq, tk), 1) + masked * tk
            s = [jnp.where(col <= row, x, -1e30) for x in s]
        m_new = [jnp.maximum(carry[h][0], jnp.max(s[h], axis=-1, keepdims=True)) for h in hs]
        alpha = [jnp.exp2(carry[h][0] - m_new[h]) for h in hs]
        p = [jnp.exp2(s[h] - m_new[h]) for h in hs]
        l_new = [alpha[h] * carry[h][1] + jnp.sum(p[h], axis=-1, keepdims=True) for h in hs]
        pv = [_dot(p[h].astype(BF16), v_ref[pl.ds(start, tk), h * MLA_V:(h + 1) * MLA_V]) for h in hs]
        return tuple((m_new[h], l_new[h], alpha[h] * carry[h][2] + pv[h]) for h in hs)

    init = tuple((jnp.full((tq, 1), -1e30, F32), jnp.zeros((tq, 1), F32), jnp.zeros((tq, MLA_V), F32))
                 for _ in hs)
    per_q = tq // tk
    carry = lax.fori_loop(0, qi * per_q, lambda ki, c: step(ki, c, None), init)
    for j in range(per_q):
        carry = step(qi * per_q + j, carry, j)
    for h in hs:
        o_ref[:, h * MLA_V:(h + 1) * MLA_V] = (carry[h][2] / carry[h][1]).astype(o_ref.dtype)


def mla_attention(q, k, v, batch, seq, *, tq=1024, tk=1024, hpg=2):
    tq = _tile(seq, tq)
    tk = _tile(tq, tk)
    nq = seq // tq
    return pl.pallas_call(
        functools.partial(_attn_kernel, tq=tq, tk=tk, hpg=hpg),
        grid=(batch, MLA_HEADS // hpg, nq),
        in_specs=[pl.BlockSpec((tq, hpg * MLA_HEAD_PAD), lambda b, h, i: (b * nq + i, h)),
                  pl.BlockSpec((seq, hpg * MLA_HEAD_PAD), lambda b, h, i: (b, h)),
                  pl.BlockSpec((seq, hpg * MLA_V), lambda b, h, i: (b, h))],
        out_specs=pl.BlockSpec((tq, hpg * MLA_V), lambda b, h, i: (b * nq + i, h)),
        out_shape=jax.ShapeDtypeStruct((batch * seq, MLA_HEADS * MLA_V), BF16),
        compiler_params=_cparams(("parallel", "parallel", "arbitrary")),
        name="mla_attention",
    )(q, k, v)


def _split2(x):
    hi = x.astype(BF16)
    return hi, (x - hi.astype(F32)).astype(BF16)


def _dot_x3(a, b):
    ah, al = _split2(a)
    bh, bl = _split2(b)
    return _dot(jnp.concatenate([ah, al, ah], axis=1), jnp.concatenate([bh, bh, bl], axis=0))


def _dot_nt_x3(a, b):
    ah, al = _split2(a)
    bh, bl = _split2(b)
    return _dot_nt(jnp.concatenate([ah, al, ah], axis=1), jnp.concatenate([bh, bh, bl], axis=1))


def _dot_bf(a, b):
    return _dot(a.astype(BF16), b.astype(BF16))


def _unit_lower_inverse(a_list, levels, dot):
    n = a_list[0].shape[0]
    eye = (lax.broadcasted_iota(jnp.int32, (n, n), 0) == lax.broadcasted_iota(jnp.int32, (n, n), 1)).astype(F32)
    p = [-a for a in a_list]
    r = [eye + x for x in p]
    p = [dot(x, x) for x in p]
    for _ in range(levels - 2):
        s = [dot(jnp.concatenate([ri, pi], axis=0), pi) for ri, pi in zip(r, p)]
        r = [ri + si[:n] for ri, si in zip(r, s)]
        p = [si[n:] for si in s]
    return [ri + dot(ri, pi) for ri, pi in zip(r, p)]


def _cumsum_rows(x):
    c = x.shape[0]
    row = lax.broadcasted_iota(jnp.int32, x.shape, 0)
    s = 1
    while s < c:
        x = x + jnp.where(row >= s, pltpu.roll(x, s, 0), 0.0)
        s *= 2
    return x


def _col_of_row(row_vec, n):
    eye = lax.broadcasted_iota(jnp.int32, (n, n), 0) == lax.broadcasted_iota(jnp.int32, (n, n), 1)
    return jnp.sum(jnp.where(eye, jnp.broadcast_to(row_vec, (n, n)), 0.0), axis=1, keepdims=True)


def _gdn_kernel(*refs, chunk, nsub):
    xq_scr, xk_scr, xv_scr, state_scr = refs[-4:]

    @pl.when(pl.program_id(1) == 0)
    def _():
        for scr in (xq_scr, xk_scr, xv_scr):
            scr[0:8, :] = jnp.zeros((8, scr.shape[1]), F32)
        state_scr[...] = jnp.zeros_like(state_scr)

    for sub in range(nsub):
        _gdn_chunk(slice(sub * chunk, (sub + 1) * chunk), *refs, chunk=chunk)


def _gdn_chunk(rows, gq_ref, gk_ref, gv_ref, gz_ref, sm_ref, cw_ref, alog_ref, dtb_ref, on_ref, o_ref,
               xq_scr, xk_scr, xv_scr, state_scr, *, chunk):
    c = chunk
    heads = range(GDN_HEADS)
    hs = [slice(h * LANES, (h + 1) * LANES) for h in heads]

    def conv_silu(x_ref, scr, widx):
        scr[8:8 + c, :] = x_ref[rows, :].astype(F32)
        acc = cw_ref[widx, GDN_CONV - 1:GDN_CONV, :] * scr[8:8 + c, :]
        for j in range(GDN_CONV - 1):
            s = GDN_CONV - 1 - j
            acc = acc + cw_ref[widx, j:j + 1, :] * scr[8 - s:8 - s + c, :]
        scr[0:8, :] = scr[c:c + 8, :]
        return _silu(acc)

    def l2n(x):
        return x * lax.rsqrt(jnp.sum(x * x, axis=-1, keepdims=True) + 1e-6)

    def bf(x):
        return x.astype(BF16)

    qa = conv_silu(gq_ref, xq_scr, 0)
    ka = conv_silu(gk_ref, xk_scr, 1)
    va = conv_silu(gv_ref, xv_scr, 2)
    q = [l2n(qa[:, s]) * (GDN_DK ** -0.5) for s in hs]
    k = [l2n(ka[:, s]) for s in hs]
    v = [va[:, s] for s in hs]

    sm = sm_ref[rows, :].astype(F32)
    lane = lax.broadcasted_iota(jnp.int32, (c, LANES), 1)
    beta = [_sigmoid(jnp.sum(jnp.where(lane == h, sm, 0.0), axis=-1, keepdims=True)) for h in heads]
    ga = [jnp.sum(jnp.where(lane == GDN_HEADS + h, sm, 0.0), axis=-1, keepdims=True) for h in heads]
    g = [-jnp.exp(alog_ref[:, hs[h]]) * _softplus(jnp.broadcast_to(ga[h], (c, LANES)) + dtb_ref[:, hs[h]])
         for h in heads]

    ri = lax.broadcasted_iota(jnp.int32, (c, c), 0)
    cj = lax.broadcasted_iota(jnp.int32, (c, c), 1)
    gam = [_cumsum_rows(x) for x in g]
    decay = [jnp.exp(jnp.where(cj <= ri, x - x.T, -1e30)) for x in gam]
    eg = [jnp.exp(x) for x in gam]
    gl = [x[c - 1:c, :] for x in gam]
    kb = [k[h] * beta[h] for h in heads]
    vb = [v[h] * beta[h] for h in heads]

    a_mat = [jnp.where(cj < ri, _dot_nt_x3(kb[h], k[h]) * decay[h], 0.0) for h in heads]
    tinv = _unit_lower_inverse(a_mat, 7, _dot_x3)
    rhs = [jnp.concatenate([kb[h] * eg[h], vb[h]], axis=1) for h in heads]
    wu = [bf(_dot_x3(tinv[h], rhs[h])) for h in heads]
    attn = [_dot_nt(bf(q[h]), bf(k[h])) * decay[h] for h in heads]
    k_dec = [k[h] * jnp.exp(gl[h] - gam[h]) for h in heads]
    lhs = [bf(jnp.concatenate([attn[h], k_dec[h].T], axis=0)) for h in heads]
    z = [_dot(lhs[h], wu[h]) for h in heads]
    state = [state_scr[h] for h in heads]
    l2 = [bf(jnp.concatenate([q[h] * eg[h] - z[h][:c, :LANES], -z[h][c:, :LANES]], axis=0)) for h in heads]
    rr = [_dot(l2[h], bf(state[h])) for h in heads]
    for h in heads:
        state_scr[h] = state[h] * jnp.exp(gl[h]) + rr[h][c:] + z[h][c:, LANES:]
        o = rr[h][:c] + z[h][:c, LANES:]
        o_ref[rows, hs[h]] = (_rms(o, on_ref[...]) * _silu(gz_ref[rows, hs[h]].astype(F32))).astype(o_ref.dtype)


def gdn_mixer(hcat, conv_w, a_log, dt_bias, out_norm, batch, seq, *, nsub=2):
    c = GDN_CHUNK
    nsub = nsub if seq % (c * nsub) == 0 else 1
    assert c == LANES and seq % c == 0
    nc = seq // (c * nsub)
    rows = c * nsub
    width = GDN_HEADS * LANES
    blk = lambda off: pl.BlockSpec((rows, width), lambda b, i, off=off: (b * nc + i, off // width))
    const2 = lambda b, i: (0, 0)
    cw = conv_w.reshape(GDN_CONV, 3, width).transpose(1, 0, 2)
    alog = jnp.repeat(a_log.astype(F32), LANES).reshape(1, width)
    dtb = jnp.repeat(dt_bias.astype(F32), LANES).reshape(1, width)
    return pl.pallas_call(
        functools.partial(_gdn_kernel, chunk=c, nsub=nsub),
        grid=(batch, nc),
        in_specs=[blk(HC_GQ), blk(HC_GK), blk(HC_GV), blk(HC_GZ),
                  pl.BlockSpec((rows, LANES), lambda b, i: (b * nc + i, HC_SM // LANES)),
                  pl.BlockSpec((3, GDN_CONV, width), lambda b, i: (0, 0, 0)),
                  pl.BlockSpec((1, width), const2),
                  pl.BlockSpec((1, width), const2),
                  pl.BlockSpec((1, GDN_DV), const2)],
        out_specs=pl.BlockSpec((rows, width), lambda b, i: (b * nc + i, 0)),
        out_shape=jax.ShapeDtypeStruct((batch * seq, width), BF16),
        scratch_shapes=[pltpu.VMEM((c + 8, width), F32)] * 3 + [pltpu.VMEM((GDN_HEADS, GDN_DK, GDN_DV), F32)],
        compiler_params=_cparams(("parallel", "arbitrary")),
        name="gdn_mixer",
    )(hcat, hcat, hcat, hcat, hcat, cw, alog, dtb, out_norm)


LZ_W, LZ_A, LZ_G, LZ_V, LZ_COLS = 0, 128, 256, 512, 640


def _rwkv_mix_kernel(h_ref, hp_ref, g_ref, mix_ref, a1_ref, o_ref, z_ref, *, tm, seq, has_v):
    i = pl.program_id(0)
    g = g_ref[...]
    xn = _rms(h_ref[...], g)
    prev_last = _rms(hp_ref[...], g)[7:8, :]
    prev_last = jnp.where((i * tm) % seq == 0, 0.0, prev_last)
    rows = lax.broadcasted_iota(jnp.int32, xn.shape, 0)
    xprev = jnp.where(rows == 0, prev_last, pltpu.roll(xn, 1, 0))
    xx = xprev - xn

    def mixed(j):
        return (xn + xx * mix_ref[j:j + 1, :]).astype(BF16)

    o_ref[0] = mixed(0)
    o_ref[1] = mixed(2)
    xv = mixed(3)
    o_ref[2] = xv
    z_ref[:, LZ_W:LZ_A] = jnp.tanh(_dot(mixed(1), a1_ref[:, LZ_W:LZ_A])).astype(BF16)
    z_ref[:, LZ_A:LZ_G] = _dot(mixed(4), a1_ref[:, LZ_A:LZ_G]).astype(BF16)
    z_ref[:, LZ_G:LZ_V] = _sigmoid(_dot(mixed(5), a1_ref[:, LZ_G:LZ_V])).astype(BF16)
    if has_v:
        z_ref[:, LZ_V:] = _dot(xv, a1_ref[:, LZ_V:]).astype(BF16)
    else:
        z_ref[:, LZ_V:] = jnp.zeros((tm, LZ_COLS - LZ_V), BF16)


def rwkv_mix(h, g, mix, a1_cat, seq, *, has_v, tm=512):
    m, d = h.shape
    tm = _tile(seq, tm)
    return pl.pallas_call(
        functools.partial(_rwkv_mix_kernel, tm=tm, seq=seq, has_v=has_v),
        grid=(m // tm,),
        in_specs=[pl.BlockSpec((tm, d), lambda i: (i, 0)),
                  pl.BlockSpec((8, d), lambda i: (jnp.maximum(i * (tm // 8) - 1, 0), 0)),
                  pl.BlockSpec((1, d), lambda i: (0, 0)),
                  pl.BlockSpec((6, d), lambda i: (0, 0)),
                  pl.BlockSpec((d, LZ_COLS), lambda i: (0, 0))],
        out_specs=[pl.BlockSpec((3, tm, d), lambda i: (0, i, 0)),
                   pl.BlockSpec((tm, LZ_COLS), lambda i: (i, 0))],
        out_shape=[jax.ShapeDtypeStruct((3, m, d), BF16),
                   jax.ShapeDtypeStruct((m, LZ_COLS), BF16)],
        compiler_params=_cparams(("parallel",)),
        name="rwkv_mix",
    )(h, h, g, mix, a1_cat)


def _lora_up_kernel(z_ref, b_ref, bias_ref, *rest, has_v):
    if has_v:
        v_ref, vf_ref, lw_ref, a_ref, gt_ref, vo_ref = rest
    else:
        lw_ref, a_ref, gt_ref = rest
    y = _dot(z_ref[:, LZ_W:LZ_A], b_ref[LZ_W:LZ_A, :]) + bias_ref[0:1, :]
    lw_ref[...] = -jnp.exp(-_softplus(-y) - 0.5)
    y = _dot(z_ref[:, LZ_A:LZ_G], b_ref[LZ_A:LZ_G, :]) + bias_ref[1:2, :]
    a_ref[...] = _sigmoid(y).astype(a_ref.dtype)
    gt_ref[...] = _dot(z_ref[:, LZ_G:LZ_V], b_ref[LZ_G:LZ_V, :]).astype(gt_ref.dtype)
    if has_v:
        y = _dot(z_ref[:, LZ_V:], b_ref[LZ_V:, :]) + bias_ref[2:3, :]
        v = v_ref[...].astype(F32)
        vo_ref[...] = (v + (vf_ref[...].astype(F32) - v) * _sigmoid(y)).astype(vo_ref.dtype)


def lora_up(z, b_cat, bias, v=None, v_first=None, *, tm=512):
    m = z.shape[0]
    d = b_cat.shape[1]
    has_v = v is not None
    tm = _tile(m, tm)
    row = lambda i: (i, 0)
    const = lambda i: (0, 0)
    big = pl.BlockSpec((tm, d), row)
    n_bf = 3 if has_v else 2
    return pl.pallas_call(
        functools.partial(_lora_up_kernel, has_v=has_v),
        grid=(m // tm,),
        in_specs=[pl.BlockSpec((tm, LZ_COLS), row), pl.BlockSpec((LZ_COLS, d), const),
                  pl.BlockSpec((3, d), const)] + ([big, big] if has_v else []),
        out_specs=[big] * (1 + n_bf),
        out_shape=[jax.ShapeDtypeStruct((m, d), F32)] + [jax.ShapeDtypeStruct((m, d), BF16)] * n_bf,
        compiler_params=_cparams(("parallel",)),
        name="lora_up",
    )(z, b_cat, bias, *((v, v_first) if has_v else ()))


def _rwkv_kernel(*refs, chunk, pairs, nsub):
    state_scr = refs[-1]

    @pl.when(pl.program_id(2) == 0)
    def _():
        state_scr[...] = jnp.zeros_like(state_scr)

    for sub in range(nsub):
        _rwkv_chunk(slice(sub * chunk, (sub + 1) * chunk), *refs, chunk=chunk, pairs=pairs)


def _rwkv_chunk(rows, r_ref, k_ref, v_ref, a_ref, lw_ref, gt_ref, kk_ref, ka_ref, rk_ref, lnw_ref, lnb_ref,
                o_ref, state_scr, *, chunk, pairs):
    c = chunk
    ps = range(pairs)
    sls = [slice(p * LANES, (p + 1) * LANES) for p in ps]

    lane =lax.broadcasted_iota(jnp.int32, (c, LANES), 1)
    m0 = lane < RWKV_HEAD
    r2 =lax.broadcasted_iota(jnp.int32, (2 * c, 2 * c), 0)
    c2 = lax.broadcasted_iota(jnp.int32, (2 * c, 2 * c), 1)
    strict2 = c2 < r2
    rw = lax.broadcasted_iota(jnp.int32, (c, 2 * c), 0)
    cw = lax.broadcasted_iota(jnp.int32, (c, 2 * c), 1)
    incl_w = jnp.where(cw >= c, cw - c, cw) <= rw
    zero_blk = jnp.zeros((2 * c, LANES), BF16)

    def seg_sum(x):
        s0 = jnp.sum(jnp.where(m0, x, 0.0), axis=-1, keepdims=True)
        s1 = jnp.sum(jnp.where(m0, 0.0, x), axis=-1, keepdims=True)
        return jnp.where(m0, s0, s1)

    def stack(x):
        return jnp.concatenate([jnp.where(m0, x, 0.0), jnp.where(m0, 0.0, x)], axis=0)

    def bf(x):
        return x.astype(BF16)

    r = [r_ref[rows, s].astype(F32) for s in sls]
    k = [k_ref[rows, s].astype(F32) for s in sls]
    v = [v_ref[rows, s].astype(F32) for s in sls]
    a = [a_ref[rows, s].astype(F32) for s in sls]
    lw = [lw_ref[rows, s] for s in sls]
    kk = [k[p] * kk_ref[:, sls[p]] for p in ps]
    kk = [x * lax.rsqrt(seg_sum(x * x) + 1e-12) for x in kk]
    kh = [k[p] * (1.0 + (a[p] - 1.0) * ka_ref[:, sls[p]]) for p in ps]
    bv = [kk[p] * a[p] for p in ps]

    cum = [_cumsum_rows(x) for x in lw]
    cl = [x[c - 1:c, :] for x in cum]
    e_neg = [jnp.exp(-x) for x in cum]
    e_end = [jnp.exp(cl[p] - cum[p]) for p in ps]
    rg = [r[p] * jnp.exp(cum[p]) for p in ps]
    rgc = [bf(x) for x in rg]
    xs = [bf(stack(kk[p] * jnp.exp(cum[p] - lw[p]))) for p in ps]
    bs = [bf(stack(bv[p] * e_neg[p])) for p in ps]
    ks = [bf(stack(kh[p] * e_neg[p])) for p in ps]
    vs = [bf(stack(v[p])) for p in ps]
    kht = [stack(kh[p] * e_end[p]).T for p in ps]
    bht = [stack(bv[p] * e_end[p]).T for p in ps]

    ab = [_dot_nt(xs[p], jnp.concatenate([bs[p], ks[p]], axis=0)) for p in ps]
    akk = [jnp.where(strict2, x[:, :2 * c], 0.0) for x in ab]
    akv = [bf(jnp.where(strict2, x[:, 2 * c:], 0.0)) for x in ab]
    tinv = [bf(x) for x in _unit_lower_inverse(akk, 6, _dot_bf)]
    w1 = [bf(_dot(akv[p], vs[p])) for p in ps]
    m12 = [bf(_dot(tinv[p], jnp.concatenate([xs[p], w1[p]], axis=1))) for p in ps]
    bb = [_dot_nt(rgc[p], jnp.concatenate([ks[p], bs[p]], axis=0)) for p in ps]
    lhs = [bf(jnp.concatenate([
        jnp.concatenate([jnp.where(incl_w, x[:, :2 * c], 0.0), -jnp.where(incl_w, x[:, 2 * c:], 0.0)], axis=1),
        jnp.concatenate([kht[p], -bht[p]], axis=1)], axis=0)) for p, x in enumerate(bb)]
    rhs = [jnp.concatenate([jnp.concatenate([vs[p], zero_blk], axis=1),
                            jnp.concatenate([m12[p][:, LANES:], m12[p][:, :LANES]], axis=1)], axis=0)
           for p in ps]
    z = [_dot(lhs[p], rhs[p]) for p in ps]
    state = [state_scr[p] for p in ps]
    l2 = [bf(jnp.concatenate([rg[p] + z[p][:c, LANES:], z[p][c:, LANES:]], axis=0)) for p in ps]
    rr = [_dot(l2[p], bf(state[p])) for p in ps]
    for p in ps:
        sl = sls[p]
        state_scr[p] = state[p] * _col_of_row(jnp.exp(cl[p]), LANES) + rr[p][c:] + z[p][c:, :LANES]
        y = rr[p][:c] + z[p][:c, :LANES]
        mu = seg_sum(y) * (1.0 / RWKV_HEAD)
        yc = y - mu
        var = seg_sum(yc * yc) * (1.0 / RWKV_HEAD)
        yn = yc * lax.rsqrt(var + RWKV_LN_EPS) * lnw_ref[:, sl] + lnb_ref[:, sl]
        bonus = seg_sum(r[p] * kh[p] * rk_ref[:, sl]) * v[p]
        o_ref[rows, sl] = ((yn + bonus) * gt_ref[rows, sl].astype(F32)).astype(o_ref.dtype)


def rwkv_recurrence(r, k, v, a, lw, gate, k_k, k_a, r_k, ln_w, ln_b, batch, seq, *, pairs=16, nsub=2):
    c = RWKV_CHUNK
    m, d = r.shape
    nc = seq // (c * nsub)
    width = pairs * LANES
    blk = pl.BlockSpec((c * nsub, width), lambda b, hp, i: (b * nc + i, hp))
    par = pl.BlockSpec((1, width), lambda b, hp, i: (0, hp))
    return pl.pallas_call(
        functools.partial(_rwkv_kernel, chunk=c, pairs=pairs, nsub=nsub),
        grid=(batch, d // width, nc),
        in_specs=[blk] * 6 + [par] * 5,
        out_specs=blk,
        out_shape=jax.ShapeDtypeStruct((m, d), BF16),
        scratch_shapes=[pltpu.VMEM((pairs, LANES, LANES), F32)],
        compiler_params=_cparams(("parallel", "parallel", "arbitrary")),
        name="rwkv_recurrence",
    )(r, k, v, a, lw, gate, k_k, k_a, r_k, ln_w, ln_b)


def _pad_cols(w, n):
    return jnp.pad(w, ((0, 0), (0, n - w.shape[1])))


def _rot_cols(w):
    half = MLA_ROPE // 2
    return jnp.concatenate([-w[..., half:], w[..., :half]], axis=-1)


def _input_projection_weight(w_in):
    o_kr = MLA_Q_LORA + MLA_KV_LORA
    o_g = o_kr + MLA_ROPE
    o_sm = o_g + 4 * GDN_HEADS * GDN_DK
    w_in = w_in.astype(BF16)
    w_kr = w_in[..., o_kr:o_g]
    small = w_in[..., o_sm:]
    small = jnp.pad(small, ((0, 0), (0, 0), (0, LANES - small.shape[-1])))
    return jnp.concatenate([w_in[..., :o_kr], w_in[..., o_g:o_sm], w_kr, _rot_cols(w_kr), small], axis=-1)


def _hybrid_layer(h, ct, st, batch, seq, g_pre, g_post, w_cat, q_norm, w_uq, kv_norm, w_ukv, conv_w, a_log,
                  dt_bias, out_norm, w_out, e):
    d = D_MODEL
    hcat = norm_matmul(h, g_pre.reshape(1, d), w_cat, e, tn=1792, out_dtype=BF16)

    wq = w_uq.reshape(MLA_Q_LORA, MLA_HEADS, MLA_QK)
    wq_rope = wq[..., MLA_NOPE:]
    wq = jnp.concatenate([wq[..., :MLA_NOPE], wq_rope, _rot_cols(wq_rope)], axis=-1)
    wq = wq.reshape(MLA_Q_LORA, MLA_HEADS * MLA_HEAD_PAD).astype(BF16)
    q, k, v = mla_proj(hcat, q_norm.reshape(1, -1), kv_norm.reshape(1, -1), wq, w_ukv.astype(BF16), ct, st)
    mla_out = mla_attention(q, k, v, batch, seq)
    gdn_out = gdn_mixer(hcat, conv_w, a_log, dt_bias, out_norm.reshape(1, -1), batch, seq)
    return matmul_norm_residual([mla_out, gdn_out], w_out, e, g_post.reshape(1, d), h, tn=d)


def _rwkv_layer(h, v_first, batch, seq, g_pre, g_post, mix, w_r, w_k, w_v, w_o, o, w0, w1, w2, a0, a1, a2,
                g1, g2, k_k, k_a, r_k, ln_w, ln_b, vres):
    d = D_MODEL
    row = lambda t: t.reshape(1, d).astype(F32)
    padc = lambda w, n: _pad_cols(w, n)
    padr = lambda w, n: jnp.pad(w, ((0, n - w.shape[0]), (0, 0)))
    has_v = vres is not None
    v0, v1, v2 = vres if has_v else (jnp.zeros((d,), F32), jnp.zeros((d, 0), F32), jnp.zeros((0, d), F32))
    a1_cat = jnp.concatenate([padc(w1, LZ_A - LZ_W), padc(a1, LZ_G - LZ_A), padc(g1, LZ_V - LZ_G),
                              padc(v1, LZ_COLS - LZ_V)], axis=1).astype(BF16)
    b_cat = jnp.concatenate([padr(w2, LZ_A - LZ_W), padr(a2, LZ_G - LZ_A), padr(g2, LZ_V - LZ_G),
                             padr(v2, LZ_COLS - LZ_V)], axis=0).astype(BF16)
    bias = jnp.stack([w0, a0, v0]).astype(F32)
    xs, z = rwkv_mix(h, g_pre.reshape(1, d), mix, a1_cat, seq, has_v=has_v)
    r = matmul(xs, 0, w_r, o, out_dtype=BF16)
    k = matmul(xs, 1, w_k, o, out_dtype=BF16)
    v = matmul(xs, 2, w_v, o, out_dtype=BF16)
    if has_v:
        lw, a, gate, v = lora_up(z, b_cat, bias, v, v_first)
    else:
        lw, a, gate = lora_up(z, b_cat, bias)
        v_first = v
    y = rwkv_recurrence(r, k, v, a, lw, gate, row(k_k), row(k_a), row(r_k), row(ln_w), row(ln_b), batch, seq)
    return matmul_norm_residual([y], w_o, o, g_post.reshape(1, d), h, tn=d), v_first


def _ffn_layer(h, g_pre, g_post, w_gate, w_up, w_down, layer):
    d = D_MODEL
    act = ffn_up(h, g_pre.reshape(1, d), w_gate, w_up, layer)
    return matmul_norm_residual([act], w_down, layer, g_post.reshape(1, d), h, tn=1024)


def _rope_tables(positions):
    inv_freq = 1.0 / (ROPE_THETA ** (jnp.arange(0, MLA_ROPE, 2, dtype=F32) / MLA_ROPE))
    ang = positions.astype(F32).reshape(-1, 1) * inv_freq
    zeros = jnp.zeros((ang.shape[0], LANES - MLA_ROPE), F32)
    cos, sin = jnp.cos(ang), jnp.sin(ang)
    return jnp.concatenate([cos, cos, zeros], axis=1), jnp.concatenate([sin, sin, zeros], axis=1)


def kernel(x, positions, norm_mix_pre, norm_mix_post, norm_ffn_pre, norm_ffn_post, hyb_w_in, mla_q_norm, mla_w_uq, mla_kv_norm, mla_w_ukv, gdn_conv_w, gdn_a_log, gdn_dt_bias, gdn_out_norm, hyb_w_out, rwkv_mix, rwkv_w_r, rwkv_w_k, rwkv_w_v, rwkv_w_o, rwkv_w0, rwkv_w1, rwkv_w2, rwkv_a0, rwkv_a1, rwkv_a2, rwkv_g1, rwkv_g2, rwkv_k_k, rwkv_k_a, rwkv_r_k, rwkv_ln_w, rwkv_ln_b, rwkv_v0, rwkv_v1, rwkv_v2, ffn_w_gate, ffn_w_up, ffn_w_down):
    batch, seq, d = x.shape
    h = x.reshape(batch * seq, d)
    ct, st = _rope_tables(positions)
    bf = lambda w: w.astype(BF16)
    ffn_w_gate, ffn_w_up, ffn_w_down, hyb_w_out = bf(ffn_w_gate), bf(ffn_w_up), bf(ffn_w_down), bf(hyb_w_out)
    rwkv_w_r, rwkv_w_k, rwkv_w_v, rwkv_w_o = bf(rwkv_w_r), bf(rwkv_w_k), bf(rwkv_w_v), bf(rwkv_w_o)
    hyb_w_cat = _input_projection_weight(hyb_w_in)
    v_first = None
    for layer in range(DEPTH):
        if layer % 2 == 0:
            e = layer // 2
            h = _hybrid_layer(h, ct, st, batch, seq, norm_mix_pre[layer], norm_mix_post[layer], hyb_w_cat,
                              mla_q_norm[e], mla_w_uq[e], mla_kv_norm[e], mla_w_ukv[e], gdn_conv_w[e],
                              gdn_a_log[e], gdn_dt_bias[e], gdn_out_norm[e], hyb_w_out, e)
        else:
            o = layer // 2
            vres = None if o == 0 else (rwkv_v0[o - 1], rwkv_v1[o - 1], rwkv_v2[o - 1])
            h, v_first = _rwkv_layer(h, v_first, batch, seq, norm_mix_pre[layer], norm_mix_post[layer],
                                     rwkv_mix[o], rwkv_w_r, rwkv_w_k, rwkv_w_v, rwkv_w_o, o,
                                     rwkv_w0[o], rwkv_w1[o], rwkv_w2[o], rwkv_a0[o], rwkv_a1[o], rwkv_a2[o],
                                     rwkv_g1[o], rwkv_g2[o], rwkv_k_k[o], rwkv_k_a[o], rwkv_r_k[o],
                                     rwkv_ln_w[o], rwkv_ln_b[o], vres)
        h = _ffn_layer(h, norm_ffn_pre[layer], norm_ffn_post[layer], ffn_w_gate, ffn_w_up, ffn_w_down, layer)
    return h.reshape(batch, seq, d)
```
